```python
import math
import jax, jax.numpy as jnp
from jax import lax
import numpy as np

D_MODEL = 1024
BATCH = 8
SEQ = 2048
DEPTH = 2

GRID_W = 64
CTX_LEN = 256
N_EVEN = (DEPTH + 1) // 2
N_ODD = DEPTH // 2
N_MOD = 6
EPS = 1e-6

HY_C = D_MODEL // 2
HY_ORDER = 2
HY_EMB = 33
HY_BANDS = (HY_EMB - 1) // 2
HY_FILT_W = 64
HY_SHORT = 3
HY_DECAY_SHORT_PCT = 0.3
HY_DECAY_LONG_PCT = 1.5
HY_TARGET = 1e-2
HEAD_DIM = 64
N_Q_HEADS = (D_MODEL // 2) // HEAD_DIM
N_KV_HEADS = N_Q_HEADS // 4
GQA_GROUP = N_Q_HEADS // N_KV_HEADS
WINDOW = 128
ATTN_BLK = 128
ROPE_BASE = 10000.0
AB_IN = 3 * HY_C + (N_Q_HEADS + 2 * N_KV_HEADS) * HEAD_DIM
AB_OUT = HY_C + N_Q_HEADS * HEAD_DIM

HG_HEADS = 8
HG_DK = D_MODEL // HG_HEADS
HG_DV = D_MODEL // HG_HEADS
HG_F = HG_HEADS * HG_DK
HG_IN = 3 * HG_F + 2 * HG_HEADS * HG_DV
HG_CHUNK = 64

PEER_HEADS = 8
PEER_TOPK = 16
N_KEYS = 128
N_EXPERTS = N_KEYS * N_KEYS
PEER_QDIM = 256
PEER_TOK_BLK = 128

kernel_name = "hybrid_hyena_swa_hgrn2_peer_dit"

F32 = jnp.float32


def rmsnorm(x, g):
    x32 = x.astype(F32)
    y = x32 * lax.rsqrt(jnp.mean(x32 * x32, axis=-1, keepdims=True) + EPS)
    return y.astype(x.dtype) * g


def modulate(x, shift, scale):
    return x * (1 + scale) + shift


def axial_rope_cos_sin(L):
    rows = L // GRID_W
    r = jnp.broadcast_to(jnp.arange(rows)[:, None], (rows, GRID_W)).reshape(-1).astype(F32)
    col = jnp.broadcast_to(jnp.arange(GRID_W)[None, :], (rows, GRID_W)).reshape(-1).astype(F32)
    nf = HEAD_DIM // 4
    inv = ROPE_BASE ** (-jnp.arange(nf, dtype=F32) / nf)
    ang = jnp.concatenate([r[:, None] * inv, col[:, None] * inv], axis=-1)
    return jnp.cos(ang), jnp.sin(ang)


def apply_rope(x, cos, sin):
    shp = x.shape
    nf = HEAD_DIM // 4
    xa = x.reshape(shp[:-1] + (2, 2, nf))
    bshape = (1, shp[1]) + (1,) * (x.ndim - 3) + (2, nf)
    c = cos.reshape(bshape).astype(x.dtype)
    s = sin.reshape(bshape).astype(x.dtype)
    a, b = xa[..., 0, :], xa[..., 1, :]
    out = jnp.stack([a * c - b * s, b * c + a * s], axis=-2)
    return out.reshape(shp)


def hyena_filters(L, w1, b1, w2, b2, w3, freq):
    t = jnp.linspace(0.0, 1.0, L, dtype=F32)[:, None]
    w = 2.0 * math.pi * jnp.arange(L, dtype=F32)[:, None] / L
    f = jnp.linspace(1e-4, HY_BANDS - 1, HY_BANDS, dtype=F32)[None, :]
    z = jnp.concatenate([t, jnp.cos(f * w), -jnp.sin(f * w)], axis=-1)
    fr = freq.astype(F32)
    hdn = jnp.sin(fr * (z @ w1.astype(F32) + b1.astype(F32)))
    hdn = jnp.sin(fr * (hdn @ w2.astype(F32) + b2.astype(F32)))
    h = (hdn @ w3.astype(F32)).reshape(L, HY_ORDER, 2, HY_C)
    max_decay = math.log(HY_TARGET) / HY_DECAY_SHORT_PCT
    min_decay = math.log(HY_TARGET) / HY_DECAY_LONG_PCT
    deltas = jnp.abs(jnp.linspace(min_decay, max_decay, HY_C, dtype=F32))
    h = h * jnp.exp(-t * deltas)[:, None, None, :]
    return h / jnp.sum(jnp.abs(h), axis=(0, 2), keepdims=True)


def hyena_mixer(u, conv_w, conv_b, w1, b1, w2, b2, w3, freq, d_skip):
    L = u.shape[1]
    up = jnp.pad(u, ((0, 0), (1, 1), (0, 0)))
    u = up[:, :-2] * conv_w[0] + up[:, 1:-1] * conv_w[1] + up[:, 2:] * conv_w[2] + conv_b
    v, x1, x2 = jnp.split(u, 3, axis=-1)
    h = hyena_filters(L, w1, b1, w2, b2, w3, freq)
    two_sided = jnp.concatenate([h[:, :, 0], jnp.zeros((1, HY_ORDER, HY_C), F32), h[:0:-1, :, 1]], axis=0)
    hf = jnp.fft.rfft(two_sided, axis=0)
    z = v
    for n, gate in enumerate((x1, x2)):
        zf = jnp.fft.rfft(z.astype(F32), n=2 * L, axis=1)
        conv = jnp.fft.irfft(zf * hf[None, :, n, :], n=2 * L, axis=1)[:, :L]
        z = gate * (conv.astype(z.dtype) + d_skip[n] * z)
    return z


def window_attention(q, k, v, kc, vc, sink):
    B, L = q.shape[:2]
    nb = L // ATTN_BLK
    scale = HEAD_DIM ** -0.5
    qb = q.reshape(B, nb, ATTN_BLK, N_KV_HEADS, GQA_GROUP, HEAD_DIM)

    def band(t):
        tp = jnp.pad(t, ((0, 0), (ATTN_BLK, ATTN_BLK), (0, 0), (0, 0)))
        tp = tp.reshape(B, nb + 2, ATTN_BLK, N_KV_HEADS, HEAD_DIM)
        return jnp.concatenate([tp[:, :-2], tp[:, 1:-1], tp[:, 2:]], axis=2)

    kw, vw = band(k), band(v)
    s_win = jnp.einsum('bnqhgd,bnkhd->bnhgqk', qb, kw).astype(F32) * scale
    qpos = jnp.arange(nb)[:, None, None] * ATTN_BLK + jnp.arange(ATTN_BLK)[None, :, None]
    kpos = jnp.arange(nb)[:, None, None] * ATTN_BLK - ATTN_BLK + jnp.arange(3 * ATTN_BLK)[None, None, :]
    valid = (kpos >= 0) & (kpos < L) & (jnp.abs(qpos - kpos) <= WINDOW)
    s_win = jnp.where(valid[None, :, None, None], s_win, -jnp.inf)
    s_ctx = jnp.einsum('bnqhgd,bchd->bnhgqc', qb, kc).astype(F32) * scale
    s_sink = jnp.broadcast_to(sink.astype(F32)[None, None, :, :, None, None], s_win.shape[:-1] + (1,))
    p = jax.nn.softmax(jnp.concatenate([s_sink, s_win, s_ctx], axis=-1), axis=-1).astype(q.dtype)
    nw = 3 * ATTN_BLK
    o = (jnp.einsum('bnhgqk,bnkhd->bnqhgd', p[..., 1:1 + nw], vw)
         + jnp.einsum('bnhgqc,bchd->bnqhgd', p[..., 1 + nw:], vc))
    return o.reshape(B, L, N_Q_HEADS * HEAD_DIM)


def context_attention(qc, kc, vc, sink):
    B, Lc = qc.shape[:2]
    s = jnp.einsum('bqhgd,bkhd->bhgqk', qc, kc).astype(F32) * (HEAD_DIM ** -0.5)
    s_sink = jnp.broadcast_to(sink.astype(F32)[None, :, :, None, None], s.shape[:-1] + (1,))
    p = jax.nn.softmax(jnp.concatenate([s_sink, s], axis=-1), axis=-1).astype(qc.dtype)
    o = jnp.einsum('bhgqk,bkhd->bqhgd', p[..., 1:], vc)
    return o.reshape(B, Lc, N_Q_HEADS * HEAD_DIM)


def hyena_attention_mixer(a_lat, a_ctx, w_in, w_out, conv_w, conv_b, fw1, fb1, fw2, fb2, fw3, ffreq,
                          d_skip, sink, need_ctx):
    B, L, _ = a_lat.shape
    o_q = 3 * HY_C
    o_k = o_q + N_Q_HEADS * HEAD_DIM
    o_v = o_k + N_KV_HEADS * HEAD_DIM

    def split(p):
        Lx = p.shape[1]
        q = p[..., o_q:o_k].reshape(B, Lx, N_KV_HEADS, GQA_GROUP, HEAD_DIM)
        k = p[..., o_k:o_v].reshape(B, Lx, N_KV_HEADS, HEAD_DIM)
        v = p[..., o_v:].reshape(B, Lx, N_KV_HEADS, HEAD_DIM)
        return p[..., :o_q], q, k, v

    hy_l, q_l, k_l, v_l = split(a_lat @ w_in)
    hy_c, q_c, k_c, v_c = split(a_ctx @ w_in)
    cos, sin = axial_rope_cos_sin(L)
    q_l = apply_rope(q_l, cos, sin)
    k_l = apply_rope(k_l, cos, sin)
    sink = sink.reshape(N_KV_HEADS, GQA_GROUP)
    hy_p = (conv_w, conv_b, fw1, fb1, fw2, fb2, fw3, ffreq, d_skip)
    y_lat = jnp.concatenate([hyena_mixer(hy_l, *hy_p),
                             window_attention(q_l, k_l, v_l, k_c, v_c, sink)], axis=-1) @ w_out
    y_ctx = None
    if need_ctx:
        y_ctx = jnp.concatenate([hyena_mixer(hy_c, *hy_p),
                                 context_attention(q_c, k_c, v_c, sink)], axis=-1) @ w_out
    return y_lat, y_ctx


def gla_chunk_scan(q, k, v, logf, S0):
    B, H, L, dk = q.shape
    dv = v.shape[-1]
    nc = L // HG_CHUNK
    tri = jnp.tril(jnp.ones((HG_CHUNK, HG_CHUNK), dtype=bool))

    def to_chunks(t):
        return jnp.moveaxis(t.reshape(B, H, nc, HG_CHUNK, t.shape[-1]), 2, 0)

    def step(S, inp):
        qc, kc, vc, gc = inp
        b = jnp.cumsum(gc, axis=2)
        o_inter = jnp.einsum('bhtk,bhkv->bhtv', qc * jnp.exp(b), S)
        diff = jnp.where(tri[:, :, None], b[:, :, :, None, :] - b[:, :, None, :, :], -jnp.inf)
        A = jnp.einsum('bhtk,bhtsk,bhsk->bhts', qc, jnp.exp(diff), kc)
        o = o_inter + jnp.einsum('bhts,bhsv->bhtv', A, vc)
        b_last = b[:, :, -1:, :]
        S = jnp.exp(b_last[:, :, 0, :, None]) * S + jnp.einsum('bhsk,bhsv->bhkv', kc * jnp.exp(b_last - b), vc)
        return S, o

    S, o = lax.scan(step, S0, (to_chunks(q), to_chunks(k), to_chunks(v), to_chunks(logf)))
    return S, jnp.moveaxis(o, 0, 2).reshape(B, H, L, dv)


def hgrn2_mixer(a_lat, a_ctx, w_in, w_out, lb, onorm_g, need_ctx):
    def proj(a):
        p = (a @ w_in).astype(F32)
        B, Lx, _ = p.shape
        q, f_f, f_b, i, g = jnp.split(p, [HG_F, 2 * HG_F, 3 * HG_F, 3 * HG_F + HG_HEADS * HG_DV], axis=-1)

        def heads(t, d):
            return t.reshape(B, Lx, HG_HEADS, d).transpose(0, 2, 1, 3)

        def gates(f, lbd):
            fg = lbd + (1.0 - lbd) * jax.nn.sigmoid(f)
            return heads(1.0 - fg, HG_DK), heads(jnp.log(fg), HG_DK)

        return heads(jax.nn.silu(q), HG_DK), heads(i, HG_DV), g, gates(f_f, lb[0]), gates(f_b, lb[1])

    def flip(t):
        return t[:, :, ::-1]

    def readout(o, g, like):
        B, H, Lx, _ = o.shape
        o = rmsnorm(o.transpose(0, 2, 1, 3), onorm_g.astype(F32)).reshape(B, Lx, H * HG_DV)
        return (o * jax.nn.silu(g)).astype(like.dtype) @ w_out

    B = a_lat.shape[0]
    S0 = jnp.zeros((B, HG_HEADS, HG_DK, HG_DV), F32)
    qc, ic, gc, (kcf, lcf), (kcb, lcb) = proj(a_ctx)
    S_cf, o_cf = gla_chunk_scan(qc, kcf, ic, lcf, S0)
    S_cb, o_cb = gla_chunk_scan(flip(qc), flip(kcb), flip(ic), flip(lcb), S0)
    ql, il, gl, (klf, llf), (klb, llb) = proj(a_lat)
    _, o_lf = gla_chunk_scan(ql, klf, il, llf, S_cf)
    _, o_lb = gla_chunk_scan(flip(ql), flip(klb), flip(il), flip(llb), S_cb)
    y_lat = readout(o_lf + flip(o_lb), gl, a_lat)
    y_ctx = readout(o_cf + flip(o_cb), gc, a_ctx) if need_ctx else None
    return y_lat, y_ctx


def peer_ffn(x, wq, keys, U, V):
    B, L, D = x.shape
    K = PEER_TOPK
    q = (x @ wq).reshape(B, L, PEER_HEADS, 2, PEER_QDIM // 2)
    s = jnp.einsum('blhpd,hpnd->blhpn', q, keys).astype(F32)
    s1, i1 = lax.top_k(s[..., 0, :], K)
    s2, i2 = lax.top_k(s[..., 1, :], K)
    cand = (s1[..., :, None] + s2[..., None, :]).reshape(B, L, PEER_HEADS, K * K)
    top_s, top_j = lax.top_k(cand, K)
    e1 = jnp.take_along_axis(i1, top_j // K, axis=-1)
    e2 = jnp.take_along_axis(i2, top_j % K, axis=-1)
    idx = e1 * N_KEYS + e2
    gate = jax.nn.softmax(top_s, axis=-1).astype(x.dtype)
    nblk = (B * L) // PEER_TOK_BLK
    xb = x.reshape(nblk, PEER_TOK_BLK, D)
    ib = idx.reshape(nblk, PEER_TOK_BLK, PEER_HEADS * K)
    gb = gate.reshape(nblk, PEER_TOK_BLK, PEER_HEADS * K)

    def block(args):
        xt, it, gt = args
        act = jax.nn.gelu(jnp.einsum('tkd,td->tk', U[it], xt), approximate=False) * gt
        return jnp.einsum('tk,tkd->td', act, V[it])

    return lax.map(block, (xb, ib, gb)).reshape(B, L, D)


def setup_inputs(seed: int = 0) -> dict:
    key = jax.random.key(seed)
    ks = iter(jax.random.split(key, 40))
    D = D_MODEL

    def nrm(shape, s):
        return jax.random.normal(next(ks), shape, F32) * s

    return {
        "x": nrm((BATCH, SEQ, D), 1.0),
        "c": nrm((BATCH, D), 1.0),
        "ctx": nrm((BATCH, CTX_LEN, D), 1.0),
        "c_ctx": nrm((D,), 1.0),
        "ada_w": nrm((DEPTH, D, N_MOD * D), 0.5 * D ** -0.5),
        "ada_b": nrm((DEPTH, N_MOD * D), 0.01),
        "norm1_g": 1.0 + nrm((DEPTH, D), 0.05),
        "norm2_g": 1.0 + nrm((DEPTH, D), 0.05),
        "final_g": 1.0 + nrm((D,), 0.05),
        "ab_w_in": nrm((N_EVEN, D, AB_IN), D ** -0.5),
        "ab_w_out": nrm((N_EVEN, AB_OUT, D), AB_OUT ** -0.5),
        "hy_conv_w": nrm((N_EVEN, HY_SHORT, 3 * HY_C), HY_SHORT ** -0.5),
        "hy_conv_b": nrm((N_EVEN, 3 * HY_C), 0.01),
        "hy_filt_w1": nrm((N_EVEN, HY_EMB, HY_FILT_W), HY_EMB ** -0.5),
        "hy_filt_b1": nrm((N_EVEN, HY_FILT_W), 0.02),
        "hy_filt_w2": nrm((N_EVEN, HY_FILT_W, HY_FILT_W), HY_FILT_W ** -0.5),
        "hy_filt_b2": nrm((N_EVEN, HY_FILT_W), 0.02),
        "hy_filt_w3": nrm((N_EVEN, HY_FILT_W, HY_ORDER * 2 * HY_C), HY_FILT_W ** -0.5),
        "hy_filt_freq": 1.0 + nrm((N_EVEN, HY_FILT_W), 0.1),
        "hy_skip": nrm((N_EVEN, HY_ORDER, HY_C), 0.5),
        "attn_sink": nrm((N_EVEN, N_Q_HEADS), 0.5),
        "hg_w_in": nrm((N_ODD, D, HG_IN), D ** -0.5),
        "hg_w_out": nrm((N_ODD, HG_HEADS * HG_DV, D), (HG_HEADS * HG_DV) ** -0.5),
        "hg_lb_logits": nrm((DEPTH, 2, HG_F), 0.5),
        "hg_onorm_g": 1.0 + nrm((N_ODD, HG_DV), 0.05),
        "peer_wq": nrm((DEPTH, D, PEER_HEADS * PEER_QDIM), D ** -0.5),
        "peer_keys": nrm((DEPTH, PEER_HEADS, 2, N_KEYS, PEER_QDIM // 2), (PEER_QDIM // 2) ** -0.5),
        "peer_u": nrm((DEPTH, N_EXPERTS, D), D ** -0.5),
        "peer_v": nrm((DEPTH, N_EXPERTS, D), 1.0),
    }


def reference(x, c, ctx, c_ctx, ada_w, ada_b, norm1_g, norm2_g, final_g, ab_w_in, ab_w_out,
              hy_conv_w, hy_conv_b, hy_filt_w1, hy_filt_b1, hy_filt_w2, hy_filt_b2, hy_filt_w3,
              hy_filt_freq, hy_skip, attn_sink, hg_w_in, hg_w_out, hg_lb_logits, hg_onorm_g,
              peer_wq, peer_keys, peer_u, peer_v):
    lb_p = jax.nn.softmax(hg_lb_logits.astype(F32), axis=0)
    lb_all = jnp.cumsum(lb_p, axis=0) - lb_p[0:1]
    h_lat, h_ctx = x, ctx
    for l in range(DEPTH):
        need_ctx = l < DEPTH - 1
        mod_l = (jax.nn.silu(c) @ ada_w[l] + ada_b[l])[:, None, :]
        mod_c = jax.nn.silu(c_ctx) @ ada_w[l] + ada_b[l]
        sh1, sc1, g1, sh2, sc2, g2 = jnp.split(mod_l, N_MOD, axis=-1)
        sh1c, sc1c, g1c, sh2c, sc2c, g2c = jnp.split(mod_c, N_MOD, axis=-1)
        a_lat = modulate(rmsnorm(h_lat, norm1_g[l]), sh1, sc1)
        a_ctx = modulate(rmsnorm(h_ctx, norm1_g[l]), sh1c, sc1c)
        j = l // 2
        if l % 2 == 0:
            y_lat, y_ctx = hyena_attention_mixer(
                a_lat, a_ctx, ab_w_in[j], ab_w_out[j], hy_conv_w[j], hy_conv_b[j],
                hy_filt_w1[j], hy_filt_b1[j], hy_filt_w2[j], hy_filt_b2[j], hy_filt_w3[j],
                hy_filt_freq[j], hy_skip[j], attn_sink[j], need_ctx)
        else:
            y_lat, y_ctx = hgrn2_mixer(a_lat, a_ctx, hg_w_in[j], hg_w_out[j], lb_all[l].astype(a_lat.dtype),
                                       hg_onorm_g[j], need_ctx)
        h_lat = h_lat + g1 * y_lat
        h_lat = h_lat + g2 * peer_ffn(modulate(rmsnorm(h_lat, norm2_g[l]), sh2, sc2),
                                      peer_wq[l], peer_keys[l], peer_u[l], peer_v[l])
        if need_ctx:
            h_ctx = h_ctx + g1c * y_ctx
            h_ctx = h_ctx + g2c * peer_ffn(modulate(rmsnorm(h_ctx, norm2_g[l]), sh2c, sc2c),
                                           peer_wq[l], peer_keys[l], peer_u[l], peer_v[l])
    return rmsnorm(h_lat, final_g)
```

```python
import functools
import math

import jax
import jax.numpy as jnp
from jax import lax
from jax.experimental import pallas as pl
from jax.experimental.pallas import tpu as pltpu

F32 = jnp.float32
BF16 = jnp.bfloat16
EPS = 1e-6

HY_C = 512
HY_EMB = 33
HY_EMB_PAD = 40
HY_FILT_W = 64
HY_DECAY_SHORT_PCT = 0.3
HY_DECAY_LONG_PCT = 1.5
HY_TARGET = 1e-2
HEAD_DIM = 64
N_Q_HEADS = 8
N_KV_HEADS = 2
GQA_GROUP = 4
WINDOW = 128
ATTN_BLK = 128
GRID_W = 64
ROPE_BASE = 10000.0
HG_HEADS = 8
HG_DK = 128
PEER_HEADS = 8
PEER_TOPK = 16
N_KEYS = 128

LANES = 128
VMEM_LIMIT_BYTES = 56 * 1024 * 1024
ROW_TILE = 512
GLA_CHUNK = 16
GLA_SEQ_BLOCK = 256
PEER_PREP_TOK = 256
PEER_TOK = 512
PEER_ROWS = 8
PEER_CW = 256
NEG_BIG = -1e30


def _cparams(sem):
    return pltpu.CompilerParams(dimension_semantics=sem, vmem_limit_bytes=VMEM_LIMIT_BYTES)


def _sigmoid(x):
    return 1.0 / (1.0 + jnp.exp(-x))


def _tile(n, t):
    t = min(n, t)
    assert n % t == 0, (n, t)
    return t


def _ada_body(c_ref, w_ref, b_ref, o_ref):
    c = c_ref[...]
    s = c * _sigmoid(c)
    o_ref[...] = jnp.dot(s.astype(BF16), w_ref[...].astype(BF16), preferred_element_type=F32) + b_ref[...]


def ada_mod(c16, w, b):
    d, n = w.shape
    tn = _tile(n, 1536)
    return pl.pallas_call(
        _ada_body,
        out_shape=jax.ShapeDtypeStruct((c16.shape[0], n), F32),
        grid=(n // tn,),
        in_specs=[pl.BlockSpec(c16.shape, lambda j: (0, 0)),
                  pl.BlockSpec((d, tn), lambda j: (0, j)),
                  pl.BlockSpec((1, tn), lambda j: (0, j))],
        out_specs=pl.BlockSpec((c16.shape[0], tn), lambda j: (0, j)),
        compiler_params=_cparams(("arbitrary",)),
        name="ada_mod",
    )(c16, w, b)


def _norm_mod(x, g, sh, sc):
    y = x * lax.rsqrt(jnp.mean(x * x, axis=-1, keepdims=True) + EPS)
    return (y * g) * (1.0 + sc) + sh


def _nm_matmul_body(h_ref, g_ref, sh_ref, sc_ref, w_ref, p_ref, a_scr):
    @pl.when(pl.program_id(2) == 0)
    def _():
        a_scr[...] = _norm_mod(h_ref[0], g_ref[...], sh_ref[0], sc_ref[0]).astype(BF16)

    p_ref[0] = jnp.dot(a_scr[...], w_ref[...], preferred_element_type=F32)


def norm_mod_matmul(h, g, sh, sc, w_bf):
    bx, lx, d = h.shape
    n = w_bf.shape[1]
    tm = _tile(lx, ROW_TILE)
    tn = _tile(n, 1280 if n % 1280 == 0 else (1152 if n % 1152 == 0 else 1024))
    return pl.pallas_call(
        _nm_matmul_body,
        out_shape=jax.ShapeDtypeStruct((bx, lx, n), F32),
        grid=(bx, lx // tm, n // tn),
        in_specs=[pl.BlockSpec((1, tm, d), lambda b, i, j: (b, i, 0)),
                  pl.BlockSpec((1, d), lambda b, i, j: (0, 0)),
                  pl.BlockSpec((1, 1, d), lambda b, i, j: (b, 0, 0)),
                  pl.BlockSpec((1, 1, d), lambda b, i, j: (b, 0, 0)),
                  pl.BlockSpec((d, tn), lambda b, i, j: (0, j))],
        out_specs=pl.BlockSpec((1, tm, tn), lambda b, i, j: (b, i, j)),
        scratch_shapes=[pltpu.VMEM((tm, d), BF16)],
        compiler_params=_cparams(("parallel", "parallel", "arbitrary")),
        name="norm_mod_matmul",
    )(h, g, sh, sc, w_bf)


def _nm_only_body(h_ref, g_ref, sh_ref, sc_ref, a_ref):
    a_ref[0] = _norm_mod(h_ref[0], g_ref[...], sh_ref[0], sc_ref[0]).astype(BF16)


def norm_mod(h, g, sh, sc):
    bx, lx, d = h.shape
    tm = _tile(lx, ROW_TILE)
    return pl.pallas_call(
        _nm_only_body,
        out_shape=jax.ShapeDtypeStruct((bx, lx, d), BF16),
        grid=(bx, lx // tm),
        in_specs=[pl.BlockSpec((1, tm, d), lambda b, i: (b, i, 0)),
                  pl.BlockSpec((1, d), lambda b, i: (0, 0)),
                  pl.BlockSpec((1, 1, d), lambda b, i: (b, 0, 0)),
                  pl.BlockSpec((1, 1, d), lambda b, i: (b, 0, 0))],
        out_specs=pl.BlockSpec((1, tm, d), lambda b, i: (b, i, 0)),
        compiler_params=_cparams(("parallel", "parallel")),
        name="norm_mod",
    )(h, g, sh, sc)


def _mm_body(a_ref, b_ref, o_ref):
    o_ref[...] = jnp.dot(a_ref[...], b_ref[...], preferred_element_type=F32)


def matmul_bf16(a, b):
    m, k = a.shape
    n = b.shape[1]
    tm = _tile(m, 1024)
    tn = _tile(n, 1024)
    return pl.pallas_call(
        _mm_body,
        out_shape=jax.ShapeDtypeStruct((m, n), F32),
        grid=(m // tm, n // tn),
        in_specs=[pl.BlockSpec((tm, k), lambda i, j: (i, 0)),
                  pl.BlockSpec((k, tn), lambda i, j: (0, j))],
        out_specs=pl.BlockSpec((tm, tn), lambda i, j: (i, j)),
        compiler_params=_cparams(("parallel", "parallel")),
        name="matmul_bf16",
    )(a, b)


def _filter_body(z_ref, w1_ref, b1_ref, w2_ref, b2_ref, w3_ref, fr_ref, dec_ref, o_ref):
    hi = lax.Precision.HIGHEST
    fr = fr_ref[...]
    hdn = jnp.sin(fr * (jnp.dot(z_ref[...], w1_ref[...], precision=hi, preferred_element_type=F32) + b1_ref[...]))
    hdn = jnp.sin(fr * (jnp.dot(hdn, w2_ref[...], precision=hi, preferred_element_type=F32) + b2_ref[...]))
    h = jnp.dot(hdn, w3_ref[...], precision=hi, preferred_element_type=F32)
    dec = dec_ref[...]
    c = dec.shape[1]
    h0 = h[:, :c] * dec
    h1 = h[:, c:] * dec
    nrm = jnp.sum(jnp.abs(h0) + jnp.abs(h1), axis=0, keepdims=True)
    inv = 1.0 / nrm
    ri = lax.broadcasted_iota(jnp.int32, h1.shape, 0)
    o_ref[:, :c] = (h0 * inv).astype(BF16)
    o_ref[:, c:] = jnp.where(ri == 0, 0.0, h1 * inv).astype(BF16)


def hyena_filters(zfeat, w1p, b1, w2, b2, w3, freq, decay):
    l = zfeat.shape[0]
    c = decay.shape[1]
    n_order = w3.shape[1] // (2 * c)
    full = lambda shape: pl.BlockSpec(shape, lambda o: (0,) * len(shape))
    return pl.pallas_call(
        _filter_body,
        out_shape=jax.ShapeDtypeStruct((l, n_order * 2 * c), BF16),
        grid=(n_order,),
        in_specs=[full(zfeat.shape), full(w1p.shape), full(b1.shape), full(w2.shape), full(b2.shape),
                  pl.BlockSpec((w3.shape[0], 2 * c), lambda o: (0, o)),
                  full(freq.shape), full(decay.shape)],
        out_specs=pl.BlockSpec((l, 2 * c), lambda o: (0, o)),
        compiler_params=_cparams(("arbitrary",)),
        name="hyena_filters",
    )(zfeat, w1p, b1, w2, b2, w3, freq, decay)


def _short_conv_body(u_ref, w_ref, b_ref, o_ref):
    u = u_ref[0]
    l = u.shape[0]
    ri = lax.broadcasted_iota(jnp.int32, u.shape, 0)
    prev = jnp.where(ri == 0, 0.0, pltpu.roll(u, 1, 0))
    nxt = jnp.where(ri == l - 1, 0.0, pltpu.roll(u, l - 1, 0))
    w = w_ref[...]
    o_ref[0] = prev * w[0:1] + u * w[1:2] + nxt * w[2:3] + b_ref[...]


def short_conv(p, conv_w, conv_b):
    bx, lx, _ = p.shape
    c3 = conv_w.shape[1]
    tc = _tile(c3, 512)
    return pl.pallas_call(
        _short_conv_body,
        out_shape=jax.ShapeDtypeStruct((bx, lx, c3), F32),
        grid=(bx, c3 // tc),
        in_specs=[pl.BlockSpec((1, lx, tc), lambda b, j: (b, 0, j)),
                  pl.BlockSpec((3, tc), lambda b, j: (0, j)),
                  pl.BlockSpec((1, tc), lambda b, j: (0, j))],
        out_specs=pl.BlockSpec((1, lx, tc), lambda b, j: (b, 0, j)),
        compiler_params=_cparams(("parallel", "parallel")),
        name="hyena_short_conv",
    )(p, conv_w, conv_b)


def _hy_fwd_body(z_ref, f_ref, h0_ref, h1_ref, y_ref):
    z = z_ref[0].astype(BF16)
    kt = f_ref.shape[1]
    zre = jnp.dot(f_ref[0], z, preferred_element_type=F32)
    zim = jnp.dot(f_ref[1], z, preferred_element_type=F32)
    k0 = (pl.program_id(0) * kt + lax.broadcasted_iota(jnp.int32, zre.shape, 0)) == 0
    hre = h0_ref[0] + h1_ref[0]
    him = jnp.where(k0, h0_ref[1] + h1_ref[1], h0_ref[1] - h1_ref[1])
    yre = jnp.where(k0, zre * hre, zre * hre - zim * him)
    yim = jnp.where(k0, zim * him, zre * him + zim * hre)
    y_ref[0, 0] = yre.astype(BF16)
    y_ref[0, 1] = yim.astype(BF16)


def hyena_fwd(z_src, z_col, f3, fh3, order, c):
    bx, lx, _ = z_src.shape
    kt = _tile(lx, 512)
    return pl.pallas_call(
        _hy_fwd_body,
        out_shape=jax.ShapeDtypeStruct((bx, 2, lx, c), BF16),
        grid=(lx // kt, bx),
        in_specs=[pl.BlockSpec((1, lx, c), lambda k, b: (b, 0, z_col)),
                  pl.BlockSpec((2, kt, lx), lambda k, b: (0, k, 0)),
                  pl.BlockSpec((2, kt, c), lambda k, b: (0, k, 2 * order)),
                  pl.BlockSpec((2, kt, c), lambda k, b: (0, k, 2 * order + 1))],
        out_specs=pl.BlockSpec((1, 2, kt, c), lambda k, b: (b, 0, k, 0)),
        compiler_params=_cparams(("parallel", "parallel")),
        name="hyena_dft_fwd",
    )(z_src, f3, fh3, fh3)


def _hy_inv_body(y_ref, fi_ref, gate_ref, z_ref, skip_ref, o_ref):
    conv = jnp.dot(fi_ref[...], y_ref[0], preferred_element_type=F32)
    o_ref[0] = gate_ref[0] * (conv + skip_ref[...] * z_ref[0])


def hyena_inv(yf, finv, gate_src, gate_col, z_src, z_col, skip_row):
    bx, l2, c = yf.shape
    lx = l2 // 2
    tl = _tile(lx, 512)
    return pl.pallas_call(
        _hy_inv_body,
        out_shape=jax.ShapeDtypeStruct((bx, lx, c), F32),
        grid=(lx // tl, bx),
        in_specs=[pl.BlockSpec((1, l2, c), lambda t, b: (b, 0, 0)),
                  pl.BlockSpec((tl, l2), lambda t, b: (t, 0)),
                  pl.BlockSpec((1, tl, c), lambda t, b: (b, t, gate_col)),
                  pl.BlockSpec((1, tl, c), lambda t, b: (b, t, z_col)),
                  pl.BlockSpec((1, c), lambda t, b: (0, 0))],
        out_specs=pl.BlockSpec((1, tl, c), lambda t, b: (b, t, 0)),
        compiler_params=_cparams(("parallel", "parallel")),
        name="hyena_dft_inv",
    )(yf, finv, gate_src, z_src, skip_row)


def _dft_mats(l):
    n = 2 * l
    k = jnp.arange(l, dtype=jnp.int32)[:, None]
    s = jnp.arange(l, dtype=jnp.int32)[None, :]
    ang = ((k * s) % n).astype(F32) * (2.0 * math.pi / n)
    cosm = jnp.cos(ang)
    sinm = -jnp.sin(ang)
    nyq = jnp.where(s % 2 == 0, 1.0, -1.0).astype(F32)
    imag = jnp.where(k == 0, nyq, sinm)
    fwd = jnp.concatenate([cosm, imag], axis=0)
    scale = jnp.where(jnp.arange(l) == 0, 1.0 / n, 2.0 / n).astype(F32)
    inv = jnp.concatenate([cosm.T * scale[None, :], imag.T * scale[None, :]], axis=1)
    return fwd.astype(BF16), inv.astype(BF16)


def _filter_features(l):
    bands = (HY_EMB - 1) // 2
    t = jnp.linspace(0.0, 1.0, l, dtype=F32)[:, None]
    w = 2.0 * math.pi * jnp.arange(l, dtype=F32)[:, None] / l
    f = jnp.linspace(1e-4, bands - 1, bands, dtype=F32)[None, :]
    z = jnp.concatenate([t, jnp.cos(f * w), -jnp.sin(f * w)], axis=-1)
    z = jnp.pad(z, ((0, 0), (0, HY_EMB_PAD - HY_EMB)))
    max_decay = math.log(HY_TARGET) / HY_DECAY_SHORT_PCT
    min_decay = math.log(HY_TARGET) / HY_DECAY_LONG_PCT
    deltas = jnp.abs(jnp.linspace(min_decay, max_decay, HY_C, dtype=F32))
    return z, jnp.exp(-t * deltas)


def hyena_mixer(p, conv_w, conv_b, fw1, fb1, fw2, fb2, fw3, ffreq, d_skip):
    bx, lx, _ = p.shape
    c = HY_C
    zfeat, decay = _filter_features(lx)
    w1p = jnp.pad(fw1, ((0, HY_EMB_PAD - HY_EMB), (0, 0)))
    filt = hyena_filters(zfeat, w1p, fb1[None], fw2, fb2[None], fw3, ffreq[None], decay)
    fwd, inv = _dft_mats(lx)
    fh3 = matmul_bf16(fwd, filt).reshape(2, lx, filt.shape[1])
    f3 = fwd.reshape(2, lx, lx)
    uc = short_conv(p, conv_w, conv_b)
    y0 = hyena_fwd(uc, 0, f3, fh3, 0, c).reshape(bx, 2 * lx, c)
    z1 = hyena_inv(y0, inv, uc, 1, uc, 0, d_skip[0:1])
    y1 = hyena_fwd(z1, 0, f3, fh3, 1, c).reshape(bx, 2 * lx, c)
    return hyena_inv(y1, inv, uc, 2, z1, 0, d_skip[1:2])


def _rope_tables(l):
    rows = l // GRID_W
    r = jnp.broadcast_to(jnp.arange(rows)[:, None], (rows, GRID_W)).reshape(-1).astype(F32)
    col = jnp.broadcast_to(jnp.arange(GRID_W)[None, :], (rows, GRID_W)).reshape(-1).astype(F32)
    nf = HEAD_DIM // 4
    inv = ROPE_BASE ** (-jnp.arange(nf, dtype=F32) / nf)
    ar = r[:, None] * inv
    ac = col[:, None] * inv
    ang = jnp.concatenate([ar, ar, ac, ac], axis=-1)
    sign = jnp.concatenate([-jnp.ones((nf,)), jnp.ones((nf,))] * 2).astype(F32)
    cos = jnp.tile(jnp.cos(ang), (1, N_Q_HEADS))
    sin = jnp.tile(jnp.sin(ang) * sign[None, :], (1, N_Q_HEADS))
    return cos, sin


def _rope_body(q_ref, k_ref, cos_ref, sin_ref, qo_ref, ko_ref):
    def rot(x, cos, sin):
        n = x.shape[1]
        lane = lax.broadcasted_iota(jnp.int32, x.shape, 1)
        first = (lane % 32) < 16
        partner = jnp.where(first, pltpu.roll(x, n - 16, 1), pltpu.roll(x, 16, 1))
        return x * cos + partner * sin

    nk = k_ref.shape[2]
    qo_ref[0] = (rot(q_ref[0], cos_ref[...], sin_ref[...]) * (HEAD_DIM ** -0.5)).astype(BF16)
    ko_ref[0] = rot(k_ref[0], cos_ref[:, :nk], sin_ref[:, :nk]).astype(BF16)


def rope_qk(p, cos, sin, q_col, k_col):
    bx, lx, _ = p.shape
    nq = N_Q_HEADS * HEAD_DIM
    nk = N_KV_HEADS * HEAD_DIM
    tr = _tile(lx, ROW_TILE)
    return pl.pallas_call(
        _rope_body,
        out_shape=(jax.ShapeDtypeStruct((bx, lx, nq), BF16), jax.ShapeDtypeStruct((bx, lx, nk), BF16)),
        grid=(bx, lx // tr),
        in_specs=[pl.BlockSpec((1, tr, nq), lambda b, i: (b, i, q_col)),
                  pl.BlockSpec((1, tr, nk), lambda b, i: (b, i, k_col)),
                  pl.BlockSpec((tr, nq), lambda b, i: (i, 0)),
                  pl.BlockSpec((tr, nq), lambda b, i: (i, 0))],
        out_specs=(pl.BlockSpec((1, tr, nq), lambda b, i: (b, i, 0)),
                   pl.BlockSpec((1, tr, nk), lambda b, i: (b, i, 0))),
        compiler_params=_cparams(("parallel", "parallel")),
        name="rope_qk",
    )(p, p, cos, sin)


def _softmax_av(q4, sink_col, parts):
    ss = []
    m = sink_col
    for k, _, mask in parts:
        s = lax.dot_general(q4, k, (((1,), (1,)), ((), ())), preferred_element_type=F32)
        if mask is not None:
            s = jnp.where(mask, s, NEG_BIG)
        ss.append(s)
        m = jnp.maximum(m, jnp.max(s, axis=-1, keepdims=True))
    den = jnp.exp(sink_col - m)
    acc = None
    for s, (_, v, _) in zip(ss, parts):
        e = jnp.exp(s - m)
        den = den + jnp.sum(e, axis=-1, keepdims=True)
        o = jnp.dot(e.astype(BF16), v, preferred_element_type=F32)
        acc = o if acc is None else acc + o
    return acc / den


def _attn_heads(q, sink_ref, parts_for_head, o_ref):
    t = q.shape[0]
    outs = []
    for hk in range(N_KV_HEADS):
        q4 = jnp.concatenate(
            [q[:, (hk * GQA_GROUP + g) * HEAD_DIM:(hk * GQA_GROUP + g + 1) * HEAD_DIM] for g in range(GQA_GROUP)], axis=0)
        sink_col = jnp.concatenate(
            [jnp.full((t, 1), sink_ref[hk * GQA_GROUP + g], F32) for g in range(GQA_GROUP)], axis=0)
        o4 = _softmax_av(q4, sink_col, parts_for_head(hk))
        outs.extend([o4[g * t:(g + 1) * t] for g in range(GQA_GROUP)])
    o_ref[0] = jnp.concatenate(outs, axis=1)


def _win_attn_body(sink_ref, q_ref, k0_ref, k1_ref, k2_ref, v0_ref, v1_ref, v2_ref, kc_ref, vc_ref, o_ref, *, seq_len):
    n = pl.program_id(1)
    blk = q_ref.shape[1]
    kb = jnp.concatenate([k0_ref[0], k1_ref[0], k2_ref[0]], axis=0)
    vb = jnp.concatenate([v0_ref[0], v1_ref[0], v2_ref[0]], axis=0).astype(BF16)
    kc = kc_ref[0].astype(BF16)
    vc = vc_ref[0].astype(BF16)
    qi = lax.broadcasted_iota(jnp.int32, (blk, 3 * blk), 0)
    kj = lax.broadcasted_iota(jnp.int32, (blk, 3 * blk), 1)
    kpos = (n - 1) * blk + kj
    diff = qi + blk - kj
    valid = (kpos >= 0) & (kpos < seq_len) & (diff <= WINDOW) & (diff >= -WINDOW)
    valid4 = jnp.concatenate([valid] * GQA_GROUP, axis=0)

    def parts(hk):
        sl = slice(hk * HEAD_DIM, (hk + 1) * HEAD_DIM)
        return [(kb[:, sl], vb[:, sl], valid4), (kc[:, sl], vc[:, sl], None)]

    _attn_heads(q_ref[0], sink_ref, parts, o_ref)


def window_attention(qr, kr, p, p_ctx, sink, k_col, v_col):
    bx, lx, nq = qr.shape
    nk = kr.shape[2]
    lc = p_ctx.shape[1]
    blk = ATTN_BLK
    nb = lx // blk
    lo = lambda b, n: (b, jnp.maximum(n - 1, 0), 0)
    mid = lambda b, n: (b, n, 0)
    hi = lambda b, n: (b, jnp.minimum(n + 1, nb - 1), 0)
    vlo = lambda b, n: (b, jnp.maximum(n - 1, 0), v_col)
    vmid = lambda b, n: (b, n, v_col)
    vhi = lambda b, n: (b, jnp.minimum(n + 1, nb - 1), v_col)
    return pl.pallas_call(
        functools.partial(_win_attn_body, seq_len=lx),
        out_shape=jax.ShapeDtypeStruct((bx, lx, nq), F32),
        grid=(bx, nb),
        in_specs=[pl.BlockSpec(memory_space=pltpu.SMEM),
                  pl.BlockSpec((1, blk, nq), mid),
                  pl.BlockSpec((1, blk, nk), lo), pl.BlockSpec((1, blk, nk), mid), pl.BlockSpec((1, blk, nk), hi),
                  pl.BlockSpec((1, blk, nk), vlo), pl.BlockSpec((1, blk, nk), vmid), pl.BlockSpec((1, blk, nk), vhi),
                  pl.BlockSpec((1, lc, nk), lambda b, n: (b, 0, k_col)),
                  pl.BlockSpec((1, lc, nk), lambda b, n: (b, 0, v_col))],
        out_specs=pl.BlockSpec((1, blk, nq), mid),
        compiler_params=_cparams(("parallel", "parallel")),
        name="window_attention",
    )(sink, qr, kr, kr, kr, p, p, p, p_ctx, p_ctx)


def _ctx_attn_body(sink_ref, q_ref, kc_ref, vc_ref, o_ref):
    kc = kc_ref[0].astype(BF16)
    vc = vc_ref[0].astype(BF16)
    q = (q_ref[0] * (HEAD_DIM ** -0.5)).astype(BF16)

    def parts(hk):
        sl = slice(hk * HEAD_DIM, (hk + 1) * HEAD_DIM)
        return [(kc[:, sl], vc[:, sl], None)]

    _attn_heads(q, sink_ref, parts, o_ref)


def context_attention(p_ctx, sink, q_col, k_col, v_col):
    bx, lc, _ = p_ctx.shape
    nq = N_Q_HEADS * HEAD_DIM
    nk = N_KV_HEADS * HEAD_DIM
    return pl.pallas_call(
        _ctx_attn_body,
        out_shape=jax.ShapeDtypeStruct((bx, lc, nq), F32),
        grid=(bx,),
        in_specs=[pl.BlockSpec(memory_space=pltpu.SMEM),
                  pl.BlockSpec((1, lc, nq), lambda b: (b, 0, q_col)),
                  pl.BlockSpec((1, lc, nk), lambda b: (b, 0, k_col)),
                  pl.BlockSpec((1, lc, nk), lambda b: (b, 0, v_col))],
        out_specs=pl.BlockSpec((1, lc, nq), lambda b: (b, 0, 0)),
        compiler_params=_cparams(("parallel",)),
        name="context_attention",
    )(sink, p_ctx, p_ctx, p_ctx)


def _out_proj_body(ya_ref, yb_ref, wa_ref, wb_ref, h_ref, g_ref, o_ref):
    y = jnp.dot(ya_ref[0].astype(BF16), wa_ref[...], preferred_element_type=F32)
    y = y + jnp.dot(yb_ref[0].astype(BF16), wb_ref[...], preferred_element_type=F32)
    o_ref[0] = h_ref[0] + g_ref[0] * y


def out_proj_residual(ya, yb, w_bf, h, g1):
    bx, lx, d = h.shape
    ca = ya.shape[2]
    cb = yb.shape[2]
    assert ca == cb and w_bf.shape[0] == ca + cb
    tm = _tile(lx, ROW_TILE)
    return pl.pallas_call(
        _out_proj_body,
        out_shape=jax.ShapeDtypeStruct((bx, lx, d), F32),
        grid=(bx, lx // tm),
        in_specs=[pl.BlockSpec((1, tm, ca), lambda b, i: (b, i, 0)),
                  pl.BlockSpec((1, tm, cb), lambda b, i: (b, i, 0)),
                  pl.BlockSpec((ca, d), lambda b, i: (0, 0)),
                  pl.BlockSpec((cb, d), lambda b, i: (1, 0)),
                  pl.BlockSpec((1, tm, d), lambda b, i: (b, i, 0)),
                  pl.BlockSpec((1, 1, d), lambda b, i: (b, 0, 0))],
        out_specs=pl.BlockSpec((1, tm, d), lambda b, i: (b, i, 0)),
        compiler_params=_cparams(("parallel", "parallel")),
        name="out_proj_residual",
    )(ya, yb, w_bf, w_bf, h, g1)


def _gla_body(q_ref, f_ref, v_ref, lb_ref, s0_ref, o_ref, sout_ref, st_scr, qs_scr, kk_scr, lf_scr, *, reverse):
    s_idx = pl.program_id(2)
    ts = q_ref.shape[1]
    c = GLA_CHUNK
    nch = ts // c

    @pl.when(s_idx == 0)
    def _():
        st_scr[...] = s0_ref[0, 0]

    q = q_ref[0]
    lb = lb_ref[0]
    fg = lb + (1.0 - lb) * _sigmoid(f_ref[0])
    qs_scr[...] = q * _sigmoid(q)
    kk_scr[...] = 1.0 - fg
    lf_scr[...] = jnp.log(fg)

    ti = lax.broadcasted_iota(jnp.int32, (c, c), 0)
    si = lax.broadcasted_iota(jnp.int32, (c, c), 1)
    cum = jnp.where((si >= ti) if reverse else (si <= ti), 1.0, 0.0).astype(F32)
    trow = lax.broadcasted_iota(jnp.int32, (c, 1), 0)

    def chunk(i, carry):
        ci = (nch - 1 - i) if reverse else i
        r0 = pl.multiple_of(ci * c, c)
        qc = qs_scr[pl.ds(r0, c), :]
        kc = kk_scr[pl.ds(r0, c), :]
        lf = lf_scr[pl.ds(r0, c), :]
        vc = v_ref[0, pl.ds(r0, c), :]
        b = jnp.dot(cum, lf, precision=lax.Precision.HIGHEST, preferred_element_type=F32)
        st = st_scr[...]
        qd = qc * jnp.exp(b)
        o = lax.dot_general(qd.astype(BF16), st.astype(BF16), (((1,), (1,)), ((), ())), preferred_element_type=F32)
        for s in range(c):
            d = b - b[s:s + 1, :]
            keep = (trow <= s) if reverse else (trow >= s)
            e = jnp.exp(jnp.where(keep, d, NEG_BIG))
            col = jnp.sum(qc * e * kc[s:s + 1, :], axis=-1, keepdims=True)
            o = o + col * vc[s:s + 1, :]
        o_ref[0, pl.ds(r0, c), :] = o
        b_end = b[0:1, :] if reverse else b[c - 1:c, :]
        kd = kc * jnp.exp(b_end - b)
        upd = lax.dot_general(vc.astype(BF16), kd.astype(BF16), (((0,), (0,)), ((), ())), preferred_element_type=F32)
        st_scr[...] = st * jnp.exp(b_end) + upd
        return carry

    lax.fori_loop(0, nch, chunk, 0)

    @pl.when(s_idx == pl.num_programs(2) - 1)
    def _():
        sout_ref[0, 0] = st_scr[...]


def gla_scan(p, lb3, s0, direction, q_col, f_col, v_col):
    bx, lx, _ = p.shape
    hh = HG_HEADS
    dk = HG_DK
    ts = _tile(lx, GLA_SEQ_BLOCK)
    ns = lx // ts
    rev = direction == 1
    sblk = (lambda s: ns - 1 - s) if rev else (lambda s: s)
    return pl.pallas_call(
        functools.partial(_gla_body, reverse=rev),
        out_shape=(jax.ShapeDtypeStruct((bx, lx, hh * dk), F32), jax.ShapeDtypeStruct((bx, hh, dk, dk), F32)),
        grid=(bx, hh, ns),
        in_specs=[pl.BlockSpec((1, ts, dk), lambda b, h, s: (b, sblk(s), q_col * hh + h)),
                  pl.BlockSpec((1, ts, dk), lambda b, h, s: (b, sblk(s), f_col * hh + h)),
                  pl.BlockSpec((1, ts, dk), lambda b, h, s: (b, sblk(s), v_col * hh + h)),
                  pl.BlockSpec((1, 1, dk), lambda b, h, s: (direction, 0, h)),
                  pl.BlockSpec((1, 1, dk, dk), lambda b, h, s: (b, h, 0, 0))],
        out_specs=(pl.BlockSpec((1, ts, dk), lambda b, h, s: (b, sblk(s), h)),
                   pl.BlockSpec((1, 1, dk, dk), lambda b, h, s: (b, h, 0, 0))),
        scratch_shapes=[pltpu.VMEM((dk, dk), F32), pltpu.VMEM((ts, dk), F32),
                        pltpu.VMEM((ts, dk), F32), pltpu.VMEM((ts, dk), F32)],
        compiler_params=_cparams(("parallel", "parallel", "arbitrary")),
        name="gla_scan_bwd" if rev else "gla_scan_fwd",
    )(p, p, p, lb3, s0)


def _hg_readout_body(of_ref, ob_ref, g_ref, gn_ref, w_ref, h_ref, g1_ref, o_ref):
    o = of_ref[0] + ob_ref[0]
    gn = gn_ref[...]
    pieces = []
    for hh in range(HG_HEADS):
        oh = o[:, hh * HG_DK:(hh + 1) * HG_DK]
        y = oh * lax.rsqrt(jnp.mean(oh * oh, axis=-1, keepdims=True) + EPS)
        pieces.append(y * gn)
    on = jnp.concatenate(pieces, axis=1)
    g = g_ref[0]
    on = on * (g * _sigmoid(g))
    y = jnp.dot(on.astype(BF16), w_ref[...], preferred_element_type=F32)
    o_ref[0] = h_ref[0] + g1_ref[0] * y


def hgrn_readout_residual(o_f, o_b, p, g_col, onorm_row, w_bf, h, g1):
    bx, lx, d = h.shape
    f = o_f.shape[2]
    tm = _tile(lx, ROW_TILE)
    return pl.pallas_call(
        _hg_readout_body,
        out_shape=jax.ShapeDtypeStruct((bx, lx, d), F32),
        grid=(bx, lx // tm),
        in_specs=[pl.BlockSpec((1, tm, f), lambda b, i: (b, i, 0)),
                  pl.BlockSpec((1, tm, f), lambda b, i: (b, i, 0)),
                  pl.BlockSpec((1, tm, f), lambda b, i: (b, i, g_col)),
                  pl.BlockSpec((1, HG_DK), lambda b, i: (0, 0)),
                  pl.BlockSpec((f, d), lambda b, i: (0, 0)),
                  pl.BlockSpec((1, tm, d), lambda b, i: (b, i, 0)),
                  pl.BlockSpec((1, 1, d), lambda b, i: (b, 0, 0))],
        out_specs=pl.BlockSpec((1, tm, d), lambda b, i: (b, i, 0)),
        compiler_params=_cparams(("parallel", "parallel")),
        name="hgrn_readout_residual",
    )(o_f, o_b, p, onorm_row, w_bf, h, g1)


def _pair_rows():
    rows = [(0, b) for b in range(16)] + [(1, b) for b in range(8)]
    for a in range(2, 8):
        rows += [(a, b) for b in range(8)]
    rows += [(a, 0) for a in range(8, 16)]
    return rows


def _top16_rows(s):
    n = s.shape[0]
    ri = lax.broadcasted_iota(jnp.int32, s.shape, 0)
    rank = jnp.full(s.shape, float(PEER_TOPK), F32)
    vals = []
    for a in range(PEER_TOPK):
        m = jnp.max(s, axis=0, keepdims=True)
        first = jnp.min(jnp.where(s == m, ri, n), axis=0, keepdims=True)
        sel = ri == first
        rank = jnp.where(sel, float(a), rank)
        s = jnp.where(sel, -jnp.inf, s)
        vals.append(m)
    return rank, jnp.concatenate(vals, axis=0)


def _peer_prep_body(x_ref, wq_ref, keys_ref, flat_ref, okf_ref, rank1_ref, bn_ref, crow_ref, av_ref, q_scr):
    x = x_ref[...]
    q_scr[...] = lax.dot_general(wq_ref[...], x, (((1,), (1,)), ((), ())), preferred_element_type=F32).astype(BF16)
    nk = N_KEYS
    flat = flat_ref[...]
    okf = okf_ref[...]

    def head(h, carry):
        r0 = pl.multiple_of(h * 2 * nk, 2 * nk)
        s0 = jnp.dot(keys_ref[h, 0], q_scr[pl.ds(r0, nk), :], preferred_element_type=F32)
        s1 = jnp.dot(keys_ref[h, 1], q_scr[pl.ds(r0 + nk, nk), :], preferred_element_type=F32)
        rank0, v0 = _top16_rows(s0)
        rank1, v1 = _top16_rows(s1)
        blocks = [v0[0:1] + v1, v0[1:2] + v1[0:8]]
        blocks += [v0[a:a + 1] + v1[0:8] for a in range(2, 8)]
        blocks += [v0[8:16] + v1[0:1]]
        cand = jnp.concatenate(blocks, axis=0)
        cand = jnp.where(okf > 0.0, cand, -jnp.inf)
        top = cand[0:1]
        work = cand
        selm = jnp.zeros(cand.shape, F32)
        for _ in range(PEER_TOPK):
            m = jnp.max(work, axis=0, keepdims=True)
            first = jnp.min(jnp.where(work == m, flat, 1 << 20), axis=0, keepdims=True)
            sel = flat == first
            selm = jnp.where(sel, 1.0, selm)
            work = jnp.where(sel, -jnp.inf, work)
        z = jnp.sum(selm * jnp.exp(jnp.where(selm > 0.0, cand - top, 0.0)), axis=0, keepdims=True)
        cnt = [jnp.sum(selm[0:16], axis=0, keepdims=True)]
        cnt += [jnp.sum(selm[16 + 8 * (a - 1):24 + 8 * (a - 1)], axis=0, keepdims=True) for a in range(1, 8)]
        cnt8 = selm[72:80]
        crow = jnp.zeros(rank0.shape, F32)
        for a in range(8):
            crow = jnp.where(rank0 == float(a), cnt[a], crow)
        for a in range(8, 16):
            crow = jnp.where(rank0 == float(a), cnt8[a - 8:a - 7], crow)
        av = jnp.where(rank0 < float(PEER_TOPK), jnp.exp(s0 - v0[0:1]), 0.0)
        bn = jnp.where(rank1 < float(PEER_TOPK), jnp.exp(s1 - v1[0:1]), 0.0) / z
        rank1_ref[h] = rank1
        bn_ref[h] = bn
        crow_ref[h] = crow
        av_ref[h] = av
        return carry

    lax.fori_loop(0, PEER_HEADS, head, 0)


def peer_prep(x_bf, wq_t, keys_bf):
    t, d = x_bf.shape
    tm = _tile(t, PEER_PREP_TOK)
    rows = _pair_rows()
    flat = jnp.asarray([[a * 16 + b] for a, b in rows], jnp.int32)
    okf = jnp.asarray([[1.0 if (a + 1) * (b + 1) <= PEER_TOPK else 0.0] for a, b in rows], F32)
    out = jax.ShapeDtypeStruct((PEER_HEADS, N_KEYS, t), F32)
    ospec = pl.BlockSpec((PEER_HEADS, N_KEYS, tm), lambda i: (0, 0, i))
    return pl.pallas_call(
        _peer_prep_body,
        out_shape=(out, out, out, out),
        grid=(t // tm,),
        in_specs=[pl.BlockSpec((tm, d), lambda i: (i, 0)),
                  pl.BlockSpec(wq_t.shape, lambda i: (0, 0)),
                  pl.BlockSpec(keys_bf.shape, lambda i: (0, 0, 0, 0)),
                  pl.BlockSpec(flat.shape, lambda i: (0, 0)),
                  pl.BlockSpec(okf.shape, lambda i: (0, 0))],
        out_specs=(ospec, ospec, ospec, ospec),
        scratch_shapes=[pltpu.VMEM((wq_t.shape[0], tm), BF16)],
        compiler_params=_cparams(("parallel",)),
        name="peer_prep",
    )(x_bf, wq_t, keys_bf, flat, okf)


def _peer_main_body(x_ref, u_ref, v_ref, rank1_ref, bn_ref, crow_ref, av_ref, h_ref, g2_ref, fg_ref, o_ref,
                    act_scr, pt_scr, acc_scr, *, final_norm):
    j = pl.program_id(1)
    tm = x_ref.shape[0]
    nrow = u_ref.shape[0] // N_KEYS

    @pl.when(j == 0)
    def _():
        acc_scr[...] = jnp.zeros(acc_scr.shape, F32)

    act_scr[...] = lax.dot_general(u_ref[...], x_ref[...], (((1,), (1,)), ((), ())), preferred_element_type=F32)

    def row(r, carry):
        e1 = j * nrow + r
        r0 = pl.multiple_of(r * N_KEYS, N_KEYS)
        for c0 in range(0, tm, PEER_CW):
            cs = slice(c0, c0 + PEER_CW)
            w = jnp.zeros((N_KEYS, PEER_CW), F32)
            for hd in range(PEER_HEADS):
                cr = crow_ref[hd, pl.ds(e1, 1), cs]
                ar = av_ref[hd, pl.ds(e1, 1), cs]
                w = w + jnp.where(rank1_ref[hd, :, cs] < cr, bn_ref[hd, :, cs], 0.0) * ar
            a = act_scr[pl.ds(r0, N_KEYS), cs]
            gelu = 0.5 * a * (1.0 + lax.erf(a * (2.0 ** -0.5)))
            pt_scr[pl.ds(r0, N_KEYS), cs] = (gelu * w).astype(BF16)
        return carry

    lax.fori_loop(0, nrow, row, 0)
    acc_scr[...] += lax.dot_general(pt_scr[...], v_ref[...], (((0,), (0,)), ((), ())), preferred_element_type=F32)

    @pl.when(j == pl.num_programs(1) - 1)
    def _():
        hn = h_ref[...] + g2_ref[0] * acc_scr[...]
        if final_norm:
            hn = hn * lax.rsqrt(jnp.mean(hn * hn, axis=-1, keepdims=True) + EPS) * fg_ref[...]
        o_ref[...] = hn


def peer_main(x_bf, u_bf, v_bf, prep, h2, g2, tokens_per_batch, final_g, final_norm):
    t, d = x_bf.shape
    e = u_bf.shape[0]
    tm = _tile(tokens_per_batch, PEER_TOK)
    te = PEER_ROWS * N_KEYS
    assert e % te == 0 and tm % PEER_CW == 0
    tpb = tokens_per_batch // tm
    pspec = pl.BlockSpec((PEER_HEADS, N_KEYS, tm), lambda i, j: (0, 0, i))
    return pl.pallas_call(
        functools.partial(_peer_main_body, final_norm=final_norm),
        out_shape=jax.ShapeDtypeStruct((t, d), F32),
        grid=(t // tm, e // te),
        in_specs=[pl.BlockSpec((tm, d), lambda i, j: (i, 0)),
                  pl.BlockSpec((te, d), lambda i, j: (j, 0)),
                  pl.BlockSpec((te, d), lambda i, j: (j, 0)),
                  pspec, pspec, pspec, pspec,
                  pl.BlockSpec((tm, d), lambda i, j: (i, 0)),
                  pl.BlockSpec((1, 1, d), lambda i, j: (i // tpb, 0, 0)),
                  pl.BlockSpec((1, d), lambda i, j: (0, 0))],
        out_specs=pl.BlockSpec((tm, d), lambda i, j: (i, 0)),
        scratch_shapes=[pltpu.VMEM((te, tm), F32), pltpu.VMEM((te, tm), BF16), pltpu.VMEM((tm, d), F32)],
        compiler_params=_cparams(("parallel", "arbitrary")),
        name="peer_dense",
    )(x_bf, u_bf, v_bf, *prep, h2, g2, final_g)


def peer_residual(h, norm_g, sh, sc, g2, wq_t, keys_bf, u_bf, v_bf, final_g, final_norm):
    bx, lx, d = h.shape
    a = norm_mod(h, norm_g, sh, sc).reshape(bx * lx, d)
    prep = peer_prep(a, wq_t, keys_bf)
    out = peer_main(a, u_bf, v_bf, prep, h.reshape(bx * lx, d), g2, lx, final_g, final_norm)
    return out.reshape(bx, lx, d)


def kernel(x, c, ctx, c_ctx, ada_w, ada_b, norm1_g, norm2_g, final_g, ab_w_in, ab_w_out, hy_conv_w, hy_conv_b,
           hy_filt_w1, hy_filt_b1, hy_filt_w2, hy_filt_b2, hy_filt_w3, hy_filt_freq, hy_skip, attn_sink,
           hg_w_in, hg_w_out, hg_lb_logits, hg_onorm_g, peer_wq, peer_keys, peer_u, peer_v):
    bsz, seq, d = x.shape
    depth = ada_w.shape[0]
    assert depth == 2
    lb_p = jax.nn.softmax(hg_lb_logits.astype(F32), axis=0)
    lb_all = jnp.cumsum(lb_p, axis=0) - lb_p[0:1]

    c16 = jnp.concatenate([c, c_ctx[None], jnp.zeros((16 - bsz - 1, d), F32)], axis=0)
    final_row = final_g[None]
    h_lat, h_ctx = x, ctx

    for l in range(depth):
        need_ctx = l < depth - 1
        mod = ada_mod(c16, ada_w[l], ada_b[l][None])
        lat = [mod[:bsz, i * d:(i + 1) * d][:, None, :] for i in range(6)]
        cx = [jnp.broadcast_to(mod[bsz:bsz + 1, i * d:(i + 1) * d][:, None, :], (bsz, 1, d)) for i in range(6)]
        n1 = norm1_g[l][None]
        n2 = norm2_g[l][None]
        wq_t = peer_wq[l].T.astype(BF16)
        keys_bf = peer_keys[l].astype(BF16)
        u_bf = peer_u[l].astype(BF16)
        v_bf = peer_v[l].astype(BF16)
        j = l // 2
        if l % 2 == 0:
            w_in = ab_w_in[j].astype(BF16)
            w_out = ab_w_out[j].astype(BF16)
            p_lat = norm_mod_matmul(h_lat, n1, lat[0], lat[1], w_in)
            p_ctx = norm_mod_matmul(h_ctx, n1, cx[0], cx[1], w_in)
            hy = (hy_conv_w[j], hy_conv_b[j][None], hy_filt_w1[j], hy_filt_b1[j], hy_filt_w2[j], hy_filt_b2[j],
                  hy_filt_w3[j], hy_filt_freq[j], hy_skip[j])
            cos, sin = _rope_tables(seq)
            qr, kr = rope_qk(p_lat, cos, sin, 3, 16)
            sink = attn_sink[j]
            at_lat = window_attention(qr, kr, p_lat, p_ctx, sink, 16, 17)
            hy_lat = hyena_mixer(p_lat, *hy)
            h_lat = out_proj_residual(hy_lat, at_lat, w_out, h_lat, lat[2])
            if need_ctx:
                at_ctx = context_attention(p_ctx, sink, 3, 16, 17)
                hy_ctx = hyena_mixer(p_ctx, *hy)
                h_ctx = out_proj_residual(hy_ctx, at_ctx, w_out, h_ctx, cx[2])
        else:
            w_in = hg_w_in[j].astype(BF16)
            w_out = hg_w_out[j].astype(BF16)
            lb3 = lb_all[l].astype(F32)[:, None, :]
            p_lat = norm_mod_matmul(h_lat, n1, lat[0], lat[1], w_in)
            p_ctx = norm_mod_matmul(h_ctx, n1, cx[0], cx[1], w_in)
            s0 = jnp.zeros((bsz, HG_HEADS, HG_DK, HG_DK), F32)
            o_cf, s_cf = gla_scan(p_ctx, lb3, s0, 0, 0, 1, 3)
            o_cb, s_cb = gla_scan(p_ctx, lb3, s0, 1, 0, 2, 3)
            o_lf, _ = gla_scan(p_lat, lb3, s_cf, 0, 0, 1, 3)
            o_lb, _ = gla_scan(p_lat, lb3, s_cb, 1, 0, 2, 3)
            onorm = hg_onorm_g[j].astype(F32)[None]
            h_lat = hgrn_readout_residual(o_lf, o_lb, p_lat, 4, onorm, w_out, h_lat, lat[2])
            if need_ctx:
                h_ctx = hgrn_readout_residual(o_cf, o_cb, p_ctx, 4, onorm, w_out, h_ctx, cx[2])
        last = l == depth - 1
        h_lat = peer_residual(h_lat, n2, lat[3], lat[4], lat[5], wq_t, keys_bf, u_bf, v_bf, final_row, last)
        if need_ctx:
            h_ctx = peer_residual(h_ctx, n2, cx[3], cx[4], cx[5], wq_t, keys_bf, u_bf, v_bf, final_row, False)
    return h_lat
```

```python
import functools
import math

import jax
import jax.numpy as jnp
import numpy as np
from jax import lax
from jax.experimental import pallas as pl
from jax.experimental.pallas import tpu as pltpu

F32 = jnp.float32
BF16 = jnp.bfloat16
EPS = 1e-6

HY_C = 512
HY_EMB = 33
HY_EMB_PAD = 40
HY_FILT_W = 64
HY_DECAY_SHORT_PCT = 0.3
HY_DECAY_LONG_PCT = 1.5
HY_TARGET = 1e-2
HEAD_DIM = 64
N_Q_HEADS = 8
N_KV_HEADS = 2
GQA_GROUP = 4
WINDOW = 128
ATTN_BLK = 128
GRID_W = 64
ROPE_BASE = 10000.0
HG_HEADS = 8
HG_DK = 128
PEER_HEADS = 8
PEER_TOPK = 16
N_KEYS = 128

LANES = 128
BF16_SUBLANES = 16
VMEM_LIMIT_BYTES = 56 * 1024 * 1024
ROW_TILE = 512
GLA_CHUNK = 64
GLA_SEQ_BLOCK = 256
PEER_PREP_TOK = 256
PEER_TOK = 512
PEER_ROWS = 8
PEER_SUB_ROWS = 2
PEER_CW = 256
NEG_BIG = -1e30


def _cparams(sem):
    return pltpu.CompilerParams(dimension_semantics=sem, vmem_limit_bytes=VMEM_LIMIT_BYTES)


def _sigmoid(x):
    return 1.0 / (1.0 + jnp.exp(-x))


def _tile(n, t):
    t = min(n, t)
    assert n % t == 0, (n, t)
    return t


def _ada_body(c_ref, w_ref, b_ref, o_ref):
    c = c_ref[...]
    s = c * _sigmoid(c)
    o_ref[...] = jnp.dot(s.astype(BF16), w_ref[...].astype(BF16), preferred_element_type=F32) + b_ref[...]


def ada_mod(c16, w, b):
    d, n = w.shape
    tn = _tile(n, 1536)
    return pl.pallas_call(
        _ada_body,
        out_shape=jax.ShapeDtypeStruct((c16.shape[0], n), F32),
        grid=(n // tn,),
        in_specs=[pl.BlockSpec(c16.shape, lambda j: (0, 0)),
                  pl.BlockSpec((d, tn), lambda j: (0, j)),
                  pl.BlockSpec((1, tn), lambda j: (0, j))],
        out_specs=pl.BlockSpec((c16.shape[0], tn), lambda j: (0, j)),
        compiler_params=_cparams(("arbitrary",)),
        name="ada_mod",
    )(c16, w, b)


def _norm_mod(x, g, sh, sc):
    y = x * lax.rsqrt(jnp.mean(x * x, axis=-1, keepdims=True) + EPS)
    return (y * g) * (1.0 + sc) + sh


def _nm_matmul_body(h_ref, g_ref, sh_ref, sc_ref, w_ref, p_ref, a_scr):
    @pl.when(pl.program_id(2) == 0)
    def _():
        a_scr[...] = _norm_mod(h_ref[0], g_ref[...], sh_ref[0], sc_ref[0]).astype(BF16)

    p_ref[0] = jnp.dot(a_scr[...], w_ref[...], preferred_element_type=F32)


def norm_mod_matmul(h, g, sh, sc, w_bf):
    bx, lx, d = h.shape
    n = w_bf.shape[1]
    tm = _tile(lx, ROW_TILE)
    tn = _tile(n, 1280 if n % 1280 == 0 else (1152 if n % 1152 == 0 else 1024))
    return pl.pallas_call(
        _nm_matmul_body,
        out_shape=jax.ShapeDtypeStruct((bx, lx, n), F32),
        grid=(bx, lx // tm, n // tn),
        in_specs=[pl.BlockSpec((1, tm, d), lambda b, i, j: (b, i, 0)),
                  pl.BlockSpec((1, d), lambda b, i, j: (0, 0)),
                  pl.BlockSpec((1, 1, d), lambda b, i, j: (b, 0, 0)),
                  pl.BlockSpec((1, 1, d), lambda b, i, j: (b, 0, 0)),
                  pl.BlockSpec((d, tn), lambda b, i, j: (0, j))],
        out_specs=pl.BlockSpec((1, tm, tn), lambda b, i, j: (b, i, j)),
        scratch_shapes=[pltpu.VMEM((tm, d), BF16)],
        compiler_params=_cparams(("parallel", "parallel", "arbitrary")),
        name="norm_mod_matmul",
    )(h, g, sh, sc, w_bf)


def _nm_only_body(h_ref, g_ref, sh_ref, sc_ref, a_ref):
    a_ref[0] = _norm_mod(h_ref[0], g_ref[...], sh_ref[0], sc_ref[0]).astype(BF16)


def norm_mod(h, g, sh, sc):
    bx, lx, d = h.shape
    tm = _tile(lx, ROW_TILE)
    return pl.pallas_call(
        _nm_only_body,
        out_shape=jax.ShapeDtypeStruct((bx, lx, d), BF16),
        grid=(bx, lx // tm),
        in_specs=[pl.BlockSpec((1, tm, d), lambda b, i: (b, i, 0)),
                  pl.BlockSpec((1, d), lambda b, i: (0, 0)),
                  pl.BlockSpec((1, 1, d), lambda b, i: (b, 0, 0)),
                  pl.BlockSpec((1, 1, d), lambda b, i: (b, 0, 0))],
        out_specs=pl.BlockSpec((1, tm, d), lambda b, i: (b, i, 0)),
        compiler_params=_cparams(("parallel", "parallel")),
        name="norm_mod",
    )(h, g, sh, sc)


def _mm_body(a_ref, b_ref, o_ref):
    o_ref[...] = jnp.dot(a_ref[...], b_ref[...], preferred_element_type=F32)


def matmul_bf16(a, b):
    m, k = a.shape
    n = b.shape[1]
    tm = _tile(m, 1024)
    tn = _tile(n, 1024)
    return pl.pallas_call(
        _mm_body,
        out_shape=jax.ShapeDtypeStruct((m, n), F32),
        grid=(m // tm, n // tn),
        in_specs=[pl.BlockSpec((tm, k), lambda i, j: (i, 0)),
                  pl.BlockSpec((k, tn), lambda i, j: (0, j))],
        out_specs=pl.BlockSpec((tm, tn), lambda i, j: (i, j)),
        compiler_params=_cparams(("parallel", "parallel")),
        name="matmul_bf16",
    )(a, b)


def _filter_body(z_ref, w1_ref, b1_ref, w2_ref, b2_ref, w3_ref, fr_ref, dec_ref, o_ref):
    hi = lax.Precision.HIGHEST
    fr = fr_ref[...]
    hdn = jnp.sin(fr * (jnp.dot(z_ref[...], w1_ref[...], precision=hi, preferred_element_type=F32) + b1_ref[...]))
    hdn = jnp.sin(fr * (jnp.dot(hdn, w2_ref[...], precision=hi, preferred_element_type=F32) + b2_ref[...]))
    h = jnp.dot(hdn, w3_ref[...], precision=hi, preferred_element_type=F32)
    dec = dec_ref[...]
    c = dec.shape[1]
    h0 = h[:, :c] * dec
    h1 = h[:, c:] * dec
    nrm = jnp.sum(jnp.abs(h0) + jnp.abs(h1), axis=0, keepdims=True)
    inv = 1.0 / nrm
    ri = lax.broadcasted_iota(jnp.int32, h1.shape, 0)
    o_ref[:, :c] = (h0 * inv).astype(BF16)
    o_ref[:, c:] = jnp.where(ri == 0, 0.0, h1 * inv).astype(BF16)


def hyena_filters(zfeat, w1p, b1, w2, b2, w3, freq, decay):
    l = zfeat.shape[0]
    c = decay.shape[1]
    n_order = w3.shape[1] // (2 * c)
    full = lambda shape: pl.BlockSpec(shape, lambda o: (0,) * len(shape))
    return pl.pallas_call(
        _filter_body,
        out_shape=jax.ShapeDtypeStruct((l, n_order * 2 * c), BF16),
        grid=(n_order,),
        in_specs=[full(zfeat.shape), full(w1p.shape), full(b1.shape), full(w2.shape), full(b2.shape),
                  pl.BlockSpec((w3.shape[0], 2 * c), lambda o: (0, o)),
                  full(freq.shape), full(decay.shape)],
        out_specs=pl.BlockSpec((l, 2 * c), lambda o: (0, o)),
        compiler_params=_cparams(("arbitrary",)),
        name="hyena_filters",
    )(zfeat, w1p, b1, w2, b2, w3, freq, decay)


def _short_conv_body(u_ref, w_ref, b_ref, o_ref):
    u = u_ref[0]
    l = u.shape[0]
    ri = lax.broadcasted_iota(jnp.int32, u.shape, 0)
    prev = jnp.where(ri == 0, 0.0, pltpu.roll(u, 1, 0))
    nxt = jnp.where(ri == l - 1, 0.0, pltpu.roll(u, l - 1, 0))
    w = w_ref[...]
    o_ref[0] = prev * w[0:1] + u * w[1:2] + nxt * w[2:3] + b_ref[...]


def short_conv(p, conv_w, conv_b):
    bx, lx, _ = p.shape
    c3 = conv_w.shape[1]
    tc = _tile(c3, 512)
    return pl.pallas_call(
        _short_conv_body,
        out_shape=jax.ShapeDtypeStruct((bx, lx, c3), F32),
        grid=(bx, c3 // tc),
        in_specs=[pl.BlockSpec((1, lx, tc), lambda b, j: (b, 0, j)),
                  pl.BlockSpec((3, tc), lambda b, j: (0, j)),
                  pl.BlockSpec((1, tc), lambda b, j: (0, j))],
        out_specs=pl.BlockSpec((1, lx, tc), lambda b, j: (b, 0, j)),
        compiler_params=_cparams(("parallel", "parallel")),
        name="hyena_short_conv",
    )(p, conv_w, conv_b)


def _hy_fwd_body(z_ref, f_ref, h0_ref, h1_ref, y_ref):
    z = z_ref[0].astype(BF16)
    kt = f_ref.shape[1]
    zre = jnp.dot(f_ref[0], z, preferred_element_type=F32)
    zim = jnp.dot(f_ref[1], z, preferred_element_type=F32)
    k0 = (pl.program_id(0) * kt + lax.broadcasted_iota(jnp.int32, zre.shape, 0)) == 0
    hre = h0_ref[0] + h1_ref[0]
    him = jnp.where(k0, h0_ref[1] + h1_ref[1], h0_ref[1] - h1_ref[1])
    yre = jnp.where(k0, zre * hre, zre * hre - zim * him)
    yim = jnp.where(k0, zim * him, zre * him + zim * hre)
    y_ref[0, 0] = yre.astype(BF16)
    y_ref[0, 1] = yim.astype(BF16)


def hyena_fwd(z_src, z_col, f3, fh3, order, c):
    bx, lx, _ = z_src.shape
    kt = _tile(lx, 512)
    return pl.pallas_call(
        _hy_fwd_body,
        out_shape=jax.ShapeDtypeStruct((bx, 2, lx, c), BF16),
        grid=(lx // kt, bx),
        in_specs=[pl.BlockSpec((1, lx, c), lambda k, b: (b, 0, z_col)),
                  pl.BlockSpec((2, kt, lx), lambda k, b: (0, k, 0)),
                  pl.BlockSpec((2, kt, c), lambda k, b: (0, k, 2 * order)),
                  pl.BlockSpec((2, kt, c), lambda k, b: (0, k, 2 * order + 1))],
        out_specs=pl.BlockSpec((1, 2, kt, c), lambda k, b: (b, 0, k, 0)),
        compiler_params=_cparams(("parallel", "parallel")),
        name="hyena_dft_fwd",
    )(z_src, f3, fh3, fh3)


def _hy_inv_body(y_ref, fi_ref, gate_ref, z_ref, skip_ref, o_ref):
    conv = jnp.dot(fi_ref[...], y_ref[0], preferred_element_type=F32)
    o_ref[0] = gate_ref[0] * (conv + skip_ref[...] * z_ref[0])


def hyena_inv(yf, finv, gate_src, gate_col, z_src, z_col, skip_row):
    bx, l2, c = yf.shape
    lx = l2 // 2
    tl = _tile(lx, 512)
    return pl.pallas_call(
        _hy_inv_body,
        out_shape=jax.ShapeDtypeStruct((bx, lx, c), F32),
        grid=(lx // tl, bx),
        in_specs=[pl.BlockSpec((1, l2, c), lambda t, b: (b, 0, 0)),
                  pl.BlockSpec((tl, l2), lambda t, b: (t, 0)),
                  pl.BlockSpec((1, tl, c), lambda t, b: (b, t, gate_col)),
                  pl.BlockSpec((1, tl, c), lambda t, b: (b, t, z_col)),
                  pl.BlockSpec((1, c), lambda t, b: (0, 0))],
        out_specs=pl.BlockSpec((1, tl, c), lambda t, b: (b, t, 0)),
        compiler_params=_cparams(("parallel", "parallel")),
        name="hyena_dft_inv",
    )(yf, finv, gate_src, z_src, skip_row)


def _dft_mats(l):
    n = 2 * l
    k = jnp.arange(l, dtype=jnp.int32)[:, None]
    s = jnp.arange(l, dtype=jnp.int32)[None, :]
    ang = ((k * s) % n).astype(F32) * (2.0 * math.pi / n)
    cosm = jnp.cos(ang)
    sinm = -jnp.sin(ang)
    nyq = jnp.where(s % 2 == 0, 1.0, -1.0).astype(F32)
    imag = jnp.where(k == 0, nyq, sinm)
    fwd = jnp.concatenate([cosm, imag], axis=0)
    scale = jnp.where(jnp.arange(l) == 0, 1.0 / n, 2.0 / n).astype(F32)
    inv = jnp.concatenate([cosm.T * scale[None, :], imag.T * scale[None, :]], axis=1)
    return fwd.astype(BF16), inv.astype(BF16)


def _filter_features(l):
    bands = (HY_EMB - 1) // 2
    t = jnp.linspace(0.0, 1.0, l, dtype=F32)[:, None]
    w = 2.0 * math.pi * jnp.arange(l, dtype=F32)[:, None] / l
    f = jnp.linspace(1e-4, bands - 1, bands, dtype=F32)[None, :]
    z = jnp.concatenate([t, jnp.cos(f * w), -jnp.sin(f * w)], axis=-1)
    z = jnp.pad(z, ((0, 0), (0, HY_EMB_PAD - HY_EMB)))
    max_decay = math.log(HY_TARGET) / HY_DECAY_SHORT_PCT
    min_decay = math.log(HY_TARGET) / HY_DECAY_LONG_PCT
    deltas = jnp.abs(jnp.linspace(min_decay, max_decay, HY_C, dtype=F32))
    return z, jnp.exp(-t * deltas)


def hyena_mixer(p, conv_w, conv_b, fw1, fb1, fw2, fb2, fw3, ffreq, d_skip):
    bx, lx, _ = p.shape
    c = HY_C
    zfeat, decay = _filter_features(lx)
    w1p = jnp.pad(fw1, ((0, HY_EMB_PAD - HY_EMB), (0, 0)))
    filt = hyena_filters(zfeat, w1p, fb1[None], fw2, fb2[None], fw3, ffreq[None], decay)
    fwd, inv = _dft_mats(lx)
    fh3 = matmul_bf16(fwd, filt).reshape(2, lx, filt.shape[1])
    f3 = fwd.reshape(2, lx, lx)
    uc = short_conv(p, conv_w, conv_b)
    y0 = hyena_fwd(uc, 0, f3, fh3, 0, c).reshape(bx, 2 * lx, c)
    z1 = hyena_inv(y0, inv, uc, 1, uc, 0, d_skip[0:1])
    y1 = hyena_fwd(z1, 0, f3, fh3, 1, c).reshape(bx, 2 * lx, c)
    return hyena_inv(y1, inv, uc, 2, z1, 0, d_skip[1:2])


def _rope_tables(l):
    rows = l // GRID_W
    r = jnp.broadcast_to(jnp.arange(rows)[:, None], (rows, GRID_W)).reshape(-1).astype(F32)
    col = jnp.broadcast_to(jnp.arange(GRID_W)[None, :], (rows, GRID_W)).reshape(-1).astype(F32)
    nf = HEAD_DIM // 4
    inv = ROPE_BASE ** (-jnp.arange(nf, dtype=F32) / nf)
    ar = r[:, None] * inv
    ac = col[:, None] * inv
    ang = jnp.concatenate([ar, ar, ac, ac], axis=-1)
    sign = jnp.concatenate([-jnp.ones((nf,)), jnp.ones((nf,))] * 2).astype(F32)
    cos = jnp.tile(jnp.cos(ang), (1, N_Q_HEADS))
    sin = jnp.tile(jnp.sin(ang) * sign[None, :], (1, N_Q_HEADS))
    return cos, sin


def _rope_body(q_ref, k_ref, cos_ref, sin_ref, qo_ref, ko_ref):
    def rot(x, cos, sin):
        n = x.shape[1]
        lane = lax.broadcasted_iota(jnp.int32, x.shape, 1)
        first = (lane % 32) < 16
        partner = jnp.where(first, pltpu.roll(x, n - 16, 1), pltpu.roll(x, 16, 1))
        return x * cos + partner * sin

    nk = k_ref.shape[2]
    qo_ref[0] = (rot(q_ref[0], cos_ref[...], sin_ref[...]) * (HEAD_DIM ** -0.5)).astype(BF16)
    ko_ref[0] = rot(k_ref[0], cos_ref[:, :nk], sin_ref[:, :nk]).astype(BF16)


def rope_qk(p, cos, sin, q_col, k_col):
    bx, lx, _ = p.shape
    nq = N_Q_HEADS * HEAD_DIM
    nk = N_KV_HEADS * HEAD_DIM
    tr = _tile(lx, ROW_TILE)
    return pl.pallas_call(
        _rope_body,
        out_shape=(jax.ShapeDtypeStruct((bx, lx, nq), BF16), jax.ShapeDtypeStruct((bx, lx, nk), BF16)),
        grid=(bx, lx // tr),
        in_specs=[pl.BlockSpec((1, tr, nq), lambda b, i: (b, i, q_col)),
                  pl.BlockSpec((1, tr, nk), lambda b, i: (b, i, k_col)),
                  pl.BlockSpec((tr, nq), lambda b, i: (i, 0)),
                  pl.BlockSpec((tr, nq), lambda b, i: (i, 0))],
        out_specs=(pl.BlockSpec((1, tr, nq), lambda b, i: (b, i, 0)),
                   pl.BlockSpec((1, tr, nk), lambda b, i: (b, i, 0))),
        compiler_params=_cparams(("parallel", "parallel")),
        name="rope_qk",
    )(p, p, cos, sin)


def _softmax_av(q4, sink_col, parts):
    ss = []
    m = sink_col
    for k, _, mask in parts:
        s = lax.dot_general(q4, k, (((1,), (1,)), ((), ())), preferred_element_type=F32)
        if mask is not None:
            s = jnp.where(mask, s, NEG_BIG)
        ss.append(s)
        m = jnp.maximum(m, jnp.max(s, axis=-1, keepdims=True))
    den = jnp.exp(sink_col - m)
    acc = None
    for s, (_, v, _) in zip(ss, parts):
        e = jnp.exp(s - m)
        den = den + jnp.sum(e, axis=-1, keepdims=True)
        o = jnp.dot(e.astype(BF16), v, preferred_element_type=F32)
        acc = o if acc is None else acc + o
    return acc / den


def _attn_heads(q, sink_ref, parts_for_head, o_ref):
    t = q.shape[0]
    outs = []
    for hk in range(N_KV_HEADS):
        q4 = jnp.concatenate(
            [q[:, (hk * GQA_GROUP + g) * HEAD_DIM:(hk * GQA_GROUP + g + 1) * HEAD_DIM] for g in range(GQA_GROUP)], axis=0)
        sink_col = jnp.concatenate(
            [jnp.full((t, 1), sink_ref[hk * GQA_GROUP + g], F32) for g in range(GQA_GROUP)], axis=0)
        o4 = _softmax_av(q4, sink_col, parts_for_head(hk))
        outs.extend([o4[g * t:(g + 1) * t] for g in range(GQA_GROUP)])
    o_ref[0] = jnp.concatenate(outs, axis=1)


def _win_attn_body(sink_ref, q_ref, k0_ref, k1_ref, k2_ref, v0_ref, v1_ref, v2_ref, kc_ref, vc_ref, o_ref, *, seq_len):
    n = pl.program_id(1)
    blk = q_ref.shape[1]
    kb = jnp.concatenate([k0_ref[0], k1_ref[0], k2_ref[0]], axis=0)
    vb = jnp.concatenate([v0_ref[0], v1_ref[0], v2_ref[0]], axis=0).astype(BF16)
    kc = kc_ref[0].astype(BF16)
    vc = vc_ref[0].astype(BF16)
    qi = lax.broadcasted_iota(jnp.int32, (blk, 3 * blk), 0)
    kj = lax.broadcasted_iota(jnp.int32, (blk, 3 * blk), 1)
    kpos = (n - 1) * blk + kj
    diff = qi + blk - kj
    valid = (kpos >= 0) & (kpos < seq_len) & (diff <= WINDOW) & (diff >= -WINDOW)
    valid4 = jnp.concatenate([valid] * GQA_GROUP, axis=0)

    def parts(hk):
        sl = slice(hk * HEAD_DIM, (hk + 1) * HEAD_DIM)
        return [(kb[:, sl], vb[:, sl], valid4), (kc[:, sl], vc[:, sl], None)]

    _attn_heads(q_ref[0], sink_ref, parts, o_ref)


def window_attention(qr, kr, p, p_ctx, sink, k_col, v_col):
    bx, lx, nq = qr.shape
    nk = kr.shape[2]
    lc = p_ctx.shape[1]
    blk = ATTN_BLK
    nb = lx // blk
    lo = lambda b, n: (b, jnp.maximum(n - 1, 0), 0)
    mid = lambda b, n: (b, n, 0)
    hi = lambda b, n: (b, jnp.minimum(n + 1, nb - 1), 0)
    vlo = lambda b, n: (b, jnp.maximum(n - 1, 0), v_col)
    vmid = lambda b, n: (b, n, v_col)
    vhi = lambda b, n: (b, jnp.minimum(n + 1, nb - 1), v_col)
    return pl.pallas_call(
        functools.partial(_win_attn_body, seq_len=lx),
        out_shape=jax.ShapeDtypeStruct((bx, lx, nq), F32),
        grid=(bx, nb),
        in_specs=[pl.BlockSpec(memory_space=pltpu.SMEM),
                  pl.BlockSpec((1, blk, nq), mid),
                  pl.BlockSpec((1, blk, nk), lo), pl.BlockSpec((1, blk, nk), mid), pl.BlockSpec((1, blk, nk), hi),
                  pl.BlockSpec((1, blk, nk), vlo), pl.BlockSpec((1, blk, nk), vmid), pl.BlockSpec((1, blk, nk), vhi),
                  pl.BlockSpec((1, lc, nk), lambda b, n: (b, 0, k_col)),
                  pl.BlockSpec((1, lc, nk), lambda b, n: (b, 0, v_col))],
        out_specs=pl.BlockSpec((1, blk, nq), mid),
        compiler_params=_cparams(("parallel", "parallel")),
        name="window_attention",
    )(sink, qr, kr, kr, kr, p, p, p, p_ctx, p_ctx)


def _ctx_attn_body(sink_ref, q_ref, kc_ref, vc_ref, o_ref):
    kc = kc_ref[0].astype(BF16)
    vc = vc_ref[0].astype(BF16)
    q = (q_ref[0] * (HEAD_DIM ** -0.5)).astype(BF16)

    def parts(hk):
        sl = slice(hk * HEAD_DIM, (hk + 1) * HEAD_DIM)
        return [(kc[:, sl], vc[:, sl], None)]

    _attn_heads(q, sink_ref, parts, o_ref)


def context_attention(p_ctx, sink, q_col, k_col, v_col):
    bx, lc, _ = p_ctx.shape
    nq = N_Q_HEADS * HEAD_DIM
    nk = N_KV_HEADS * HEAD_DIM
    return pl.pallas_call(
        _ctx_attn_body,
        out_shape=jax.ShapeDtypeStruct((bx, lc, nq), F32),
        grid=(bx,),
        in_specs=[pl.BlockSpec(memory_space=pltpu.SMEM),
                  pl.BlockSpec((1, lc, nq), lambda b: (b, 0, q_col)),
                  pl.BlockSpec((1, lc, nk), lambda b: (b, 0, k_col)),
                  pl.BlockSpec((1, lc, nk), lambda b: (b, 0, v_col))],
        out_specs=pl.BlockSpec((1, lc, nq), lambda b: (b, 0, 0)),
        compiler_params=_cparams(("parallel",)),
        name="context_attention",
    )(sink, p_ctx, p_ctx, p_ctx)


def _out_proj_body(ya_ref, yb_ref, wa_ref, wb_ref, h_ref, g_ref, o_ref):
    y = jnp.dot(ya_ref[0].astype(BF16), wa_ref[...], preferred_element_type=F32)
    y = y + jnp.dot(yb_ref[0].astype(BF16), wb_ref[...], preferred_element_type=F32)
    o_ref[0] = h_ref[0] + g_ref[0] * y


def out_proj_residual(ya, yb, w_bf, h, g1):
    bx, lx, d = h.shape
    ca = ya.shape[2]
    cb = yb.shape[2]
    assert ca == cb and w_bf.shape[0] == ca + cb
    tm = _tile(lx, ROW_TILE)
    return pl.pallas_call(
        _out_proj_body,
        out_shape=jax.ShapeDtypeStruct((bx, lx, d), F32),
        grid=(bx, lx // tm),
        in_specs=[pl.BlockSpec((1, tm, ca), lambda b, i: (b, i, 0)),
                  pl.BlockSpec((1, tm, cb), lambda b, i: (b, i, 0)),
                  pl.BlockSpec((ca, d), lambda b, i: (0, 0)),
                  pl.BlockSpec((cb, d), lambda b, i: (1, 0)),
                  pl.BlockSpec((1, tm, d), lambda b, i: (b, i, 0)),
                  pl.BlockSpec((1, 1, d), lambda b, i: (b, 0, 0))],
        out_specs=pl.BlockSpec((1, tm, d), lambda b, i: (b, i, 0)),
        compiler_params=_cparams(("parallel", "parallel")),
        name="out_proj_residual",
    )(ya, yb, w_bf, w_bf, h, g1)


def _gla_consts(c, reverse):
    t = np.arange(c)
    tau = (c - 1 - t) if reverse else t
    mats = [tau[None, :] <= tau[:, None]]
    ups, masks = [], []
    m = 1
    while m < c:
        blk = tau // (2 * m)
        ref = blk * 2 * m + m - 1
        upper = (tau % (2 * m)) >= m
        mats.append(tau[None, :] <= ref[:, None])
        ups.append(upper[:, None])
        masks.append((blk[:, None] == blk[None, :]) & upper[:, None] & (~upper)[None, :])
        m *= 2
    return (jnp.asarray(np.concatenate(mats, 0), BF16), jnp.asarray(np.stack(ups), F32),
            jnp.asarray(np.stack(masks), F32))


def _gla_body(q_ref, f_ref, v_ref, lb_ref, s0_ref, cum_ref, up_ref, mask_ref, o_ref, sout_ref,
              st_scr, qs_scr, kk_scr, lf_scr, *, reverse):
    s_idx = pl.program_id(1)
    ts = q_ref.shape[1]
    c = GLA_CHUNK
    nch = ts // c
    nlev = up_ref.shape[0]
    dk = HG_DK
    nt = (((1,), (1,)), ((), ()))
    tn = (((0,), (0,)), ((), ()))

    @pl.when(s_idx == 0)
    def _():
        st_scr[...] = s0_ref[0]

    q = q_ref[0]
    lb = lb_ref[0]
    fg = lb + (1.0 - lb) * _sigmoid(f_ref[0])
    qs_scr[...] = q * _sigmoid(q)
    kk_scr[...] = 1.0 - fg
    lf_scr[...] = jnp.log(fg)

    def chunk(i, carry):
        ci = (nch - 1 - i) if reverse else i
        r0 = pl.multiple_of(ci * c, c)
        for hd in range(HG_HEADS):
            cs = slice(hd * dk, (hd + 1) * dk)
            qc = qs_scr[pl.ds(r0, c), cs]
            kc = kk_scr[pl.ds(r0, c), cs]
            lf = lf_scr[pl.ds(r0, c), cs]
            vc = v_ref[0, pl.ds(r0, c), cs]
            l1 = lf.astype(BF16)
            r1 = lf - l1.astype(F32)
            l2 = r1.astype(BF16)
            l3 = (r1 - l2.astype(F32)).astype(BF16)
            sums = jnp.dot(cum_ref[...], jnp.concatenate([l1, l2, l3], axis=1), preferred_element_type=F32)
            ball = sums[:, :dk] + sums[:, dk:2 * dk] + sums[:, 2 * dk:]
            b = ball[:c]
            st = st_scr[hd]
            o = lax.dot_general((qc * jnp.exp(b)).astype(BF16), st.astype(BF16), nt, preferred_element_type=F32)
            o = o + jnp.sum(qc * kc, axis=-1, keepdims=True) * vc
            a = jnp.zeros((c, c), F32)
            for li in range(nlev):
                d = b - ball[(li + 1) * c:(li + 2) * c]
                up = up_ref[li] > 0.0
                e = jnp.exp(jnp.where(up, d, -d))
                qd = jnp.where(up, qc * e, 0.0).astype(BF16)
                kd = jnp.where(up, 0.0, kc * e).astype(BF16)
                a = a + mask_ref[li] * lax.dot_general(qd, kd, nt, preferred_element_type=F32)
            o = o + jnp.dot(a.astype(BF16), vc.astype(BF16), preferred_element_type=F32)
            o_ref[0, pl.ds(r0, c), cs] = o
            b_end = b[0:1, :] if reverse else b[c - 1:c, :]
            kdec = kc * jnp.exp(b_end - b)
            upd = lax.dot_general(vc.astype(BF16), kdec.astype(BF16), tn, preferred_element_type=F32)
            st_scr[hd] = st * jnp.exp(b_end) + upd
        return carry

    lax.fori_loop(0, nch, chunk, 0)

    @pl.when(s_idx == pl.num_programs(1) - 1)
    def _():
        sout_ref[0] = st_scr[...]


def gla_scan(p, lb3, s0, direction, q_col, f_col, v_col):
    bx, lx, _ = p.shape
    hh = HG_HEADS
    dk = HG_DK
    f = hh * dk
    ts = _tile(lx, GLA_SEQ_BLOCK)
    ns = lx // ts
    rev = direction == 1
    sblk = (lambda s: ns - 1 - s) if rev else (lambda s: s)
    cum, up, mask = _gla_consts(GLA_CHUNK, rev)
    const = lambda a: pl.BlockSpec(a.shape, lambda b, s: (0,) * a.ndim)
    return pl.pallas_call(
        functools.partial(_gla_body, reverse=rev),
        out_shape=(jax.ShapeDtypeStruct((bx, lx, f), F32), jax.ShapeDtypeStruct((bx, hh, dk, dk), F32)),
        grid=(bx, ns),
        in_specs=[pl.BlockSpec((1, ts, f), lambda b, s: (b, sblk(s), q_col)),
                  pl.BlockSpec((1, ts, f), lambda b, s: (b, sblk(s), f_col)),
                  pl.BlockSpec((1, ts, f), lambda b, s: (b, sblk(s), v_col)),
                  pl.BlockSpec((1, 1, f), lambda b, s: (direction, 0, 0)),
                  pl.BlockSpec((1, hh, dk, dk), lambda b, s: (b, 0, 0, 0)),
                  const(cum), const(up), const(mask)],
        out_specs=(pl.BlockSpec((1, ts, f), lambda b, s: (b, sblk(s), 0)),
                   pl.BlockSpec((1, hh, dk, dk), lambda b, s: (b, 0, 0, 0))),
        scratch_shapes=[pltpu.VMEM((hh, dk, dk), F32), pltpu.VMEM((ts, f), F32),
                        pltpu.VMEM((ts, f), F32), pltpu.VMEM((ts, f), F32)],
        compiler_params=_cparams(("parallel", "arbitrary")),
        name="gla_scan_bwd" if rev else "gla_scan_fwd",
    )(p, p, p, lb3, s0, cum, up, mask)


def _hg_readout_body(of_ref, ob_ref, g_ref, gn_ref, w_ref, h_ref, g1_ref, o_ref):
    o = of_ref[0] + ob_ref[0]
    gn = gn_ref[...]
    pieces = []
    for hh in range(HG_HEADS):
        oh = o[:, hh * HG_DK:(hh + 1) * HG_DK]
        y = oh * lax.rsqrt(jnp.mean(oh * oh, axis=-1, keepdims=True) + EPS)
        pieces.append(y * gn)
    on = jnp.concatenate(pieces, axis=1)
    g = g_ref[0]
    on = on * (g * _sigmoid(g))
    y = jnp.dot(on.astype(BF16), w_ref[...], preferred_element_type=F32)
    o_ref[0] = h_ref[0] + g1_ref[0] * y


def hgrn_readout_residual(o_f, o_b, p, g_col, onorm_row, w_bf, h, g1):
    bx, lx, d = h.shape
    f = o_f.shape[2]
    tm = _tile(lx, ROW_TILE)
    return pl.pallas_call(
        _hg_readout_body,
        out_shape=jax.ShapeDtypeStruct((bx, lx, d), F32),
        grid=(bx, lx // tm),
        in_specs=[pl.BlockSpec((1, tm, f), lambda b, i: (b, i, 0)),
                  pl.BlockSpec((1, tm, f), lambda b, i: (b, i, 0)),
                  pl.BlockSpec((1, tm, f), lambda b, i: (b, i, g_col)),
                  pl.BlockSpec((1, HG_DK), lambda b, i: (0, 0)),
                  pl.BlockSpec((f, d), lambda b, i: (0, 0)),
                  pl.BlockSpec((1, tm, d), lambda b, i: (b, i, 0)),
                  pl.BlockSpec((1, 1, d), lambda b, i: (b, 0, 0))],
        out_specs=pl.BlockSpec((1, tm, d), lambda b, i: (b, i, 0)),
        compiler_params=_cparams(("parallel", "parallel")),
        name="hgrn_readout_residual",
    )(o_f, o_b, p, onorm_row, w_bf, h, g1)


def _pair_rows():
    rows = [(0, b) for b in range(16)] + [(1, b) for b in range(8)]
    for a in range(2, 8):
        rows += [(a, b) for b in range(8)]
    rows += [(a, 0) for a in range(8, 16)]
    return rows


def _top16_rows(s):
    n = s.shape[0]
    ri = lax.broadcasted_iota(jnp.int32, s.shape, 0)
    rank = jnp.full(s.shape, float(PEER_TOPK), F32)
    vals = []
    for a in range(PEER_TOPK):
        m = jnp.max(s, axis=0, keepdims=True)
        first = jnp.min(jnp.where(s == m, ri, n), axis=0, keepdims=True)
        sel = ri == first
        rank = jnp.where(sel, float(a), rank)
        s = jnp.where(sel, -jnp.inf, s)
        vals.append(m)
    return rank, jnp.concatenate(vals, axis=0)


def _peer_prep_body(x_ref, wq_ref, keys_ref, flat_ref, okf_ref, rank1_ref, bn_ref, crow_ref, av_ref, q_scr):
    x = x_ref[...]
    q_scr[...] = lax.dot_general(wq_ref[...], x, (((1,), (1,)), ((), ())), preferred_element_type=F32).astype(BF16)
    nk = N_KEYS
    flat = flat_ref[...]
    okf = okf_ref[...]

    def head(h, carry):
        r0 = pl.multiple_of(h * 2 * nk, 2 * nk)
        s0 = jnp.dot(keys_ref[h, 0], q_scr[pl.ds(r0, nk), :], preferred_element_type=F32)
        s1 = jnp.dot(keys_ref[h, 1], q_scr[pl.ds(r0 + nk, nk), :], preferred_element_type=F32)
        rank0, v0 = _top16_rows(s0)
        rank1, v1 = _top16_rows(s1)
        blocks = [v0[0:1] + v1, v0[1:2] + v1[0:8]]
        blocks += [v0[a:a + 1] + v1[0:8] for a in range(2, 8)]
        blocks += [v0[8:16] + v1[0:1]]
        cand = jnp.concatenate(blocks, axis=0)
        cand = jnp.where(okf > 0.0, cand, -jnp.inf)
        top = cand[0:1]
        work = cand
        selm = jnp.zeros(cand.shape, F32)
        for _ in range(PEER_TOPK):
            m = jnp.max(work, axis=0, keepdims=True)
            first = jnp.min(jnp.where(work == m, flat, 1 << 20), axis=0, keepdims=True)
            sel = flat == first
            selm = jnp.where(sel, 1.0, selm)
            work = jnp.where(sel, -jnp.inf, work)
        z = jnp.sum(selm * jnp.exp(jnp.where(selm > 0.0, cand - top, 0.0)), axis=0, keepdims=True)
        cnt = [jnp.sum(selm[0:16], axis=0, keepdims=True)]
        cnt += [jnp.sum(selm[16 + 8 * (a - 1):24 + 8 * (a - 1)], axis=0, keepdims=True) for a in range(1, 8)]
        cnt8 = selm[72:80]
        crow = jnp.zeros(rank0.shape, F32)
        for a in range(8):
            crow = jnp.where(rank0 == float(a), cnt[a], crow)
        for a in range(8, 16):
            crow = jnp.where(rank0 == float(a), cnt8[a - 8:a - 7], crow)
        av = jnp.where(rank0 < float(PEER_TOPK), jnp.exp(s0 - v0[0:1]), 0.0)
        bn = jnp.where(rank1 < float(PEER_TOPK), jnp.exp(s1 - v1[0:1]), 0.0) / z
        rank1_ref[h] = rank1.astype(BF16)
        bn_ref[h] = bn.astype(BF16)
        crow_ref[h] = crow
        av_ref[h] = av
        return carry

    lax.fori_loop(0, PEER_HEADS, head, 0)


def peer_prep(x_bf, wq_t, keys_bf):
    t, d = x_bf.shape
    tm = _tile(t, PEER_PREP_TOK)
    rows = _pair_rows()
    flat = jnp.asarray([[a * 16 + b] for a, b in rows], jnp.int32)
    okf = jnp.asarray([[1.0 if (a + 1) * (b + 1) <= PEER_TOPK else 0.0] for a, b in rows], F32)
    out = jax.ShapeDtypeStruct((PEER_HEADS, N_KEYS, t), F32)
    out_bf = jax.ShapeDtypeStruct((PEER_HEADS, N_KEYS, t), BF16)
    ospec = pl.BlockSpec((PEER_HEADS, N_KEYS, tm), lambda i: (0, 0, i))
    return pl.pallas_call(
        _peer_prep_body,
        out_shape=(out_bf, out_bf, out, out),
        grid=(t // tm,),
        in_specs=[pl.BlockSpec((tm, d), lambda i: (i, 0)),
                  pl.BlockSpec(wq_t.shape, lambda i: (0, 0)),
                  pl.BlockSpec(keys_bf.shape, lambda i: (0, 0, 0, 0)),
                  pl.BlockSpec(flat.shape, lambda i: (0, 0)),
                  pl.BlockSpec(okf.shape, lambda i: (0, 0))],
        out_specs=(ospec, ospec, ospec, ospec),
        scratch_shapes=[pltpu.VMEM((wq_t.shape[0], tm), BF16)],
        compiler_params=_cparams(("parallel",)),
        name="peer_prep",
    )(x_bf, wq_t, keys_bf, flat, okf)


def _row_tile_bf16(row):
    tile = jnp.broadcast_to(row, (BF16_SUBLANES, row.shape[1])).astype(BF16)
    return pltpu.repeat(tile, N_KEYS // BF16_SUBLANES, axis=0)


def _peer_main_body(x_ref, u_ref, v_ref, rank1_ref, bn_ref, crow_ref, av_ref, h_ref, g2_ref, fg_ref, o_ref,
                    acc_scr, *, final_norm):
    j = pl.program_id(1)
    tm = x_ref.shape[0]
    nrow = u_ref.shape[0] // N_KEYS
    sub = PEER_SUB_ROWS * N_KEYS
    x = x_ref[...]

    @pl.when(j == 0)
    def _():
        acc_scr[...] = jnp.zeros(acc_scr.shape, F32)

    nsb = nrow // PEER_SUB_ROWS

    def pre_act(sb):
        return lax.dot_general(u_ref[sb * sub:(sb + 1) * sub, :], x, (((1,), (1,)), ((), ())),
                               preferred_element_type=F32)

    pts = []
    nxt = pre_act(0)
    for sb in range(nsb):
        act = nxt
        if sb + 1 < nsb:
            nxt = pre_act(sb + 1)
        rows = []
        for rr in range(PEER_SUB_ROWS):
            e1 = j * nrow + sb * PEER_SUB_ROWS + rr
            cols = []
            for c0 in range(0, tm, PEER_CW):
                cs = slice(c0, c0 + PEER_CW)
                w = None
                for hd in range(PEER_HEADS):
                    cr = _row_tile_bf16(crow_ref[hd, pl.ds(e1, 1), cs])
                    ar = _row_tile_bf16(av_ref[hd, pl.ds(e1, 1), cs])
                    t = jnp.where(rank1_ref[hd, :, cs] < cr, bn_ref[hd, :, cs], jnp.zeros((), BF16)) * ar
                    w = t if w is None else w + t
                a = act[rr * N_KEYS:(rr + 1) * N_KEYS, cs]
                gelu = 0.5 * a * (1.0 + lax.erf(a * (2.0 ** -0.5)))
                cols.append(gelu.astype(BF16) * w)
            rows.append(jnp.concatenate(cols, axis=1))
        pts.extend(rows)
    pt = jnp.concatenate(pts, axis=0)
    acc_scr[...] += lax.dot_general(pt, v_ref[...], (((0,), (0,)), ((), ())), preferred_element_type=F32)

    @pl.when(j == pl.num_programs(1) - 1)
    def _():
        hn = h_ref[...] + g2_ref[0] * acc_scr[...]
        if final_norm:
            hn = hn * lax.rsqrt(jnp.mean(hn * hn, axis=-1, keepdims=True) + EPS) * fg_ref[...]
        o_ref[...] = hn


def peer_main(x_bf, u_bf, v_bf, prep, h2, g2, tokens_per_batch, final_g, final_norm):
    t, d = x_bf.shape
    e = u_bf.shape[0]
    tm = _tile(tokens_per_batch, PEER_TOK)
    te = PEER_ROWS * N_KEYS
    assert e % te == 0 and tm % PEER_CW == 0
    tpb = tokens_per_batch // tm
    assert PEER_ROWS % PEER_SUB_ROWS == 0
    pspec = pl.BlockSpec((PEER_HEADS, N_KEYS, tm), lambda i, j: (0, 0, i))
    return pl.pallas_call(
        functools.partial(_peer_main_body, final_norm=final_norm),
        out_shape=jax.ShapeDtypeStruct((t, d), F32),
        grid=(t // tm, e // te),
        in_specs=[pl.BlockSpec((tm, d), lambda i, j: (i, 0)),
                  pl.BlockSpec((te, d), lambda i, j: (j, 0)),
                  pl.BlockSpec((te, d), lambda i, j: (j, 0)),
                  pspec, pspec, pspec, pspec,
                  pl.BlockSpec((tm, d), lambda i, j: (i, 0)),
                  pl.BlockSpec((1, 1, d), lambda i, j: (i // tpb, 0, 0)),
                  pl.BlockSpec((1, d), lambda i, j: (0, 0))],
        out_specs=pl.BlockSpec((tm, d), lambda i, j: (i, 0)),
        scratch_shapes=[pltpu.VMEM((tm, d), F32)],
        compiler_params=_cparams(("parallel", "arbitrary")),
        name="peer_dense",
    )(x_bf, u_bf, v_bf, *prep, h2, g2, final_g)


def peer_residual(h, norm_g, sh, sc, g2, wq_t, keys_bf, u_bf, v_bf, final_g, final_norm):
    bx, lx, d = h.shape
    a = norm_mod(h, norm_g, sh, sc).reshape(bx * lx, d)
    prep = peer_prep(a, wq_t, keys_bf)
    out = peer_main(a, u_bf, v_bf, prep, h.reshape(bx * lx, d), g2, lx, final_g, final_norm)
    return out.reshape(bx, lx, d)


def kernel(x, c, ctx, c_ctx, ada_w, ada_b, norm1_g, norm2_g, final_g, ab_w_in, ab_w_out, hy_conv_w, hy_conv_b,
           hy_filt_w1, hy_filt_b1, hy_filt_w2, hy_filt_b2, hy_filt_w3, hy_filt_freq, hy_skip, attn_sink,
           hg_w_in, hg_w_out, hg_lb_logits, hg_onorm_g, peer_wq, peer_keys, peer_u, peer_v):
    bsz, seq, d = x.shape
    depth = ada_w.shape[0]
    assert depth == 2
    lb_p = jax.nn.softmax(hg_lb_logits.astype(F32), axis=0)
    lb_all = jnp.cumsum(lb_p, axis=0) - lb_p[0:1]

    c16 = jnp.concatenate([c, c_ctx[None], jnp.zeros((16 - bsz - 1, d), F32)], axis=0)
    final_row = final_g[None]
    h_lat, h_ctx = x, ctx

    for l in range(depth):
        need_ctx = l < depth - 1
        mod = ada_mod(c16, ada_w[l], ada_b[l][None])
        lat = [mod[:bsz, i * d:(i + 1) * d][:, None, :] for i in range(6)]
        cx = [jnp.broadcast_to(mod[bsz:bsz + 1, i * d:(i + 1) * d][:, None, :], (bsz, 1, d)) for i in range(6)]
        n1 = norm1_g[l][None]
        n2 = norm2_g[l][None]
        wq_t = peer_wq[l].T.astype(BF16)
        keys_bf = peer_keys[l].astype(BF16)
        u_bf = peer_u[l].astype(BF16)
        v_bf = peer_v[l].astype(BF16)
        j = l // 2
        if l % 2 == 0:
            w_in = ab_w_in[j].astype(BF16)
            w_out = ab_w_out[j].astype(BF16)
            p_lat = norm_mod_matmul(h_lat, n1, lat[0], lat[1], w_in)
            p_ctx = norm_mod_matmul(h_ctx, n1, cx[0], cx[1], w_in)
            hy = (hy_conv_w[j], hy_conv_b[j][None], hy_filt_w1[j], hy_filt_b1[j], hy_filt_w2[j], hy_filt_b2[j],
                  hy_filt_w3[j], hy_filt_freq[j], hy_skip[j])
            cos, sin = _rope_tables(seq)
            qr, kr = rope_qk(p_lat, cos, sin, 3, 16)
            sink = attn_sink[j]
            at_lat = window_attention(qr, kr, p_lat, p_ctx, sink, 16, 17)
            hy_lat = hyena_mixer(p_lat, *hy)
            h_lat = out_proj_residual(hy_lat, at_lat, w_out, h_lat, lat[2])
            if need_ctx:
                at_ctx = context_attention(p_ctx, sink, 3, 16, 17)
                hy_ctx = hyena_mixer(p_ctx, *hy)
                h_ctx = out_proj_residual(hy_ctx, at_ctx, w_out, h_ctx, cx[2])
        else:
            w_in = hg_w_in[j].astype(BF16)
            w_out = hg_w_out[j].astype(BF16)
            lb3 = lb_all[l].astype(F32)[:, None, :]
            p_lat = norm_mod_matmul(h_lat, n1, lat[0], lat[1], w_in)
            p_ctx = norm_mod_matmul(h_ctx, n1, cx[0], cx[1], w_in)
            s0 = jnp.zeros((bsz, HG_HEADS, HG_DK, HG_DK), F32)
            o_cf, s_cf = gla_scan(p_ctx, lb3, s0, 0, 0, 1, 3)
            o_cb, s_cb = gla_scan(p_ctx, lb3, s0, 1, 0, 2, 3)
            o_lf, _ = gla_scan(p_lat, lb3, s_cf, 0, 0, 1, 3)
            o_lb, _ = gla_scan(p_lat, lb3, s_cb, 1, 0, 2, 3)
            onorm = hg_onorm_g[j].astype(F32)[None]
            h_lat = hgrn_readout_residual(o_lf, o_lb, p_lat, 4, onorm, w_out, h_lat, lat[2])
            if need_ctx:
                h_ctx = hgrn_readout_residual(o_cf, o_cb, p_ctx, 4, onorm, w_out, h_ctx, cx[2])
        last = l == depth - 1
        h_lat = peer_residual(h_lat, n2, lat[3], lat[4], lat[5], wq_t, keys_bf, u_bf, v_bf, final_row, last)
        if need_ctx:
            h_ctx = peer_residual(h_ctx, n2, cx[3], cx[4], cx[5], wq_t, keys_bf, u_bf, v_bf, final_row, False)
    return h_lat
```

```python
import functools
import math

import jax
import jax.numpy as jnp
import numpy as np
from jax import lax
from jax.experimental import pallas as pl
from jax.experimental.pallas import tpu as pltpu

F32 = jnp.float32
BF16 = jnp.bfloat16
EPS = 1e-6

HY_C = 512
HY_EMB = 33
HY_EMB_PAD = 40
HY_FILT_W = 64
HY_DECAY_SHORT_PCT = 0.3
HY_DECAY_LONG_PCT = 1.5
HY_TARGET = 1e-2
HEAD_DIM = 64
N_Q_HEADS = 8
N_KV_HEADS = 2
GQA_GROUP = 4
WINDOW = 128
ATTN_BLK = 128
GRID_W = 64
ROPE_BASE = 10000.0
HG_HEADS = 8
HG_DK = 128
PEER_HEADS = 8
PEER_TOPK = 16
N_KEYS = 128

LANES = 128
BF16_SUBLANES = 16
VMEM_LIMIT_BYTES = 56 * 1024 * 1024
ROW_TILE = 512
GLA_CHUNK = 64
GLA_SEQ_BLOCK = 256
PEER_PREP_TOK = 256
PEER_TOK = 512
PEER_ROWS = 8
PEER_SUB_ROWS = 2
PEER_CW = 256
NEG_BIG = -1e30


def _cparams(sem):
    return pltpu.CompilerParams(dimension_semantics=sem, vmem_limit_bytes=VMEM_LIMIT_BYTES)


def _sigmoid(x):
    return 1.0 / (1.0 + jnp.exp(-x))


def _tile(n, t):
    t = min(n, t)
    assert n % t == 0, (n, t)
    return t


def _ada_body(c_ref, w_ref, b_ref, o_ref):
    c = c_ref[...]
    s = c * _sigmoid(c)
    o_ref[...] = jnp.dot(s.astype(BF16), w_ref[...].astype(BF16), preferred_element_type=F32) + b_ref[...]


def ada_mod(c16, w, b):
    d, n = w.shape
    tn = _tile(n, 1536)
    return pl.pallas_call(
        _ada_body,
        out_shape=jax.ShapeDtypeStruct((c16.shape[0], n), F32),
        grid=(n // tn,),
        in_specs=[pl.BlockSpec(c16.shape, lambda j: (0, 0)),
                  pl.BlockSpec((d, tn), lambda j: (0, j)),
                  pl.BlockSpec((1, tn), lambda j: (0, j))],
        out_specs=pl.BlockSpec((c16.shape[0], tn), lambda j: (0, j)),
        compiler_params=_cparams(("arbitrary",)),
        name="ada_mod",
    )(c16, w, b)


def _norm_mod(x, g, sh, sc):
    y = x * lax.rsqrt(jnp.mean(x * x, axis=-1, keepdims=True) + EPS)
    return (y * g) * (1.0 + sc) + sh


def _nm_matmul_body(h_ref, g_ref, sh_ref, sc_ref, w_ref, p_ref, a_scr):
    @pl.when(pl.program_id(2) == 0)
    def _():
        a_scr[...] = _norm_mod(h_ref[0], g_ref[...], sh_ref[0], sc_ref[0]).astype(BF16)

    p_ref[0] = jnp.dot(a_scr[...], w_ref[...], preferred_element_type=F32)


def norm_mod_matmul(h, g, sh, sc, w_bf):
    bx, lx, d = h.shape
    n = w_bf.shape[1]
    tm = _tile(lx, ROW_TILE)
    tn = _tile(n, 1280 if n % 1280 == 0 else (1152 if n % 1152 == 0 else 1024))
    return pl.pallas_call(
        _nm_matmul_body,
        out_shape=jax.ShapeDtypeStruct((bx, lx, n), F32),
        grid=(bx, lx // tm, n // tn),
        in_specs=[pl.BlockSpec((1, tm, d), lambda b, i, j: (b, i, 0)),
                  pl.BlockSpec((1, d), lambda b, i, j: (0, 0)),
                  pl.BlockSpec((1, 1, d), lambda b, i, j: (b, 0, 0)),
                  pl.BlockSpec((1, 1, d), lambda b, i, j: (b, 0, 0)),
                  pl.BlockSpec((d, tn), lambda b, i, j: (0, j))],
        out_specs=pl.BlockSpec((1, tm, tn), lambda b, i, j: (b, i, j)),
        scratch_shapes=[pltpu.VMEM((tm, d), BF16)],
        compiler_params=_cparams(("parallel", "parallel", "arbitrary")),
        name="norm_mod_matmul",
    )(h, g, sh, sc, w_bf)


def _nm_only_body(h_ref, g_ref, sh_ref, sc_ref, a_ref):
    a_ref[0] = _norm_mod(h_ref[0], g_ref[...], sh_ref[0], sc_ref[0]).astype(BF16)


def norm_mod(h, g, sh, sc):
    bx, lx, d = h.shape
    tm = _tile(lx, ROW_TILE)
    return pl.pallas_call(
        _nm_only_body,
        out_shape=jax.ShapeDtypeStruct((bx, lx, d), BF16),
        grid=(bx, lx // tm),
        in_specs=[pl.BlockSpec((1, tm, d), lambda b, i: (b, i, 0)),
                  pl.BlockSpec((1, d), lambda b, i: (0, 0)),
                  pl.BlockSpec((1, 1, d), lambda b, i: (b, 0, 0)),
                  pl.BlockSpec((1, 1, d), lambda b, i: (b, 0, 0))],
        out_specs=pl.BlockSpec((1, tm, d), lambda b, i: (b, i, 0)),
        compiler_params=_cparams(("parallel", "parallel")),
        name="norm_mod",
    )(h, g, sh, sc)


def _mm_body(a_ref, b_ref, o_ref):
    o_ref[...] = jnp.dot(a_ref[...], b_ref[...], preferred_element_type=F32)


def matmul_bf16(a, b):
    m, k = a.shape
    n = b.shape[1]
    tm = _tile(m, 1024)
    tn = _tile(n, 1024)
    return pl.pallas_call(
        _mm_body,
        out_shape=jax.ShapeDtypeStruct((m, n), F32),
        grid=(m // tm, n // tn),
        in_specs=[pl.BlockSpec((tm, k), lambda i, j: (i, 0)),
                  pl.BlockSpec((k, tn), lambda i, j: (0, j))],
        out_specs=pl.BlockSpec((tm, tn), lambda i, j: (i, j)),
        compiler_params=_cparams(("parallel", "parallel")),
        name="matmul_bf16",
    )(a, b)


def _filter_body(z_ref, w1_ref, b1_ref, w2_ref, b2_ref, w3_ref, fr_ref, dec_ref, o_ref):
    hi = lax.Precision.HIGHEST
    fr = fr_ref[...]
    hdn = jnp.sin(fr * (jnp.dot(z_ref[...], w1_ref[...], precision=hi, preferred_element_type=F32) + b1_ref[...]))
    hdn = jnp.sin(fr * (jnp.dot(hdn, w2_ref[...], precision=hi, preferred_element_type=F32) + b2_ref[...]))
    h = jnp.dot(hdn, w3_ref[...], precision=hi, preferred_element_type=F32)
    dec = dec_ref[...]
    c = dec.shape[1]
    h0 = h[:, :c] * dec
    h1 = h[:, c:] * dec
    nrm = jnp.sum(jnp.abs(h0) + jnp.abs(h1), axis=0, keepdims=True)
    inv = 1.0 / nrm
    ri = lax.broadcasted_iota(jnp.int32, h1.shape, 0)
    o_ref[:, :c] = (h0 * inv).astype(BF16)
    o_ref[:, c:] = jnp.where(ri == 0, 0.0, h1 * inv).astype(BF16)


def hyena_filters(zfeat, w1p, b1, w2, b2, w3, freq, decay):
    l = zfeat.shape[0]
    c = decay.shape[1]
    n_order = w3.shape[1] // (2 * c)
    full = lambda shape: pl.BlockSpec(shape, lambda o: (0,) * len(shape))
    return pl.pallas_call(
        _filter_body,
        out_shape=jax.ShapeDtypeStruct((l, n_order * 2 * c), BF16),
        grid=(n_order,),
        in_specs=[full(zfeat.shape), full(w1p.shape), full(b1.shape), full(w2.shape), full(b2.shape),
                  pl.BlockSpec((w3.shape[0], 2 * c), lambda o: (0, o)),
                  full(freq.shape), full(decay.shape)],
        out_specs=pl.BlockSpec((l, 2 * c), lambda o: (0, o)),
        compiler_params=_cparams(("arbitrary",)),
        name="hyena_filters",
    )(zfeat, w1p, b1, w2, b2, w3, freq, decay)


def _short_conv_body(u_ref, w_ref, b_ref, o_ref):
    u = u_ref[0]
    l = u.shape[0]
    ri = lax.broadcasted_iota(jnp.int32, u.shape, 0)
    prev = jnp.where(ri == 0, 0.0, pltpu.roll(u, 1, 0))
    nxt = jnp.where(ri == l - 1, 0.0, pltpu.roll(u, l - 1, 0))
    w = w_ref[...]
    o_ref[0] = prev * w[0:1] + u * w[1:2] + nxt * w[2:3] + b_ref[...]


def short_conv(p, conv_w, conv_b):
    bx, lx, _ = p.shape
    c3 = conv_w.shape[1]
    tc = _tile(c3, 512)
    return pl.pallas_call(
        _short_conv_body,
        out_shape=jax.ShapeDtypeStruct((bx, lx, c3), F32),
        grid=(bx, c3 // tc),
        in_specs=[pl.BlockSpec((1, lx, tc), lambda b, j: (b, 0, j)),
                  pl.BlockSpec((3, tc), lambda b, j: (0, j)),
                  pl.BlockSpec((1, tc), lambda b, j: (0, j))],
        out_specs=pl.BlockSpec((1, lx, tc), lambda b, j: (b, 0, j)),
        compiler_params=_cparams(("parallel", "parallel")),
        name="hyena_short_conv",
    )(p, conv_w, conv_b)


def _hy_fwd_body(z_ref, f_ref, h0_ref, h1_ref, y_ref):
    z = z_ref[0].astype(BF16)
    kt = f_ref.shape[1]
    zre = jnp.dot(f_ref[0], z, preferred_element_type=F32)
    zim = jnp.dot(f_ref[1], z, preferred_element_type=F32)
    k0 = (pl.program_id(0) * kt + lax.broadcasted_iota(jnp.int32, zre.shape, 0)) == 0
    hre = h0_ref[0] + h1_ref[0]
    him = jnp.where(k0, h0_ref[1] + h1_ref[1], h0_ref[1] - h1_ref[1])
    yre = jnp.where(k0, zre * hre, zre * hre - zim * him)
    yim = jnp.where(k0, zim * him, zre * him + zim * hre)
    y_ref[0, 0] = yre.astype(BF16)
    y_ref[0, 1] = yim.astype(BF16)


def hyena_fwd(z_src, z_col, f3, fh3, order, c):
    bx, lx, _ = z_src.shape
    kt = _tile(lx, 512)
    return pl.pallas_call(
        _hy_fwd_body,
        out_shape=jax.ShapeDtypeStruct((bx, 2, lx, c), BF16),
        grid=(lx // kt, bx),
        in_specs=[pl.BlockSpec((1, lx, c), lambda k, b: (b, 0, z_col)),
                  pl.BlockSpec((2, kt, lx), lambda k, b: (0, k, 0)),
                  pl.BlockSpec((2, kt, c), lambda k, b: (0, k, 2 * order)),
                  pl.BlockSpec((2, kt, c), lambda k, b: (0, k, 2 * order + 1))],
        out_specs=pl.BlockSpec((1, 2, kt, c), lambda k, b: (b, 0, k, 0)),
        compiler_params=_cparams(("parallel", "parallel")),
        name="hyena_dft_fwd",
    )(z_src, f3, fh3, fh3)


def _hy_inv_body(y_ref, fi_ref, gate_ref, z_ref, skip_ref, o_ref):
    conv = jnp.dot(fi_ref[...], y_ref[0], preferred_element_type=F32)
    o_ref[0] = gate_ref[0] * (conv + skip_ref[...] * z_ref[0])


def hyena_inv(yf, finv, gate_src, gate_col, z_src, z_col, skip_row):
    bx, l2, c = yf.shape
    lx = l2 // 2
    tl = _tile(lx, 512)
    return pl.pallas_call(
        _hy_inv_body,
        out_shape=jax.ShapeDtypeStruct((bx, lx, c), F32),
        grid=(lx // tl, bx),
        in_specs=[pl.BlockSpec((1, l2, c), lambda t, b: (b, 0, 0)),
                  pl.BlockSpec((tl, l2), lambda t, b: (t, 0)),
                  pl.BlockSpec((1, tl, c), lambda t, b: (b, t, gate_col)),
                  pl.BlockSpec((1, tl, c), lambda t, b: (b, t, z_col)),
                  pl.BlockSpec((1, c), lambda t, b: (0, 0))],
        out_specs=pl.BlockSpec((1, tl, c), lambda t, b: (b, t, 0)),
        compiler_params=_cparams(("parallel", "parallel")),
        name="hyena_dft_inv",
    )(yf, finv, gate_src, z_src, skip_row)


def _dft_mats(l):
    n = 2 * l
    k = jnp.arange(l, dtype=jnp.int32)[:, None]
    s = jnp.arange(l, dtype=jnp.int32)[None, :]
    ang = ((k * s) % n).astype(F32) * (2.0 * math.pi / n)
    cosm = jnp.cos(ang)
    sinm = -jnp.sin(ang)
    nyq = jnp.where(s % 2 == 0, 1.0, -1.0).astype(F32)
    imag = jnp.where(k == 0, nyq, sinm)
    fwd = jnp.concatenate([cosm, imag], axis=0)
    scale = jnp.where(jnp.arange(l) == 0, 1.0 / n, 2.0 / n).astype(F32)
    inv = jnp.concatenate([cosm.T * scale[None, :], imag.T * scale[None, :]], axis=1)
    return fwd.astype(BF16), inv.astype(BF16)


def _filter_features(l):
    bands = (HY_EMB - 1) // 2
    t = jnp.linspace(0.0, 1.0, l, dtype=F32)[:, None]
    w = 2.0 * math.pi * jnp.arange(l, dtype=F32)[:, None] / l
    f = jnp.linspace(1e-4, bands - 1, bands, dtype=F32)[None, :]
    z = jnp.concatenate([t, jnp.cos(f * w), -jnp.sin(f * w)], axis=-1)
    z = jnp.pad(z, ((0, 0), (0, HY_EMB_PAD - HY_EMB)))
    max_decay = math.log(HY_TARGET) / HY_DECAY_SHORT_PCT
    min_decay = math.log(HY_TARGET) / HY_DECAY_LONG_PCT
    deltas = jnp.abs(jnp.linspace(min_decay, max_decay, HY_C, dtype=F32))
    return z, jnp.exp(-t * deltas)


def hyena_mixer(p, conv_w, conv_b, fw1, fb1, fw2, fb2, fw3, ffreq, d_skip):
    bx, lx, _ = p.shape
    c = HY_C
    zfeat, decay = _filter_features(lx)
    w1p = jnp.pad(fw1, ((0, HY_EMB_PAD - HY_EMB), (0, 0)))
    filt = hyena_filters(zfeat, w1p, fb1[None], fw2, fb2[None], fw3, ffreq[None], decay)
    fwd, inv = _dft_mats(lx)
    fh3 = matmul_bf16(fwd, filt).reshape(2, lx, filt.shape[1])
    f3 = fwd.reshape(2, lx, lx)
    uc = short_conv(p, conv_w, conv_b)
    y0 = hyena_fwd(uc, 0, f3, fh3, 0, c).reshape(bx, 2 * lx, c)
    z1 = hyena_inv(y0, inv, uc, 1, uc, 0, d_skip[0:1])
    y1 = hyena_fwd(z1, 0, f3, fh3, 1, c).reshape(bx, 2 * lx, c)
    return hyena_inv(y1, inv, uc, 2, z1, 0, d_skip[1:2])


def _rope_tables(l):
    rows = l // GRID_W
    r = jnp.broadcast_to(jnp.arange(rows)[:, None], (rows, GRID_W)).reshape(-1).astype(F32)
    col = jnp.broadcast_to(jnp.arange(GRID_W)[None, :], (rows, GRID_W)).reshape(-1).astype(F32)
    nf = HEAD_DIM // 4
    inv = ROPE_BASE ** (-jnp.arange(nf, dtype=F32) / nf)
    ar = r[:, None] * inv
    ac = col[:, None] * inv
    ang = jnp.concatenate([ar, ar, ac, ac], axis=-1)
    sign = jnp.concatenate([-jnp.ones((nf,)), jnp.ones((nf,))] * 2).astype(F32)
    cos = jnp.tile(jnp.cos(ang), (1, N_Q_HEADS))
    sin = jnp.tile(jnp.sin(ang) * sign[None, :], (1, N_Q_HEADS))
    return cos, sin


def _rope_body(q_ref, k_ref, cos_ref, sin_ref, qo_ref, ko_ref):
    def rot(x, cos, sin):
        n = x.shape[1]
        lane = lax.broadcasted_iota(jnp.int32, x.shape, 1)
        first = (lane % 32) < 16
        partner = jnp.where(first, pltpu.roll(x, n - 16, 1), pltpu.roll(x, 16, 1))
        return x * cos + partner * sin

    nk = k_ref.shape[2]
    qo_ref[0] = (rot(q_ref[0], cos_ref[...], sin_ref[...]) * (HEAD_DIM ** -0.5)).astype(BF16)
    ko_ref[0] = rot(k_ref[0], cos_ref[:, :nk], sin_ref[:, :nk]).astype(BF16)


def rope_qk(p, cos, sin, q_col, k_col):
    bx, lx, _ = p.shape
    nq = N_Q_HEADS * HEAD_DIM
    nk = N_KV_HEADS * HEAD_DIM
    tr = _tile(lx, ROW_TILE)
    return pl.pallas_call(
        _rope_body,
        out_shape=(jax.ShapeDtypeStruct((bx, lx, nq), BF16), jax.ShapeDtypeStruct((bx, lx, nk), BF16)),
        grid=(bx, lx // tr),
        in_specs=[pl.BlockSpec((1, tr, nq), lambda b, i: (b, i, q_col)),
                  pl.BlockSpec((1, tr, nk), lambda b, i: (b, i, k_col)),
                  pl.BlockSpec((tr, nq), lambda b, i: (i, 0)),
                  pl.BlockSpec((tr, nq), lambda b, i: (i, 0))],
        out_specs=(pl.BlockSpec((1, tr, nq), lambda b, i: (b, i, 0)),
                   pl.BlockSpec((1, tr, nk), lambda b, i: (b, i, 0))),
        compiler_params=_cparams(("parallel", "parallel")),
        name="rope_qk",
    )(p, p, cos, sin)


def _softmax_av(q4, sink_col, parts):
    ss = []
    m = sink_col
    for k, _, mask in parts:
        s = lax.dot_general(q4, k, (((1,), (1,)), ((), ())), preferred_element_type=F32)
        if mask is not None:
            s = jnp.where(mask, s, NEG_BIG)
        ss.append(s)
        m = jnp.maximum(m, jnp.max(s, axis=-1, keepdims=True))
    den = jnp.exp(sink_col - m)
    acc = None
    for s, (_, v, _) in zip(ss, parts):
        e = jnp.exp(s - m)
        den = den + jnp.sum(e, axis=-1, keepdims=True)
        o = jnp.dot(e.astype(BF16), v, preferred_element_type=F32)
        acc = o if acc is None else acc + o
    return acc / den


def _attn_heads(q, sink_ref, parts_for_head, o_ref):
    t = q.shape[0]
    outs = []
    for hk in range(N_KV_HEADS):
        q4 = jnp.concatenate(
            [q[:, (hk * GQA_GROUP + g) * HEAD_DIM:(hk * GQA_GROUP + g + 1) * HEAD_DIM] for g in range(GQA_GROUP)], axis=0)
        sink_col = jnp.concatenate(
            [jnp.full((t, 1), sink_ref[hk * GQA_GROUP + g], F32) for g in range(GQA_GROUP)], axis=0)
        o4 = _softmax_av(q4, sink_col, parts_for_head(hk))
        outs.extend([o4[g * t:(g + 1) * t] for g in range(GQA_GROUP)])
    o_ref[0] = jnp.concatenate(outs, axis=1)


def _win_attn_body(sink_ref, q_ref, k0_ref, k1_ref, k2_ref, v0_ref, v1_ref, v2_ref, kc_ref, vc_ref, o_ref, *, seq_len):
    n = pl.program_id(1)
    blk = q_ref.shape[1]
    kb = jnp.concatenate([k0_ref[0], k1_ref[0], k2_ref[0]], axis=0)
    vb = jnp.concatenate([v0_ref[0], v1_ref[0], v2_ref[0]], axis=0).astype(BF16)
    kc = kc_ref[0].astype(BF16)
    vc = vc_ref[0].astype(BF16)
    qi = lax.broadcasted_iota(jnp.int32, (blk, 3 * blk), 0)
    kj = lax.broadcasted_iota(jnp.int32, (blk, 3 * blk), 1)
    kpos = (n - 1) * blk + kj
    diff = qi + blk - kj
    valid = (kpos >= 0) & (kpos < seq_len) & (diff <= WINDOW) & (diff >= -WINDOW)
    valid4 = jnp.concatenate([valid] * GQA_GROUP, axis=0)

    def parts(hk):
        sl = slice(hk * HEAD_DIM, (hk + 1) * HEAD_DIM)
        return [(kb[:, sl], vb[:, sl], valid4), (kc[:, sl], vc[:, sl], None)]

    _attn_heads(q_ref[0], sink_ref, parts, o_ref)


def window_attention(qr, kr, p, p_ctx, sink, k_col, v_col):
    bx, lx, nq = qr.shape
    nk = kr.shape[2]
    lc = p_ctx.shape[1]
    blk = ATTN_BLK
    nb = lx // blk
    lo = lambda b, n: (b, jnp.maximum(n - 1, 0), 0)
    mid = lambda b, n: (b, n, 0)
    hi = lambda b, n: (b, jnp.minimum(n + 1, nb - 1), 0)
    vlo = lambda b, n: (b, jnp.maximum(n - 1, 0), v_col)
    vmid = lambda b, n: (b, n, v_col)
    vhi = lambda b, n: (b, jnp.minimum(n + 1, nb - 1), v_col)
    return pl.pallas_call(
        functools.partial(_win_attn_body, seq_len=lx),
        out_shape=jax.ShapeDtypeStruct((bx, lx, nq), F32),
        grid=(bx, nb),
        in_specs=[pl.BlockSpec(memory_space=pltpu.SMEM),
                  pl.BlockSpec((1, blk, nq), mid),
                  pl.BlockSpec((1, blk, nk), lo), pl.BlockSpec((1, blk, nk), mid), pl.BlockSpec((1, blk, nk), hi),
                  pl.BlockSpec((1, blk, nk), vlo), pl.BlockSpec((1, blk, nk), vmid), pl.BlockSpec((1, blk, nk), vhi),
                  pl.BlockSpec((1, lc, nk), lambda b, n: (b, 0, k_col)),
                  pl.BlockSpec((1, lc, nk), lambda b, n: (b, 0, v_col))],
        out_specs=pl.BlockSpec((1, blk, nq), mid),
        compiler_params=_cparams(("parallel", "parallel")),
        name="window_attention",
    )(sink, qr, kr, kr, kr, p, p, p, p_ctx, p_ctx)


def _ctx_attn_body(sink_ref, q_ref, kc_ref, vc_ref, o_ref):
    kc = kc_ref[0].astype(BF16)
    vc = vc_ref[0].astype(BF16)
    q = (q_ref[0] * (HEAD_DIM ** -0.5)).astype(BF16)

    def parts(hk):
        sl = slice(hk * HEAD_DIM, (hk + 1) * HEAD_DIM)
        return [(kc[:, sl], vc[:, sl], None)]

    _attn_heads(q, sink_ref, parts, o_ref)


def context_attention(p_ctx, sink, q_col, k_col, v_col):
    bx, lc, _ = p_ctx.shape
    nq = N_Q_HEADS * HEAD_DIM
    nk = N_KV_HEADS * HEAD_DIM
    return pl.pallas_call(
        _ctx_attn_body,
        out_shape=jax.ShapeDtypeStruct((bx, lc, nq), F32),
        grid=(bx,),
        in_specs=[pl.BlockSpec(memory_space=pltpu.SMEM),
                  pl.BlockSpec((1, lc, nq), lambda b: (b, 0, q_col)),
                  pl.BlockSpec((1, lc, nk), lambda b: (b, 0, k_col)),
                  pl.BlockSpec((1, lc, nk), lambda b: (b, 0, v_col))],
        out_specs=pl.BlockSpec((1, lc, nq), lambda b: (b, 0, 0)),
        compiler_params=_cparams(("parallel",)),
        name="context_attention",
    )(sink, p_ctx, p_ctx, p_ctx)


def _out_proj_body(ya_ref, yb_ref, wa_ref, wb_ref, h_ref, g_ref, o_ref):
    y = jnp.dot(ya_ref[0].astype(BF16), wa_ref[...], preferred_element_type=F32)
    y = y + jnp.dot(yb_ref[0].astype(BF16), wb_ref[...], preferred_element_type=F32)
    o_ref[0] = h_ref[0] + g_ref[0] * y


def out_proj_residual(ya, yb, w_bf, h, g1):
    bx, lx, d = h.shape
    ca = ya.shape[2]
    cb = yb.shape[2]
    assert ca == cb and w_bf.shape[0] == ca + cb
    tm = _tile(lx, ROW_TILE)
    return pl.pallas_call(
        _out_proj_body,
        out_shape=jax.ShapeDtypeStruct((bx, lx, d), F32),
        grid=(bx, lx // tm),
        in_specs=[pl.BlockSpec((1, tm, ca), lambda b, i: (b, i, 0)),
                  pl.BlockSpec((1, tm, cb), lambda b, i: (b, i, 0)),
                  pl.BlockSpec((ca, d), lambda b, i: (0, 0)),
                  pl.BlockSpec((cb, d), lambda b, i: (1, 0)),
                  pl.BlockSpec((1, tm, d), lambda b, i: (b, i, 0)),
                  pl.BlockSpec((1, 1, d), lambda b, i: (b, 0, 0))],
        out_specs=pl.BlockSpec((1, tm, d), lambda b, i: (b, i, 0)),
        compiler_params=_cparams(("parallel", "parallel")),
        name="out_proj_residual",
    )(ya, yb, w_bf, w_bf, h, g1)


def _gla_consts(c, reverse):
    t = np.arange(c)
    tau = (c - 1 - t) if reverse else t
    mats = [tau[None, :] <= tau[:, None]]
    ups, masks = [], []
    m = 1
    while m < c:
        blk = tau // (2 * m)
        ref = blk * 2 * m + m - 1
        upper = (tau % (2 * m)) >= m
        mats.append(tau[None, :] <= ref[:, None])
        ups.append(upper[:, None])
        masks.append((blk[:, None] == blk[None, :]) & upper[:, None] & (~upper)[None, :])
        m *= 2
    return (jnp.asarray(np.concatenate(mats, 0), BF16), jnp.asarray(np.stack(ups), F32),
            jnp.asarray(np.stack(masks), F32))


def _gla_body(q_ref, f_ref, v_ref, lb_ref, s0_ref, cum_ref, up_ref, mask_ref, o_ref, sout_ref,
              st_scr, qs_scr, kk_scr, lf_scr, *, reverse):
    s_idx = pl.program_id(1)
    ts = q_ref.shape[1]
    c = GLA_CHUNK
    nch = ts // c
    nlev = up_ref.shape[0]
    dk = HG_DK
    nt = (((1,), (1,)), ((), ()))
    tn = (((0,), (0,)), ((), ()))

    @pl.when(s_idx == 0)
    def _():
        st_scr[...] = s0_ref[0]

    q = q_ref[0]
    lb = lb_ref[0]
    fg = lb + (1.0 - lb) * _sigmoid(f_ref[0])
    qs_scr[...] = q * _sigmoid(q)
    kk_scr[...] = 1.0 - fg
    lf_scr[...] = jnp.log(fg)

    def chunk(i, carry):
        ci = (nch - 1 - i) if reverse else i
        r0 = pl.multiple_of(ci * c, c)
        for hd in range(HG_HEADS):
            cs = slice(hd * dk, (hd + 1) * dk)
            qc = qs_scr[pl.ds(r0, c), cs]
            kc = kk_scr[pl.ds(r0, c), cs]
            lf = lf_scr[pl.ds(r0, c), cs]
            vc = v_ref[0, pl.ds(r0, c), cs]
            l1 = lf.astype(BF16)
            r1 = lf - l1.astype(F32)
            l2 = r1.astype(BF16)
            l3 = (r1 - l2.astype(F32)).astype(BF16)
            sums = jnp.dot(cum_ref[...], jnp.concatenate([l1, l2, l3], axis=1), preferred_element_type=F32)
            ball = sums[:, :dk] + sums[:, dk:2 * dk] + sums[:, 2 * dk:]
            b = ball[:c]
            st = st_scr[hd]
            o = lax.dot_general((qc * jnp.exp(b)).astype(BF16), st.astype(BF16), nt, preferred_element_type=F32)
            o = o + jnp.sum(qc * kc, axis=-1, keepdims=True) * vc
            a = jnp.zeros((c, c), F32)
            for li in range(nlev):
                d = b - ball[(li + 1) * c:(li + 2) * c]
                up = up_ref[li] > 0.0
                e = jnp.exp(jnp.where(up, d, -d))
                qd = jnp.where(up, qc * e, 0.0).astype(BF16)
                kd = jnp.where(up, 0.0, kc * e).astype(BF16)
                a = a + mask_ref[li] * lax.dot_general(qd, kd, nt, preferred_element_type=F32)
            o = o + jnp.dot(a.astype(BF16), vc.astype(BF16), preferred_element_type=F32)
            o_ref[0, pl.ds(r0, c), cs] = o
            b_end = b[0:1, :] if reverse else b[c - 1:c, :]
            kdec = kc * jnp.exp(b_end - b)
            upd = lax.dot_general(vc.astype(BF16), kdec.astype(BF16), tn, preferred_element_type=F32)
            st_scr[hd] = st * jnp.exp(b_end) + upd
        return carry

    lax.fori_loop(0, nch, chunk, 0)

    @pl.when(s_idx == pl.num_programs(1) - 1)
    def _():
        sout_ref[0] = st_scr[...]


def gla_scan(p, lb3, s0, direction, q_col, f_col, v_col):
    bx, lx, _ = p.shape
    hh = HG_HEADS
    dk = HG_DK
    f = hh * dk
    ts = _tile(lx, GLA_SEQ_BLOCK)
    ns = lx // ts
    rev = direction == 1
    sblk = (lambda s: ns - 1 - s) if rev else (lambda s: s)
    cum, up, mask = _gla_consts(GLA_CHUNK, rev)
    const = lambda a: pl.BlockSpec(a.shape, lambda b, s: (0,) * a.ndim)
    return pl.pallas_call(
        functools.partial(_gla_body, reverse=rev),
        out_shape=(jax.ShapeDtypeStruct((bx, lx, f), F32), jax.ShapeDtypeStruct((bx, hh, dk, dk), F32)),
        grid=(bx, ns),
        in_specs=[pl.BlockSpec((1, ts, f), lambda b, s: (b, sblk(s), q_col)),
                  pl.BlockSpec((1, ts, f), lambda b, s: (b, sblk(s), f_col)),
                  pl.BlockSpec((1, ts, f), lambda b, s: (b, sblk(s), v_col)),
                  pl.BlockSpec((1, 1, f), lambda b, s: (direction, 0, 0)),
                  pl.BlockSpec((1, hh, dk, dk), lambda b, s: (b, 0, 0, 0)),
                  const(cum), const(up), const(mask)],
        out_specs=(pl.BlockSpec((1, ts, f), lambda b, s: (b, sblk(s), 0)),
                   pl.BlockSpec((1, hh, dk, dk), lambda b, s: (b, 0, 0, 0))),
        scratch_shapes=[pltpu.VMEM((hh, dk, dk), F32), pltpu.VMEM((ts, f), F32),
                        pltpu.VMEM((ts, f), F32), pltpu.VMEM((ts, f), F32)],
        compiler_params=_cparams(("parallel", "arbitrary")),
        name="gla_scan_bwd" if rev else "gla_scan_fwd",
    )(p, p, p, lb3, s0, cum, up, mask)


def _hg_readout_body(of_ref, ob_ref, g_ref, gn_ref, w_ref, h_ref, g1_ref, o_ref):
    o = of_ref[0] + ob_ref[0]
    gn = gn_ref[...]
    pieces = []
    for hh in range(HG_HEADS):
        oh = o[:, hh * HG_DK:(hh + 1) * HG_DK]
        y = oh * lax.rsqrt(jnp.mean(oh * oh, axis=-1, keepdims=True) + EPS)
        pieces.append(y * gn)
    on = jnp.concatenate(pieces, axis=1)
    g = g_ref[0]
    on = on * (g * _sigmoid(g))
    y = jnp.dot(on.astype(BF16), w_ref[...], preferred_element_type=F32)
    o_ref[0] = h_ref[0] + g1_ref[0] * y


def hgrn_readout_residual(o_f, o_b, p, g_col, onorm_row, w_bf, h, g1):
    bx, lx, d = h.shape
    f = o_f.shape[2]
    tm = _tile(lx, ROW_TILE)
    return pl.pallas_call(
        _hg_readout_body,
        out_shape=jax.ShapeDtypeStruct((bx, lx, d), F32),
        grid=(bx, lx // tm),
        in_specs=[pl.BlockSpec((1, tm, f), lambda b, i: (b, i, 0)),
                  pl.BlockSpec((1, tm, f), lambda b, i: (b, i, 0)),
                  pl.BlockSpec((1, tm, f), lambda b, i: (b, i, g_col)),
                  pl.BlockSpec((1, HG_DK), lambda b, i: (0, 0)),
                  pl.BlockSpec((f, d), lambda b, i: (0, 0)),
                  pl.BlockSpec((1, tm, d), lambda b, i: (b, i, 0)),
                  pl.BlockSpec((1, 1, d), lambda b, i: (b, 0, 0))],
        out_specs=pl.BlockSpec((1, tm, d), lambda b, i: (b, i, 0)),
        compiler_params=_cparams(("parallel", "parallel")),
        name="hgrn_readout_residual",
    )(o_f, o_b, p, onorm_row, w_bf, h, g1)


def _pair_rows():
    rows = [(0, b) for b in range(16)] + [(1, b) for b in range(8)]
    for a in range(2, 8):
        rows += [(a, b) for b in range(8)]
    rows += [(a, 0) for a in range(8, 16)]
    return rows


def _top16_rows(s):
    n = s.shape[0]
    ri = lax.broadcasted_iota(jnp.int32, s.shape, 0)
    rank = jnp.full(s.shape, float(PEER_TOPK), F32)
    vals = []
    for a in range(PEER_TOPK):
        m = jnp.max(s, axis=0, keepdims=True)
        first = jnp.min(jnp.where(s == m, ri, n), axis=0, keepdims=True)
        sel = ri == first
        rank = jnp.where(sel, float(a), rank)
        s = jnp.where(sel, -jnp.inf, s)
        vals.append(m)
    return rank, jnp.concatenate(vals, axis=0)


def _pair_candidates(v0, v1, okf):
    blocks = [v0[0:1] + v1, v0[1:2] + v1[0:8]]
    blocks += [v0[a:a + 1] + v1[0:8] for a in range(2, 8)]
    blocks += [v0[8:16] + v1[0:1]]
    return jnp.where(okf > 0.0, jnp.concatenate(blocks, axis=0), -jnp.inf)


def _gate_arrays(s0, s1, rank0, rank1, v0, v1, cand, selm):
    top = cand[0:1]
    z = jnp.sum(selm * jnp.exp(jnp.where(selm > 0.0, cand - top, 0.0)), axis=0, keepdims=True)
    cnt = [jnp.sum(selm[0:16], axis=0, keepdims=True)]
    cnt += [jnp.sum(selm[16 + 8 * (a - 1):24 + 8 * (a - 1)], axis=0, keepdims=True) for a in range(1, 8)]
    cnt8 = selm[72:80]
    crow = jnp.zeros(rank0.shape, F32)
    for a in range(8):
        crow = jnp.where(rank0 == float(a), cnt[a], crow)
    for a in range(8, 16):
        crow = jnp.where(rank0 == float(a), cnt8[a - 8:a - 7], crow)
    av = jnp.where(rank0 < float(PEER_TOPK), jnp.exp(s0 - v0[0:1]), 0.0)
    bn = jnp.where(rank1 < float(PEER_TOPK), jnp.exp(s1 - v1[0:1]), 0.0) / z
    return rank1, bn, crow, av


def _select_exact(s0, s1, flat, okf):
    rank0, v0 = _top16_rows(s0)
    rank1, v1 = _top16_rows(s1)
    cand = _pair_candidates(v0, v1, okf)
    work = cand
    selm = jnp.zeros(cand.shape, F32)
    for _ in range(PEER_TOPK):
        m = jnp.max(work, axis=0, keepdims=True)
        first = jnp.min(jnp.where(work == m, flat, 1 << 20), axis=0, keepdims=True)
        sel = flat == first
        selm = jnp.where(sel, 1.0, selm)
        work = jnp.where(sel, -jnp.inf, work)
    return _gate_arrays(s0, s1, rank0, rank1, v0, v1, cand, selm)


def _cmp_exchange(xs, i, l, larger_first):
    hi = jnp.maximum(xs[i], xs[l])
    lo = jnp.minimum(xs[i], xs[l])
    xs[i], xs[l] = (hi, lo) if larger_first else (lo, hi)


def _bitonic_merge_desc(xs):
    n = len(xs)
    j = n // 2
    while j >= 1:
        for i in range(n):
            if (i ^ j) > i:
                _cmp_exchange(xs, i, i ^ j, True)
        j //= 2


def _sorted_top16(s):
    n = PEER_TOPK
    xs = [s[g * 8:(g + 1) * 8] for g in range(n)]
    k = 2
    while k <= n:
        j = k // 2
        while j >= 1:
            for i in range(n):
                if (i ^ j) > i:
                    _cmp_exchange(xs, i, i ^ j, (i & k) == 0)
            j //= 2
        k *= 2
    for shift in (4, 6, 7):
        other = [pltpu.roll(x, shift, 0) for x in xs]
        xs = [jnp.maximum(xs[i], other[n - 1 - i]) for i in range(n)]
        _bitonic_merge_desc(xs)
    return jnp.concatenate([x[0:1] for x in xs], axis=0)


def _count_greater(s, v):
    r = [v[a:a + 1] for a in range(PEER_TOPK)]
    c8 = r[7] > s
    c4 = jnp.where(c8, r[11], r[3]) > s
    c2 = jnp.where(c8, jnp.where(c4, r[13], r[9]), jnp.where(c4, r[5], r[1])) > s
    p1 = jnp.where(c8,
                   jnp.where(c4, jnp.where(c2, r[14], r[12]), jnp.where(c2, r[10], r[8])),
                   jnp.where(c4, jnp.where(c2, r[6], r[4]), jnp.where(c2, r[2], r[0])))
    c1 = p1 > s
    g = (jnp.where(c8, 8.0, 0.0) + jnp.where(c4, 4.0, 0.0)) + (jnp.where(c2, 2.0, 0.0) + jnp.where(c1, 1.0, 0.0))
    return jnp.where(r[15] > s, float(PEER_TOPK), g)


def _select_fast(s0, s1, okf):
    def one(s):
        v = _sorted_top16(s)
        rank = _count_greater(s, v)
        dup = jnp.max(jnp.where(v[0:15] == v[1:16], 1.0, 0.0), axis=0, keepdims=True)
        members = jnp.sum(jnp.where(s >= v[15:16], 1.0, 0.0), axis=0, keepdims=True)
        return v, rank, dup + jnp.where(members != float(PEER_TOPK), 1.0, 0.0)

    v0, rank0, tie0 = one(s0)
    v1, rank1, tie1 = one(s1)
    cand = _pair_candidates(v0, v1, okf)
    work = cand
    m = cand[0:1]
    for _ in range(PEER_TOPK):
        m = jnp.max(work, axis=0, keepdims=True)
        work = jnp.where(work == m, -jnp.inf, work)
    selm = jnp.where(cand >= m, 1.0, 0.0)
    tie2 = jnp.where(jnp.sum(selm, axis=0, keepdims=True) != float(PEER_TOPK), 1.0, 0.0)
    return _gate_arrays(s0, s1, rank0, rank1, v0, v1, cand, selm), tie0 + tie1 + tie2


def _peer_prep_body(x_ref, wq_ref, keys_ref, flat_ref, okf_ref, rank1_ref, bn_ref, crow_ref, av_ref, q_scr):
    x = x_ref[...]
    q_scr[...] = lax.dot_general(wq_ref[...], x, (((1,), (1,)), ((), ())), preferred_element_type=F32).astype(BF16)
    nk = N_KEYS
    tm = x.shape[0]
    flat = flat_ref[...]
    okf = okf_ref[...]

    def store(h, cs, outs):
        rank1, bn, crow, av = outs
        rank1_ref[h, :, cs] = rank1.astype(BF16)
        bn_ref[h, :, cs] = bn.astype(BF16)
        crow_ref[h, :, cs] = crow
        av_ref[h, :, cs] = av

    def head(h, carry):
        r0 = pl.multiple_of(h * 2 * nk, 2 * nk)
        for c0 in range(0, tm, LANES):
            cs = slice(c0, c0 + LANES)
            s0 = jnp.dot(keys_ref[h, 0], q_scr[pl.ds(r0, nk), cs], preferred_element_type=F32)
            s1 = jnp.dot(keys_ref[h, 1], q_scr[pl.ds(r0 + nk, nk), cs], preferred_element_type=F32)
            outs, tie = _select_fast(s0, s1, okf)
            store(h, cs, outs)

            @pl.when(jnp.max(tie) > 0.0)
            def _():
                store(h, cs, _select_exact(s0, s1, flat, okf))
        return carry

    lax.fori_loop(0, PEER_HEADS, head, 0)


def peer_prep(x_bf, wq_t, keys_bf):
    t, d = x_bf.shape
    tm = _tile(t, PEER_PREP_TOK)
    rows = _pair_rows()
    flat = jnp.asarray([[a * 16 + b] for a, b in rows], jnp.int32)
    okf = jnp.asarray([[1.0 if (a + 1) * (b + 1) <= PEER_TOPK else 0.0] for a, b in rows], F32)
    out = jax.ShapeDtypeStruct((PEER_HEADS, N_KEYS, t), F32)
    out_bf = jax.ShapeDtypeStruct((PEER_HEADS, N_KEYS, t), BF16)
    ospec = pl.BlockSpec((PEER_HEADS, N_KEYS, tm), lambda i: (0, 0, i))
    return pl.pallas_call(
        _peer_prep_body,
        out_shape=(out_bf, out_bf, out, out),
        grid=(t // tm,),
        in_specs=[pl.BlockSpec((tm, d), lambda i: (i, 0)),
                  pl.BlockSpec(wq_t.shape, lambda i: (0, 0)),
                  pl.BlockSpec(keys_bf.shape, lambda i: (0, 0, 0, 0)),
                  pl.BlockSpec(flat.shape, lambda i: (0, 0)),
                  pl.BlockSpec(okf.shape, lambda i: (0, 0))],
        out_specs=(ospec, ospec, ospec, ospec),
        scratch_shapes=[pltpu.VMEM((wq_t.shape[0], tm), BF16)],
        compiler_params=_cparams(("parallel",)),
        name="peer_prep",
    )(x_bf, wq_t, keys_bf, flat, okf)


def _row_tile_bf16(row):
    tile = jnp.broadcast_to(row, (BF16_SUBLANES, row.shape[1])).astype(BF16)
    return jnp.concatenate([tile] * (N_KEYS // BF16_SUBLANES), axis=0)


def _peer_main_body(x_ref, u_ref, v_ref, rank1_ref, bn_ref, crow_ref, av_ref, h_ref, g2_ref, fg_ref, o_ref,
                    acc_scr, pt_scr, *, final_norm):
    j = pl.program_id(1)
    nj = pl.num_programs(1) - 1
    tm = x_ref.shape[0]
    nrow = u_ref.shape[0] // N_KEYS
    sub = PEER_SUB_ROWS * N_KEYS
    nsb = nrow // PEER_SUB_ROWS
    x = x_ref[...]
    rd = (j + 1) % 2
    wr = j % 2
    row0 = jnp.minimum(j, nj - 1) * nrow

    @pl.when(j == 0)
    def _():
        acc_scr[...] = jnp.zeros(acc_scr.shape, F32)
        pt_scr[1] = jnp.zeros(pt_scr.shape[1:], BF16)

    def pre_act(sb):
        return lax.dot_general(u_ref[sb * sub:(sb + 1) * sub, :], x, (((1,), (1,)), ((), ())),
                               preferred_element_type=F32)

    pt_prev = pt_scr[rd]
    dn = v_ref.shape[1] // nsb
    acts, pvs = [], []
    for sb in range(nsb):
        acts.append(pre_act(sb))
        pvs.append(lax.dot_general(pt_prev, v_ref[:, sb * dn:(sb + 1) * dn], (((0,), (0,)), ((), ())),
                                   preferred_element_type=F32))
    for sb in range(nsb):
        act = acts[sb]
        for rr in range(PEER_SUB_ROWS):
            e1 = row0 + sb * PEER_SUB_ROWS + rr
            for c0 in range(0, tm, PEER_CW):
                cs = slice(c0, c0 + PEER_CW)
                w = None
                for hd in range(PEER_HEADS):
                    cr = _row_tile_bf16(crow_ref[hd, pl.ds(e1, 1), cs])
                    ar = _row_tile_bf16(av_ref[hd, pl.ds(e1, 1), cs])
                    t = jnp.where(rank1_ref[hd, :, cs] < cr, bn_ref[hd, :, cs], jnp.zeros((), BF16)) * ar
                    w = t if w is None else w + t
                a = act[rr * N_KEYS:(rr + 1) * N_KEYS, cs]
                gelu = 0.5 * a * (1.0 + lax.erf(a * (2.0 ** -0.5)))
                r0 = (sb * PEER_SUB_ROWS + rr) * N_KEYS
                pt_scr[wr, r0:r0 + N_KEYS, cs] = gelu.astype(BF16) * w
    for sb in range(nsb):
        acc_scr[:, sb * dn:(sb + 1) * dn] += pvs[sb]

    @pl.when(j == nj)
    def _():
        hn = h_ref[...] + g2_ref[0] * acc_scr[...]
        if final_norm:
            hn = hn * lax.rsqrt(jnp.mean(hn * hn, axis=-1, keepdims=True) + EPS) * fg_ref[...]
        o_ref[...] = hn


def peer_main(x_bf, u_bf, v_bf, prep, h2, g2, tokens_per_batch, final_g, final_norm):
    t, d = x_bf.shape
    e = u_bf.shape[0]
    tm = _tile(tokens_per_batch, PEER_TOK)
    te = PEER_ROWS * N_KEYS
    assert e % te == 0 and tm % PEER_CW == 0
    tpb = tokens_per_batch // tm
    nj = e // te
    assert PEER_ROWS % PEER_SUB_ROWS == 0 and PEER_ROWS // PEER_SUB_ROWS >= 2
    pspec = pl.BlockSpec((PEER_HEADS, N_KEYS, tm), lambda i, j: (0, 0, i))
    return pl.pallas_call(
        functools.partial(_peer_main_body, final_norm=final_norm),
        out_shape=jax.ShapeDtypeStruct((t, d), F32),
        grid=(t // tm, nj + 1),
        in_specs=[pl.BlockSpec((tm, d), lambda i, j: (i, 0)),
                  pl.BlockSpec((te, d), lambda i, j: (jnp.minimum(j, nj - 1), 0)),
                  pl.BlockSpec((te, d), lambda i, j: (jnp.maximum(j - 1, 0), 0)),
                  pspec, pspec, pspec, pspec,
                  pl.BlockSpec((tm, d), lambda i, j: (i, 0)),
                  pl.BlockSpec((1, 1, d), lambda i, j: (i // tpb, 0, 0)),
                  pl.BlockSpec((1, d), lambda i, j: (0, 0))],
        out_specs=pl.BlockSpec((tm, d), lambda i, j: (i, 0)),
        scratch_shapes=[pltpu.VMEM((tm, d), F32), pltpu.VMEM((2, te, tm), BF16)],
        compiler_params=_cparams(("parallel", "arbitrary")),
        name="peer_dense",
    )(x_bf, u_bf, v_bf, *prep, h2, g2, final_g)


def peer_residual(h, norm_g, sh, sc, g2, wq_t, keys_bf, u_bf, v_bf, final_g, final_norm):
    bx, lx, d = h.shape
    a = norm_mod(h, norm_g, sh, sc).reshape(bx * lx, d)
    prep = peer_prep(a, wq_t, keys_bf)
    out = peer_main(a, u_bf, v_bf, prep, h.reshape(bx * lx, d), g2, lx, final_g, final_norm)
    return out.reshape(bx, lx, d)


def kernel(x, c, ctx, c_ctx, ada_w, ada_b, norm1_g, norm2_g, final_g, ab_w_in, ab_w_out, hy_conv_w, hy_conv_b,
           hy_filt_w1, hy_filt_b1, hy_filt_w2, hy_filt_b2, hy_filt_w3, hy_filt_freq, hy_skip, attn_sink,
           hg_w_in, hg_w_out, hg_lb_logits, hg_onorm_g, peer_wq, peer_keys, peer_u, peer_v):
    bsz, seq, d = x.shape
    depth = ada_w.shape[0]
    assert depth == 2
    lb_p = jax.nn.softmax(hg_lb_logits.astype(F32), axis=0)
    lb_all = jnp.cumsum(lb_p, axis=0) - lb_p[0:1]

    c16 = jnp.concatenate([c, c_ctx[None], jnp.zeros((16 - bsz - 1, d), F32)], axis=0)
    final_row = final_g[None]
    h_lat, h_ctx = x, ctx

    for l in range(depth):
        need_ctx = l < depth - 1
        mod = ada_mod(c16, ada_w[l], ada_b[l][None])
        lat = [mod[:bsz, i * d:(i + 1) * d][:, None, :] for i in range(6)]
        cx = [jnp.broadcast_to(mod[bsz:bsz + 1, i * d:(i + 1) * d][:, None, :], (bsz, 1, d)) for i in range(6)]
        n1 = norm1_g[l][None]
        n2 = norm2_g[l][None]
        wq_t = peer_wq[l].T.astype(BF16)
        keys_bf = peer_keys[l].astype(BF16)
        u_bf = peer_u[l].astype(BF16)
        v_bf = peer_v[l].astype(BF16)
        j = l // 2
        if l % 2 == 0:
            w_in = ab_w_in[j].astype(BF16)
            w_out = ab_w_out[j].astype(BF16)
            p_lat = norm_mod_matmul(h_lat, n1, lat[0], lat[1], w_in)
            p_ctx = norm_mod_matmul(h_ctx, n1, cx[0], cx[1], w_in)
            hy = (hy_conv_w[j], hy_conv_b[j][None], hy_filt_w1[j], hy_filt_b1[j], hy_filt_w2[j], hy_filt_b2[j],
                  hy_filt_w3[j], hy_filt_freq[j], hy_skip[j])
            cos, sin = _rope_tables(seq)
            qr, kr = rope_qk(p_lat, cos, sin, 3, 16)
            sink = attn_sink[j]
            at_lat = window_attention(qr, kr, p_lat, p_ctx, sink, 16, 17)
            hy_lat = hyena_mixer(p_lat, *hy)
            h_lat = out_proj_residual(hy_lat, at_lat, w_out, h_lat, lat[2])
            if need_ctx:
                at_ctx = context_attention(p_ctx, sink, 3, 16, 17)
                hy_ctx = hyena_mixer(p_ctx, *hy)
                h_ctx = out_proj_residual(hy_ctx, at_ctx, w_out, h_ctx, cx[2])
        else:
            w_in = hg_w_in[j].astype(BF16)
            w_out = hg_w_out[j].astype(BF16)
            lb3 = lb_all[l].astype(F32)[:, None, :]
            p_lat = norm_mod_matmul(h_lat, n1, lat[0], lat[1], w_in)
            p_ctx = norm_mod_matmul(h_ctx, n1, cx[0], cx[1], w_in)
            s0 = jnp.zeros((bsz, HG_HEADS, HG_DK, HG_DK), F32)
            o_cf, s_cf = gla_scan(p_ctx, lb3, s0, 0, 0, 1, 3)
            o_cb, s_cb = gla_scan(p_ctx, lb3, s0, 1, 0, 2, 3)
            o_lf, _ = gla_scan(p_lat, lb3, s_cf, 0, 0, 1, 3)
            o_lb, _ = gla_scan(p_lat, lb3, s_cb, 1, 0, 2, 3)
            onorm = hg_onorm_g[j].astype(F32)[None]
            h_lat = hgrn_readout_residual(o_lf, o_lb, p_lat, 4, onorm, w_out, h_lat, lat[2])
            if need_ctx:
                h_ctx = hgrn_readout_residual(o_cf, o_cb, p_ctx, 4, onorm, w_out, h_ctx, cx[2])
        last = l == depth - 1
        h_lat = peer_residual(h_lat, n2, lat[3], lat[4], lat[5], wq_t, keys_bf, u_bf, v_bf, final_row, last)
        if need_ctx:
            h_ctx = peer_residual(h_ctx, n2, cx[3], cx[4], cx[5], wq_t, keys_bf, u_bf, v_bf, final_row, False)
    return h_lat
```

```python
import functools
import math

import jax
import jax.numpy as jnp
import numpy as np
from jax import lax
from jax.experimental import pallas as pl
from jax.experimental.pallas import tpu as pltpu

F32 = jnp.float32
BF16 = jnp.bfloat16
EPS = 1e-6

HY_C = 512
HY_EMB = 33
HY_EMB_PAD = 40
HY_FILT_W = 64
HY_DECAY_SHORT_PCT = 0.3
HY_DECAY_LONG_PCT = 1.5
HY_TARGET = 1e-2
HEAD_DIM = 64
N_Q_HEADS = 8
N_KV_HEADS = 2
GQA_GROUP = 4
WINDOW = 128
ATTN_BLK = 128
GRID_W = 64
ROPE_BASE = 10000.0
HG_HEADS = 8
HG_DK = 128
PEER_HEADS = 8
PEER_TOPK = 16
N_KEYS = 128

LANES = 128
BF16_SUBLANES = 16
VMEM_LIMIT_BYTES = 56 * 1024 * 1024
ROW_TILE = 512
GLA_CHUNK = 128
GLA_SEQ_BLOCK = 256
PEER_PREP_TOK = 256
PEER_PREP_CW = 256
PEER_TOK = 512
PEER_ROWS = 8
PEER_SUB_ROWS = 2
PEER_CW = 256
NEG_BIG = -1e30


def _cparams(sem):
    return pltpu.CompilerParams(dimension_semantics=sem, vmem_limit_bytes=VMEM_LIMIT_BYTES)


def _sigmoid(x):
    return 1.0 / (1.0 + jnp.exp(-x))


def _tile(n, t):
    t = min(n, t)
    assert n % t == 0, (n, t)
    return t


def _ada_body(c_ref, w_ref, b_ref, o_ref):
    c = c_ref[...]
    s = c * _sigmoid(c)
    o_ref[...] = jnp.dot(s.astype(BF16), w_ref[...].astype(BF16), preferred_element_type=F32) + b_ref[...]


def ada_mod(c16, w, b):
    d, n = w.shape
    tn = _tile(n, 1536)
    return pl.pallas_call(
        _ada_body,
        out_shape=jax.ShapeDtypeStruct((c16.shape[0], n), F32),
        grid=(n // tn,),
        in_specs=[pl.BlockSpec(c16.shape, lambda j: (0, 0)),
                  pl.BlockSpec((d, tn), lambda j: (0, j)),
                  pl.BlockSpec((1, tn), lambda j: (0, j))],
        out_specs=pl.BlockSpec((c16.shape[0], tn), lambda j: (0, j)),
        compiler_params=_cparams(("arbitrary",)),
        name="ada_mod",
    )(c16, w, b)


def _norm_mod(x, g, sh, sc):
    y = x * lax.rsqrt(jnp.mean(x * x, axis=-1, keepdims=True) + EPS)
    return (y * g) * (1.0 + sc) + sh


def _nm_matmul_body(h_ref, g_ref, sh_ref, sc_ref, w_ref, p_ref, a_scr):
    @pl.when(pl.program_id(2) == 0)
    def _():
        a_scr[...] = _norm_mod(h_ref[0], g_ref[...], sh_ref[0], sc_ref[0]).astype(BF16)

    p_ref[0] = jnp.dot(a_scr[...], w_ref[...], preferred_element_type=F32)


def norm_mod_matmul(h, g, sh, sc, w_bf):
    bx, lx, d = h.shape
    n = w_bf.shape[1]
    tm = _tile(lx, ROW_TILE)
    tn = _tile(n, 1280 if n % 1280 == 0 else (1152 if n % 1152 == 0 else 1024))
    return pl.pallas_call(
        _nm_matmul_body,
        out_shape=jax.ShapeDtypeStruct((bx, lx, n), F32),
        grid=(bx, lx // tm, n // tn),
        in_specs=[pl.BlockSpec((1, tm, d), lambda b, i, j: (b, i, 0)),
                  pl.BlockSpec((1, d), lambda b, i, j: (0, 0)),
                  pl.BlockSpec((1, 1, d), lambda b, i, j: (b, 0, 0)),
                  pl.BlockSpec((1, 1, d), lambda b, i, j: (b, 0, 0)),
                  pl.BlockSpec((d, tn), lambda b, i, j: (0, j))],
        out_specs=pl.BlockSpec((1, tm, tn), lambda b, i, j: (b, i, j)),
        scratch_shapes=[pltpu.VMEM((tm, d), BF16)],
        compiler_params=_cparams(("parallel", "parallel", "arbitrary")),
        name="norm_mod_matmul",
    )(h, g, sh, sc, w_bf)


def _nm_only_body(h_ref, g_ref, sh_ref, sc_ref, a_ref):
    a_ref[...] = _norm_mod(h_ref[0], g_ref[...], sh_ref[0], sc_ref[0]).T.astype(BF16)


def norm_mod_t(h, g, sh, sc):
    bx, lx, d = h.shape
    tm = _tile(lx, ROW_TILE)
    nt = lx // tm
    return pl.pallas_call(
        _nm_only_body,
        out_shape=jax.ShapeDtypeStruct((d, bx * lx), BF16),
        grid=(bx, nt),
        in_specs=[pl.BlockSpec((1, tm, d), lambda b, i: (b, i, 0)),
                  pl.BlockSpec((1, d), lambda b, i: (0, 0)),
                  pl.BlockSpec((1, 1, d), lambda b, i: (b, 0, 0)),
                  pl.BlockSpec((1, 1, d), lambda b, i: (b, 0, 0))],
        out_specs=pl.BlockSpec((d, tm), lambda b, i: (0, b * nt + i)),
        compiler_params=_cparams(("parallel", "parallel")),
        name="norm_mod_t",
    )(h, g, sh, sc)


def _mm_body(a_ref, b_ref, o_ref):
    o_ref[...] = jnp.dot(a_ref[...], b_ref[...], preferred_element_type=F32)


def matmul_bf16(a, b):
    m, k = a.shape
    n = b.shape[1]
    tm = _tile(m, 1024)
    tn = _tile(n, 1024)
    return pl.pallas_call(
        _mm_body,
        out_shape=jax.ShapeDtypeStruct((m, n), F32),
        grid=(m // tm, n // tn),
        in_specs=[pl.BlockSpec((tm, k), lambda i, j: (i, 0)),
                  pl.BlockSpec((k, tn), lambda i, j: (0, j))],
        out_specs=pl.BlockSpec((tm, tn), lambda i, j: (i, j)),
        compiler_params=_cparams(("parallel", "parallel")),
        name="matmul_bf16",
    )(a, b)


def _filter_body(z_ref, w1_ref, b1_ref, w2_ref, b2_ref, w3_ref, fr_ref, dec_ref, o_ref):
    hi = lax.Precision.HIGHEST
    fr = fr_ref[...]
    hdn = jnp.sin(fr * (jnp.dot(z_ref[...], w1_ref[...], precision=hi, preferred_element_type=F32) + b1_ref[...]))
    hdn = jnp.sin(fr * (jnp.dot(hdn, w2_ref[...], precision=hi, preferred_element_type=F32) + b2_ref[...]))
    h = jnp.dot(hdn, w3_ref[...], precision=hi, preferred_element_type=F32)
    dec = dec_ref[...]
    c = dec.shape[1]
    h0 = h[:, :c] * dec
    h1 = h[:, c:] * dec
    nrm = jnp.sum(jnp.abs(h0) + jnp.abs(h1), axis=0, keepdims=True)
    inv = 1.0 / nrm
    ri = lax.broadcasted_iota(jnp.int32, h1.shape, 0)
    o_ref[:, :c] = (h0 * inv).astype(BF16)
    o_ref[:, c:] = jnp.where(ri == 0, 0.0, h1 * inv).astype(BF16)


def hyena_filters(zfeat, w1p, b1, w2, b2, w3, freq, decay):
    l = zfeat.shape[0]
    c = decay.shape[1]
    n_order = w3.shape[1] // (2 * c)
    full = lambda shape: pl.BlockSpec(shape, lambda o: (0,) * len(shape))
    return pl.pallas_call(
        _filter_body,
        out_shape=jax.ShapeDtypeStruct((l, n_order * 2 * c), BF16),
        grid=(n_order,),
        in_specs=[full(zfeat.shape), full(w1p.shape), full(b1.shape), full(w2.shape), full(b2.shape),
                  pl.BlockSpec((w3.shape[0], 2 * c), lambda o: (0, o)),
                  full(freq.shape), full(decay.shape)],
        out_specs=pl.BlockSpec((l, 2 * c), lambda o: (0, o)),
        compiler_params=_cparams(("arbitrary",)),
        name="hyena_filters",
    )(zfeat, w1p, b1, w2, b2, w3, freq, decay)


def _short_conv_body(u_ref, w_ref, b_ref, o_ref):
    u = u_ref[0]
    l = u.shape[0]
    ri = lax.broadcasted_iota(jnp.int32, u.shape, 0)
    prev = jnp.where(ri == 0, 0.0, pltpu.roll(u, 1, 0))
    nxt = jnp.where(ri == l - 1, 0.0, pltpu.roll(u, l - 1, 0))
    w = w_ref[...]
    o_ref[0] = prev * w[0:1] + u * w[1:2] + nxt * w[2:3] + b_ref[...]


def short_conv(p, conv_w, conv_b):
    bx, lx, _ = p.shape
    c3 = conv_w.shape[1]
    tc = _tile(c3, 512)
    return pl.pallas_call(
        _short_conv_body,
        out_shape=jax.ShapeDtypeStruct((bx, lx, c3), F32),
        grid=(bx, c3 // tc),
        in_specs=[pl.BlockSpec((1, lx, tc), lambda b, j: (b, 0, j)),
                  pl.BlockSpec((3, tc), lambda b, j: (0, j)),
                  pl.BlockSpec((1, tc), lambda b, j: (0, j))],
        out_specs=pl.BlockSpec((1, lx, tc), lambda b, j: (b, 0, j)),
        compiler_params=_cparams(("parallel", "parallel")),
        name="hyena_short_conv",
    )(p, conv_w, conv_b)


def _hy_fwd_body(z_ref, f_ref, h0_ref, h1_ref, y_ref):
    z = z_ref[0].astype(BF16)
    kt = f_ref.shape[1]
    zre = jnp.dot(f_ref[0], z, preferred_element_type=F32)
    zim = jnp.dot(f_ref[1], z, preferred_element_type=F32)
    k0 = (pl.program_id(0) * kt + lax.broadcasted_iota(jnp.int32, zre.shape, 0)) == 0
    hre = h0_ref[0] + h1_ref[0]
    him = jnp.where(k0, h0_ref[1] + h1_ref[1], h0_ref[1] - h1_ref[1])
    yre = jnp.where(k0, zre * hre, zre * hre - zim * him)
    yim = jnp.where(k0, zim * him, zre * him + zim * hre)
    y_ref[0, 0] = yre.astype(BF16)
    y_ref[0, 1] = yim.astype(BF16)


def hyena_fwd(z_src, z_col, f3, fh3, order, c):
    bx, lx, _ = z_src.shape
    kt = _tile(lx, 512)
    return pl.pallas_call(
        _hy_fwd_body,
        out_shape=jax.ShapeDtypeStruct((bx, 2, lx, c), BF16),
        grid=(lx // kt, bx),
        in_specs=[pl.BlockSpec((1, lx, c), lambda k, b: (b, 0, z_col)),
                  pl.BlockSpec((2, kt, lx), lambda k, b: (0, k, 0)),
                  pl.BlockSpec((2, kt, c), lambda k, b: (0, k, 2 * order)),
                  pl.BlockSpec((2, kt, c), lambda k, b: (0, k, 2 * order + 1))],
        out_specs=pl.BlockSpec((1, 2, kt, c), lambda k, b: (b, 0, k, 0)),
        compiler_params=_cparams(("parallel", "parallel")),
        name="hyena_dft_fwd",
    )(z_src, f3, fh3, fh3)


def _hy_inv_body(y_ref, fi_ref, gate_ref, z_ref, skip_ref, o_ref):
    conv = jnp.dot(fi_ref[...], y_ref[0], preferred_element_type=F32)
    o_ref[0] = gate_ref[0] * (conv + skip_ref[...] * z_ref[0])


def hyena_inv(yf, finv, gate_src, gate_col, z_src, z_col, skip_row):
    bx, l2, c = yf.shape
    lx = l2 // 2
    tl = _tile(lx, 512)
    return pl.pallas_call(
        _hy_inv_body,
        out_shape=jax.ShapeDtypeStruct((bx, lx, c), F32),
        grid=(lx // tl, bx),
        in_specs=[pl.BlockSpec((1, l2, c), lambda t, b: (b, 0, 0)),
                  pl.BlockSpec((tl, l2), lambda t, b: (t, 0)),
                  pl.BlockSpec((1, tl, c), lambda t, b: (b, t, gate_col)),
                  pl.BlockSpec((1, tl, c), lambda t, b: (b, t, z_col)),
                  pl.BlockSpec((1, c), lambda t, b: (0, 0))],
        out_specs=pl.BlockSpec((1, tl, c), lambda t, b: (b, t, 0)),
        compiler_params=_cparams(("parallel", "parallel")),
        name="hyena_dft_inv",
    )(yf, finv, gate_src, z_src, skip_row)


def _dft_mats(l):
    n = 2 * l
    k = jnp.arange(l, dtype=jnp.int32)[:, None]
    s = jnp.arange(l, dtype=jnp.int32)[None, :]
    ang = ((k * s) % n).astype(F32) * (2.0 * math.pi / n)
    cosm = jnp.cos(ang)
    sinm = -jnp.sin(ang)
    nyq = jnp.where(s % 2 == 0, 1.0, -1.0).astype(F32)
    imag = jnp.where(k == 0, nyq, sinm)
    fwd = jnp.concatenate([cosm, imag], axis=0)
    scale = jnp.where(jnp.arange(l) == 0, 1.0 / n, 2.0 / n).astype(F32)
    inv = jnp.concatenate([cosm.T * scale[None, :], imag.T * scale[None, :]], axis=1)
    return fwd.astype(BF16), inv.astype(BF16)


def _filter_features(l):
    bands = (HY_EMB - 1) // 2
    t = jnp.linspace(0.0, 1.0, l, dtype=F32)[:, None]
    w = 2.0 * math.pi * jnp.arange(l, dtype=F32)[:, None] / l
    f = jnp.linspace(1e-4, bands - 1, bands, dtype=F32)[None, :]
    z = jnp.concatenate([t, jnp.cos(f * w), -jnp.sin(f * w)], axis=-1)
    z = jnp.pad(z, ((0, 0), (0, HY_EMB_PAD - HY_EMB)))
    max_decay = math.log(HY_TARGET) / HY_DECAY_SHORT_PCT
    min_decay = math.log(HY_TARGET) / HY_DECAY_LONG_PCT
    deltas = jnp.abs(jnp.linspace(min_decay, max_decay, HY_C, dtype=F32))
    return z, jnp.exp(-t * deltas)


def hyena_mixer(p, conv_w, conv_b, fw1, fb1, fw2, fb2, fw3, ffreq, d_skip):
    bx, lx, _ = p.shape
    c = HY_C
    zfeat, decay = _filter_features(lx)
    w1p = jnp.pad(fw1, ((0, HY_EMB_PAD - HY_EMB), (0, 0)))
    filt = hyena_filters(zfeat, w1p, fb1[None], fw2, fb2[None], fw3, ffreq[None], decay)
    fwd, inv = _dft_mats(lx)
    fh3 = matmul_bf16(fwd, filt).reshape(2, lx, filt.shape[1])
    f3 = fwd.reshape(2, lx, lx)
    uc = short_conv(p, conv_w, conv_b)
    y0 = hyena_fwd(uc, 0, f3, fh3, 0, c).reshape(bx, 2 * lx, c)
    z1 = hyena_inv(y0, inv, uc, 1, uc, 0, d_skip[0:1])
    y1 = hyena_fwd(z1, 0, f3, fh3, 1, c).reshape(bx, 2 * lx, c)
    return hyena_inv(y1, inv, uc, 2, z1, 0, d_skip[1:2])


def _rope_tables(l):
    rows = l // GRID_W
    r = jnp.broadcast_to(jnp.arange(rows)[:, None], (rows, GRID_W)).reshape(-1).astype(F32)
    col = jnp.broadcast_to(jnp.arange(GRID_W)[None, :], (rows, GRID_W)).reshape(-1).astype(F32)
    nf = HEAD_DIM // 4
    inv = ROPE_BASE ** (-jnp.arange(nf, dtype=F32) / nf)
    ar = r[:, None] * inv
    ac = col[:, None] * inv
    ang = jnp.concatenate([ar, ar, ac, ac], axis=-1)
    sign = jnp.concatenate([-jnp.ones((nf,)), jnp.ones((nf,))] * 2).astype(F32)
    cos = jnp.tile(jnp.cos(ang), (1, N_Q_HEADS))
    sin = jnp.tile(jnp.sin(ang) * sign[None, :], (1, N_Q_HEADS))
    return cos, sin


def _rope_body(q_ref, k_ref, cos_ref, sin_ref, qo_ref, ko_ref):
    def rot(x, cos, sin):
        n = x.shape[1]
        lane = lax.broadcasted_iota(jnp.int32, x.shape, 1)
        first = (lane % 32) < 16
        partner = jnp.where(first, pltpu.roll(x, n - 16, 1), pltpu.roll(x, 16, 1))
        return x * cos + partner * sin

    nk = k_ref.shape[2]
    qo_ref[0] = (rot(q_ref[0], cos_ref[...], sin_ref[...]) * (HEAD_DIM ** -0.5)).astype(BF16)
    ko_ref[0] = rot(k_ref[0], cos_ref[:, :nk], sin_ref[:, :nk]).astype(BF16)


def rope_qk(p, cos, sin, q_col, k_col):
    bx, lx, _ = p.shape
    nq = N_Q_HEADS * HEAD_DIM
    nk = N_KV_HEADS * HEAD_DIM
    tr = _tile(lx, ROW_TILE)
    return pl.pallas_call(
        _rope_body,
        out_shape=(jax.ShapeDtypeStruct((bx, lx, nq), BF16), jax.ShapeDtypeStruct((bx, lx, nk), BF16)),
        grid=(bx, lx // tr),
        in_specs=[pl.BlockSpec((1, tr, nq), lambda b, i: (b, i, q_col)),
                  pl.BlockSpec((1, tr, nk), lambda b, i: (b, i, k_col)),
                  pl.BlockSpec((tr, nq), lambda b, i: (i, 0)),
                  pl.BlockSpec((tr, nq), lambda b, i: (i, 0))],
        out_specs=(pl.BlockSpec((1, tr, nq), lambda b, i: (b, i, 0)),
                   pl.BlockSpec((1, tr, nk), lambda b, i: (b, i, 0))),
        compiler_params=_cparams(("parallel", "parallel")),
        name="rope_qk",
    )(p, p, cos, sin)


def _softmax_av(q4, sink_col, parts):
    ss = []
    m = sink_col
    for k, _, mask in parts:
        s = lax.dot_general(q4, k, (((1,), (1,)), ((), ())), preferred_element_type=F32)
        if mask is not None:
            s = jnp.where(mask, s, NEG_BIG)
        ss.append(s)
        m = jnp.maximum(m, jnp.max(s, axis=-1, keepdims=True))
    den = jnp.exp(sink_col - m)
    acc = None
    for s, (_, v, _) in zip(ss, parts):
        e = jnp.exp(s - m)
        den = den + jnp.sum(e, axis=-1, keepdims=True)
        o = jnp.dot(e.astype(BF16), v, preferred_element_type=F32)
        acc = o if acc is None else acc + o
    return acc / den


def _attn_heads(q, sink_ref, parts_for_head, o_ref):
    t = q.shape[0]
    outs = []
    for hk in range(N_KV_HEADS):
        q4 = jnp.concatenate(
            [q[:, (hk * GQA_GROUP + g) * HEAD_DIM:(hk * GQA_GROUP + g + 1) * HEAD_DIM] for g in range(GQA_GROUP)], axis=0)
        sink_col = jnp.concatenate(
            [jnp.full((t, 1), sink_ref[hk * GQA_GROUP + g], F32) for g in range(GQA_GROUP)], axis=0)
        o4 = _softmax_av(q4, sink_col, parts_for_head(hk))
        outs.extend([o4[g * t:(g + 1) * t] for g in range(GQA_GROUP)])
    o_ref[0] = jnp.concatenate(outs, axis=1)


def _win_attn_body(sink_ref, q_ref, k0_ref, k1_ref, k2_ref, v0_ref, v1_ref, v2_ref, kc_ref, vc_ref, o_ref, *, seq_len):
    n = pl.program_id(1)
    blk = q_ref.shape[1]
    kb = jnp.concatenate([k0_ref[0], k1_ref[0], k2_ref[0]], axis=0)
    vb = jnp.concatenate([v0_ref[0], v1_ref[0], v2_ref[0]], axis=0).astype(BF16)
    kc = kc_ref[0].astype(BF16)
    vc = vc_ref[0].astype(BF16)
    qi = lax.broadcasted_iota(jnp.int32, (blk, 3 * blk), 0)
    kj = lax.broadcasted_iota(jnp.int32, (blk, 3 * blk), 1)
    kpos = (n - 1) * blk + kj
    diff = qi + blk - kj
    valid = (kpos >= 0) & (kpos < seq_len) & (diff <= WINDOW) & (diff >= -WINDOW)
    valid4 = jnp.concatenate([valid] * GQA_GROUP, axis=0)

    def parts(hk):
        sl = slice(hk * HEAD_DIM, (hk + 1) * HEAD_DIM)
        return [(kb[:, sl], vb[:, sl], valid4), (kc[:, sl], vc[:, sl], None)]

    _attn_heads(q_ref[0], sink_ref, parts, o_ref)


def window_attention(qr, kr, p, p_ctx, sink, k_col, v_col):
    bx, lx, nq = qr.shape
    nk = kr.shape[2]
    lc = p_ctx.shape[1]
    blk = ATTN_BLK
    nb = lx // blk
    lo = lambda b, n: (b, jnp.maximum(n - 1, 0), 0)
    mid = lambda b, n: (b, n, 0)
    hi = lambda b, n: (b, jnp.minimum(n + 1, nb - 1), 0)
    vlo = lambda b, n: (b, jnp.maximum(n - 1, 0), v_col)
    vmid = lambda b, n: (b, n, v_col)
    vhi = lambda b, n: (b, jnp.minimum(n + 1, nb - 1), v_col)
    return pl.pallas_call(
        functools.partial(_win_attn_body, seq_len=lx),
        out_shape=jax.ShapeDtypeStruct((bx, lx, nq), F32),
        grid=(bx, nb),
        in_specs=[pl.BlockSpec(memory_space=pltpu.SMEM),
                  pl.BlockSpec((1, blk, nq), mid),
                  pl.BlockSpec((1, blk, nk), lo), pl.BlockSpec((1, blk, nk), mid), pl.BlockSpec((1, blk, nk), hi),
                  pl.BlockSpec((1, blk, nk), vlo), pl.BlockSpec((1, blk, nk), vmid), pl.BlockSpec((1, blk, nk), vhi),
                  pl.BlockSpec((1, lc, nk), lambda b, n: (b, 0, k_col)),
                  pl.BlockSpec((1, lc, nk), lambda b, n: (b, 0, v_col))],
        out_specs=pl.BlockSpec((1, blk, nq), mid),
        compiler_params=_cparams(("parallel", "parallel")),
        name="window_attention",
    )(sink, qr, kr, kr, kr, p, p, p, p_ctx, p_ctx)


def _ctx_attn_body(sink_ref, q_ref, kc_ref, vc_ref, o_ref):
    kc = kc_ref[0].astype(BF16)
    vc = vc_ref[0].astype(BF16)
    q = (q_ref[0] * (HEAD_DIM ** -0.5)).astype(BF16)

    def parts(hk):
        sl = slice(hk * HEAD_DIM, (hk + 1) * HEAD_DIM)
        return [(kc[:, sl], vc[:, sl], None)]

    _attn_heads(q, sink_ref, parts, o_ref)


def context_attention(p_ctx, sink, q_col, k_col, v_col):
    bx, lc, _ = p_ctx.shape
    nq = N_Q_HEADS * HEAD_DIM
    nk = N_KV_HEADS * HEAD_DIM
    return pl.pallas_call(
        _ctx_attn_body,
        out_shape=jax.ShapeDtypeStruct((bx, lc, nq), F32),
        grid=(bx,),
        in_specs=[pl.BlockSpec(memory_space=pltpu.SMEM),
                  pl.BlockSpec((1, lc, nq), lambda b: (b, 0, q_col)),
                  pl.BlockSpec((1, lc, nk), lambda b: (b, 0, k_col)),
                  pl.BlockSpec((1, lc, nk), lambda b: (b, 0, v_col))],
        out_specs=pl.BlockSpec((1, lc, nq), lambda b: (b, 0, 0)),
        compiler_params=_cparams(("parallel",)),
        name="context_attention",
    )(sink, p_ctx, p_ctx, p_ctx)


def _out_proj_body(ya_ref, yb_ref, wa_ref, wb_ref, h_ref, g_ref, o_ref):
    y = jnp.dot(ya_ref[0].astype(BF16), wa_ref[...], preferred_element_type=F32)
    y = y + jnp.dot(yb_ref[0].astype(BF16), wb_ref[...], preferred_element_type=F32)
    o_ref[0] = h_ref[0] + g_ref[0] * y


def out_proj_residual(ya, yb, w_bf, h, g1):
    bx, lx, d = h.shape
    ca = ya.shape[2]
    cb = yb.shape[2]
    assert ca == cb and w_bf.shape[0] == ca + cb
    tm = _tile(lx, ROW_TILE)
    return pl.pallas_call(
        _out_proj_body,
        out_shape=jax.ShapeDtypeStruct((bx, lx, d), F32),
        grid=(bx, lx // tm),
        in_specs=[pl.BlockSpec((1, tm, ca), lambda b, i: (b, i, 0)),
                  pl.BlockSpec((1, tm, cb), lambda b, i: (b, i, 0)),
                  pl.BlockSpec((ca, d), lambda b, i: (0, 0)),
                  pl.BlockSpec((cb, d), lambda b, i: (1, 0)),
                  pl.BlockSpec((1, tm, d), lambda b, i: (b, i, 0)),
                  pl.BlockSpec((1, 1, d), lambda b, i: (b, 0, 0))],
        out_specs=pl.BlockSpec((1, tm, d), lambda b, i: (b, i, 0)),
        compiler_params=_cparams(("parallel", "parallel")),
        name="out_proj_residual",
    )(ya, yb, w_bf, w_bf, h, g1)


def _gla_consts(c, reverse):
    t = np.arange(c)
    tau = (c - 1 - t) if reverse else t
    mats = [tau[None, :] <= tau[:, None]]
    ups, masks = [], []
    m = 1
    while m < c:
        blk = tau // (2 * m)
        ref = blk * 2 * m + m - 1
        upper = (tau % (2 * m)) >= m
        mats.append(tau[None, :] <= ref[:, None])
        ups.append(upper[:, None])
        masks.append((blk[:, None] == blk[None, :]) & upper[:, None] & (~upper)[None, :])
        m *= 2
    sgn = np.broadcast_to(2.0 * np.stack(ups).astype(np.float32) - 1.0, (len(ups), c, LANES))
    return jnp.asarray(mats[0], BF16), jnp.asarray(sgn, F32), jnp.asarray(np.stack(masks), F32)


def _ref_rows(b, m, reverse):
    c, w = b.shape
    off = m if reverse else m - 1
    if 2 * m >= 8:
        g = b.reshape(c // (2 * m), 2 * m, w)
        return jnp.broadcast_to(g[:, off:off + 1, :], g.shape).reshape(c, w)
    pos = lax.broadcasted_iota(jnp.int32, b.shape, 0) % (2 * m)
    out = b
    for p in range(2 * m):
        if p != off:
            out = jnp.where(pos == p, pltpu.roll(b, (p - off) % c, 0), out)
    return out


def _gla_body(q_ref, f_ref, v_ref, lb_ref, s0_ref, cum_ref, up_ref, mask_ref, o_ref, sout_ref,
              st_scr, qs_scr, kk_scr, lf_scr, *, reverse):
    s_idx = pl.program_id(1)
    ts = q_ref.shape[1]
    c = GLA_CHUNK
    nch = ts // c
    nlev = up_ref.shape[0]
    dk = HG_DK
    nt = (((1,), (1,)), ((), ()))
    tn = (((0,), (0,)), ((), ()))

    @pl.when(s_idx == 0)
    def _():
        st_scr[...] = s0_ref[0]

    q = q_ref[0]
    lb = lb_ref[0]
    fg = lb + (1.0 - lb) * _sigmoid(f_ref[0])
    qs_scr[...] = q * _sigmoid(q)
    kk_scr[...] = 1.0 - fg
    lf_scr[...] = jnp.log(fg)

    def chunk(i, carry):
        ci = (nch - 1 - i) if reverse else i
        r0 = pl.multiple_of(ci * c, c)
        heads = range(HG_HEADS)
        cols = [slice(hd * dk, (hd + 1) * dk) for hd in heads]
        qc = [qs_scr[pl.ds(r0, c), cs] for cs in cols]
        kc = [kk_scr[pl.ds(r0, c), cs] for cs in cols]
        vc = [v_ref[0, pl.ds(r0, c), cs] for cs in cols]
        b = []
        for cs in cols:
            lf = lf_scr[pl.ds(r0, c), cs]
            l1 = lf.astype(BF16)
            r1 = lf - l1.astype(F32)
            l2 = r1.astype(BF16)
            l3 = (r1 - l2.astype(F32)).astype(BF16)
            sums = jnp.dot(cum_ref[...], jnp.concatenate([l1, l2, l3], axis=1), preferred_element_type=F32)
            b.append(sums[:, :dk] + sums[:, dk:2 * dk] + sums[:, 2 * dk:])
        st = [st_scr[hd] for hd in heads]
        o = []
        for hd in heads:
            oi = lax.dot_general((qc[hd] * jnp.exp(b[hd])).astype(BF16), st[hd].astype(BF16), nt,
                                 preferred_element_type=F32)
            o.append(oi + jnp.sum(qc[hd] * kc[hd], axis=-1, keepdims=True) * vc[hd])
        a = [jnp.zeros((c, c), F32) for _ in heads]
        for li in range(nlev):
            for hd in heads:
                e = jnp.exp((b[hd] - _ref_rows(b[hd], 1 << li, reverse)) * up_ref[li])
                qd = (qc[hd] * e).astype(BF16)
                kd = (kc[hd] * e).astype(BF16)
                a[hd] = a[hd] + mask_ref[li] * lax.dot_general(qd, kd, nt, preferred_element_type=F32)
        for hd in heads:
            o_ref[0, pl.ds(r0, c), cols[hd]] = o[hd] + jnp.dot(a[hd].astype(BF16), vc[hd].astype(BF16),
                                                                preferred_element_type=F32)
            b_end = b[hd][0:1, :] if reverse else b[hd][c - 1:c, :]
            kdec = kc[hd] * jnp.exp(b_end - b[hd])
            upd = lax.dot_general(vc[hd].astype(BF16), kdec.astype(BF16), tn, preferred_element_type=F32)
            st_scr[hd] = st[hd] * jnp.exp(b_end) + upd
        return carry

    lax.fori_loop(0, nch, chunk, 0)

    @pl.when(s_idx == pl.num_programs(1) - 1)
    def _():
        sout_ref[0] = st_scr[...]


def gla_scan(p, lb3, s0, direction, q_col, f_col, v_col):
    bx, lx, _ = p.shape
    hh = HG_HEADS
    dk = HG_DK
    f = hh * dk
    ts = _tile(lx, GLA_SEQ_BLOCK)
    ns = lx // ts
    rev = direction == 1
    sblk = (lambda s: ns - 1 - s) if rev else (lambda s: s)
    cum, up, mask = _gla_consts(GLA_CHUNK, rev)
    const = lambda a: pl.BlockSpec(a.shape, lambda b, s: (0,) * a.ndim)
    return pl.pallas_call(
        functools.partial(_gla_body, reverse=rev),
        out_shape=(jax.ShapeDtypeStruct((bx, lx, f), F32), jax.ShapeDtypeStruct((bx, hh, dk, dk), F32)),
        grid=(bx, ns),
        in_specs=[pl.BlockSpec((1, ts, f), lambda b, s: (b, sblk(s), q_col)),
                  pl.BlockSpec((1, ts, f), lambda b, s: (b, sblk(s), f_col)),
                  pl.BlockSpec((1, ts, f), lambda b, s: (b, sblk(s), v_col)),
                  pl.BlockSpec((1, 1, f), lambda b, s: (direction, 0, 0)),
                  pl.BlockSpec((1, hh, dk, dk), lambda b, s: (b, 0, 0, 0)),
                  const(cum), const(up), const(mask)],
        out_specs=(pl.BlockSpec((1, ts, f), lambda b, s: (b, sblk(s), 0)),
                   pl.BlockSpec((1, hh, dk, dk), lambda b, s: (b, 0, 0, 0))),
        scratch_shapes=[pltpu.VMEM((hh, dk, dk), F32), pltpu.VMEM((ts, f), F32),
                        pltpu.VMEM((ts, f), F32), pltpu.VMEM((ts, f), F32)],
        compiler_params=_cparams(("parallel", "arbitrary")),
        name="gla_scan_bwd" if rev else "gla_scan_fwd",
    )(p, p, p, lb3, s0, cum, up, mask)


def _hg_readout_body(of_ref, ob_ref, g_ref, gn_ref, w_ref, h_ref, g1_ref, o_ref):
    o = of_ref[0] + ob_ref[0]
    gn = gn_ref[...]
    pieces = []
    for hh in range(HG_HEADS):
        oh = o[:, hh * HG_DK:(hh + 1) * HG_DK]
        y = oh * lax.rsqrt(jnp.mean(oh * oh, axis=-1, keepdims=True) + EPS)
        pieces.append(y * gn)
    on = jnp.concatenate(pieces, axis=1)
    g = g_ref[0]
    on = on * (g * _sigmoid(g))
    y = jnp.dot(on.astype(BF16), w_ref[...], preferred_element_type=F32)
    o_ref[0] = h_ref[0] + g1_ref[0] * y


def hgrn_readout_residual(o_f, o_b, p, g_col, onorm_row, w_bf, h, g1):
    bx, lx, d = h.shape
    f = o_f.shape[2]
    tm = _tile(lx, ROW_TILE)
    return pl.pallas_call(
        _hg_readout_body,
        out_shape=jax.ShapeDtypeStruct((bx, lx, d), F32),
        grid=(bx, lx // tm),
        in_specs=[pl.BlockSpec((1, tm, f), lambda b, i: (b, i, 0)),
                  pl.BlockSpec((1, tm, f), lambda b, i: (b, i, 0)),
                  pl.BlockSpec((1, tm, f), lambda b, i: (b, i, g_col)),
                  pl.BlockSpec((1, HG_DK), lambda b, i: (0, 0)),
                  pl.BlockSpec((f, d), lambda b, i: (0, 0)),
                  pl.BlockSpec((1, tm, d), lambda b, i: (b, i, 0)),
                  pl.BlockSpec((1, 1, d), lambda b, i: (b, 0, 0))],
        out_specs=pl.BlockSpec((1, tm, d), lambda b, i: (b, i, 0)),
        compiler_params=_cparams(("parallel", "parallel")),
        name="hgrn_readout_residual",
    )(o_f, o_b, p, onorm_row, w_bf, h, g1)


def _pair_rows():
    rows = [(0, b) for b in range(16)] + [(1, b) for b in range(8)]
    for a in range(2, 8):
        rows += [(a, b) for b in range(8)]
    rows += [(a, 0) for a in range(8, 16)]
    return rows


def _top16_rows(s):
    n = s.shape[0]
    ri = lax.broadcasted_iota(jnp.int32, s.shape, 0)
    rank = jnp.full(s.shape, float(PEER_TOPK), F32)
    vals = []
    for a in range(PEER_TOPK):
        m = jnp.max(s, axis=0, keepdims=True)
        first = jnp.min(jnp.where(s == m, ri, n), axis=0, keepdims=True)
        sel = ri == first
        rank = jnp.where(sel, float(a), rank)
        s = jnp.where(sel, -jnp.inf, s)
        vals.append(m)
    return rank, jnp.concatenate(vals, axis=0)


def _pair_candidates(v0, v1, okf):
    blocks = [v0[0:1] + v1, v0[1:2] + v1[0:8]]
    blocks += [v0[a:a + 1] + v1[0:8] for a in range(2, 8)]
    blocks += [v0[8:16] + v1[0:1]]
    return jnp.where(okf > 0.0, jnp.concatenate(blocks, axis=0), -jnp.inf)


def _gate_arrays(s0, s1, rank0, rank1, v0, v1, cand, selm):
    top = cand[0:1]
    z = jnp.sum(selm * jnp.exp(jnp.where(selm > 0.0, cand - top, 0.0)), axis=0, keepdims=True)
    cnt = [jnp.sum(selm[0:16], axis=0, keepdims=True)]
    cnt += [jnp.sum(selm[16 + 8 * (a - 1):24 + 8 * (a - 1)], axis=0, keepdims=True) for a in range(1, 8)]
    cnt8 = selm[72:80]
    crow = jnp.zeros(rank0.shape, F32)
    for a in range(8):
        crow = jnp.where(rank0 == float(a), cnt[a], crow)
    for a in range(8, 16):
        crow = jnp.where(rank0 == float(a), cnt8[a - 8:a - 7], crow)
    av = jnp.where(rank0 < float(PEER_TOPK), jnp.exp(s0 - v0[0:1]), 0.0)
    bn = jnp.where(rank1 < float(PEER_TOPK), jnp.exp(s1 - v1[0:1]), 0.0) / z
    return rank1, bn, crow, av


def _select_exact(s0, s1, flat, okf):
    rank0, v0 = _top16_rows(s0)
    rank1, v1 = _top16_rows(s1)
    cand = _pair_candidates(v0, v1, okf)
    work = cand
    selm = jnp.zeros(cand.shape, F32)
    for _ in range(PEER_TOPK):
        m = jnp.max(work, axis=0, keepdims=True)
        first = jnp.min(jnp.where(work == m, flat, 1 << 20), axis=0, keepdims=True)
        sel = flat == first
        selm = jnp.where(sel, 1.0, selm)
        work = jnp.where(sel, -jnp.inf, work)
    return _gate_arrays(s0, s1, rank0, rank1, v0, v1, cand, selm)


def _cmp_exchange(xs, i, l, larger_first):
    hi = jnp.maximum(xs[i], xs[l])
    lo = jnp.minimum(xs[i], xs[l])
    xs[i], xs[l] = (hi, lo) if larger_first else (lo, hi)


def _bitonic_merge_desc(xs):
    n = len(xs)
    j = n // 2
    while j >= 1:
        for i in range(n):
            if (i ^ j) > i:
                _cmp_exchange(xs, i, i ^ j, True)
        j //= 2


def _sorted_top16(s):
    n = PEER_TOPK
    xs = [s[g * 8:(g + 1) * 8] for g in range(n)]
    k = 2
    while k <= n:
        j = k // 2
        while j >= 1:
            for i in range(n):
                if (i ^ j) > i:
                    _cmp_exchange(xs, i, i ^ j, (i & k) == 0)
            j //= 2
        k *= 2
    for shift in (4, 6, 7):
        other = [pltpu.roll(x, shift, 0) for x in xs]
        xs = [jnp.maximum(xs[i], other[n - 1 - i]) for i in range(n)]
        _bitonic_merge_desc(xs)
    return jnp.concatenate([x[0:1] for x in xs], axis=0)


def _count_greater(s, v):
    r = [v[a:a + 1] for a in range(PEER_TOPK)]
    c8 = r[7] > s
    c4 = jnp.where(c8, r[11], r[3]) > s
    c2 = jnp.where(c8, jnp.where(c4, r[13], r[9]), jnp.where(c4, r[5], r[1])) > s
    p1 = jnp.where(c8,
                   jnp.where(c4, jnp.where(c2, r[14], r[12]), jnp.where(c2, r[10], r[8])),
                   jnp.where(c4, jnp.where(c2, r[6], r[4]), jnp.where(c2, r[2], r[0])))
    c1 = p1 > s
    g = (jnp.where(c8, 8.0, 0.0) + jnp.where(c4, 4.0, 0.0)) + (jnp.where(c2, 2.0, 0.0) + jnp.where(c1, 1.0, 0.0))
    return jnp.where(r[15] > s, float(PEER_TOPK), g)


def _select_fast(s0, s1, okf):
    w = s0.shape[1]
    s = jnp.concatenate([s0, s1], axis=1)
    v = _sorted_top16(s)
    rank = _count_greater(s, v)
    dup = jnp.max(jnp.where(v[0:15] == v[1:16], 1.0, 0.0), axis=0, keepdims=True)
    members = jnp.sum(jnp.where(s >= v[15:16], 1.0, 0.0), axis=0, keepdims=True)
    tie = dup + jnp.where(members != float(PEER_TOPK), 1.0, 0.0)
    v0, rank0, tie0 = v[:, :w], rank[:, :w], tie[:, :w]
    v1, rank1, tie1 = v[:, w:], rank[:, w:], tie[:, w:]
    cand = _pair_candidates(v0, v1, okf)
    work = cand
    m = cand[0:1]
    for _ in range(PEER_TOPK):
        m = jnp.max(work, axis=0, keepdims=True)
        work = jnp.where(work == m, -jnp.inf, work)
    selm = jnp.where(cand >= m, 1.0, 0.0)
    tie2 = jnp.where(jnp.sum(selm, axis=0, keepdims=True) != float(PEER_TOPK), 1.0, 0.0)
    return _gate_arrays(s0, s1, rank0, rank1, v0, v1, cand, selm), tie0 + tie1 + tie2


def _peer_prep_body(x_ref, wq_ref, keys_ref, flat_ref, okf_ref, rank1_ref, bn_ref, crow_ref, av_ref, q_scr):
    q_scr[...] = jnp.dot(wq_ref[...], x_ref[...], preferred_element_type=F32).astype(BF16)
    nk = N_KEYS
    tm = x_ref.shape[1]
    flat = flat_ref[...]
    okf = okf_ref[...]

    def store(h, cs, outs):
        rank1, bn, crow, av = outs
        rank1_ref[h, :, cs] = rank1.astype(BF16)
        bn_ref[h, :, cs] = bn.astype(BF16)
        crow_ref[h, :, cs] = crow
        av_ref[h, :, cs] = av

    def head(h, carry):
        r0 = pl.multiple_of(h * 2 * nk, 2 * nk)
        for c0 in range(0, tm, PEER_PREP_CW):
            cs = slice(c0, c0 + PEER_PREP_CW)
            s0 = jnp.dot(keys_ref[h, 0], q_scr[pl.ds(r0, nk), cs], preferred_element_type=F32)
            s1 = jnp.dot(keys_ref[h, 1], q_scr[pl.ds(r0 + nk, nk), cs], preferred_element_type=F32)
            outs, tie = _select_fast(s0, s1, okf)
            store(h, cs, outs)

            @pl.when(jnp.max(tie) > 0.0)
            def _():
                store(h, cs, _select_exact(s0, s1, flat, okf))
        return carry

    lax.fori_loop(0, PEER_HEADS, head, 0)


def peer_prep(x_t, wq_t, keys_bf):
    d, t = x_t.shape
    tm = _tile(t, PEER_PREP_TOK)
    rows = _pair_rows()
    flat = jnp.asarray([[a * 16 + b] for a, b in rows], jnp.int32)
    okf = jnp.asarray([[1.0 if (a + 1) * (b + 1) <= PEER_TOPK else 0.0] for a, b in rows], F32)
    out = jax.ShapeDtypeStruct((PEER_HEADS, N_KEYS, t), F32)
    out_bf = jax.ShapeDtypeStruct((PEER_HEADS, N_KEYS, t), BF16)
    ospec = pl.BlockSpec((PEER_HEADS, N_KEYS, tm), lambda i: (0, 0, i))
    return pl.pallas_call(
        _peer_prep_body,
        out_shape=(out_bf, out_bf, out, out),
        grid=(t // tm,),
        in_specs=[pl.BlockSpec((d, tm), lambda i: (0, i)),
                  pl.BlockSpec(wq_t.shape, lambda i: (0, 0)),
                  pl.BlockSpec(keys_bf.shape, lambda i: (0, 0, 0, 0)),
                  pl.BlockSpec(flat.shape, lambda i: (0, 0)),
                  pl.BlockSpec(okf.shape, lambda i: (0, 0))],
        out_specs=(ospec, ospec, ospec, ospec),
        scratch_shapes=[pltpu.VMEM((wq_t.shape[0], tm), BF16)],
        compiler_params=_cparams(("parallel",)),
        name="peer_prep",
    )(x_t, wq_t, keys_bf, flat, okf)


def _row_tile_bf16(row):
    tile = jnp.broadcast_to(row, (BF16_SUBLANES, row.shape[1])).astype(BF16)
    return jnp.concatenate([tile] * (N_KEYS // BF16_SUBLANES), axis=0)


def _peer_main_body(x_ref, u_ref, v_ref, rank1_ref, bn_ref, crow_ref, av_ref, h_ref, g2_ref, fg_ref, o_ref,
                    acc_scr, pt_scr, *, final_norm):
    j = pl.program_id(1)
    nj = pl.num_programs(1) - 1
    tm = x_ref.shape[1]
    nrow = u_ref.shape[0] // N_KEYS
    sub = PEER_SUB_ROWS * N_KEYS
    nsb = nrow // PEER_SUB_ROWS
    x = x_ref[...]
    rd = (j + 1) % 2
    wr = j % 2
    row0 = jnp.minimum(j, nj - 1) * nrow

    @pl.when(j == 0)
    def _():
        acc_scr[...] = jnp.zeros(acc_scr.shape, F32)
        pt_scr[1] = jnp.zeros(pt_scr.shape[1:], BF16)

    def pre_act(sb):
        return jnp.dot(u_ref[sb * sub:(sb + 1) * sub, :], x, preferred_element_type=F32)

    pt_prev = pt_scr[rd]
    dn = v_ref.shape[1] // nsb
    acts, pvs = [], []
    for sb in range(nsb):
        acts.append(pre_act(sb))
        pvs.append(lax.dot_general(pt_prev, v_ref[:, sb * dn:(sb + 1) * dn], (((0,), (0,)), ((), ())),
                                   preferred_element_type=F32))
    for sb in range(nsb):
        act = acts[sb]
        for rr in range(PEER_SUB_ROWS):
            e1 = row0 + sb * PEER_SUB_ROWS + rr
            for c0 in range(0, tm, PEER_CW):
                cs = slice(c0, c0 + PEER_CW)
                w = None
                for hd in range(PEER_HEADS):
                    cr = _row_tile_bf16(crow_ref[hd, pl.ds(e1, 1), cs])
                    ar = _row_tile_bf16(av_ref[hd, pl.ds(e1, 1), cs])
                    t = jnp.where(rank1_ref[hd, :, cs] < cr, bn_ref[hd, :, cs], jnp.zeros((), BF16)) * ar
                    w = t if w is None else w + t
                a = act[rr * N_KEYS:(rr + 1) * N_KEYS, cs]
                gelu = 0.5 * a * (1.0 + lax.erf(a * (2.0 ** -0.5)))
                r0 = (sb * PEER_SUB_ROWS + rr) * N_KEYS
                pt_scr[wr, r0:r0 + N_KEYS, cs] = gelu.astype(BF16) * w
    for sb in range(nsb):
        acc_scr[:, sb * dn:(sb + 1) * dn] += pvs[sb]

    @pl.when(j == nj)
    def _():
        hn = h_ref[...] + g2_ref[0] * acc_scr[...]
        if final_norm:
            hn = hn * lax.rsqrt(jnp.mean(hn * hn, axis=-1, keepdims=True) + EPS) * fg_ref[...]
        o_ref[...] = hn


def peer_main(x_t, u_bf, v_bf, prep, h2, g2, tokens_per_batch, final_g, final_norm):
    d, t = x_t.shape
    e = u_bf.shape[0]
    tm = _tile(tokens_per_batch, PEER_TOK)
    te = PEER_ROWS * N_KEYS
    assert e % te == 0 and tm % PEER_CW == 0
    tpb = tokens_per_batch // tm
    nj = e // te
    assert PEER_ROWS % PEER_SUB_ROWS == 0 and PEER_ROWS // PEER_SUB_ROWS >= 2
    pspec = pl.BlockSpec((PEER_HEADS, N_KEYS, tm), lambda i, j: (0, 0, i))
    return pl.pallas_call(
        functools.partial(_peer_main_body, final_norm=final_norm),
        out_shape=jax.ShapeDtypeStruct((t, d), F32),
        grid=(t // tm, nj + 1),
        in_specs=[pl.BlockSpec((d, tm), lambda i, j: (0, i)),
                  pl.BlockSpec((te, d), lambda i, j: (jnp.minimum(j, nj - 1), 0)),
                  pl.BlockSpec((te, d), lambda i, j: (jnp.maximum(j - 1, 0), 0)),
                  pspec, pspec, pspec, pspec,
                  pl.BlockSpec((tm, d), lambda i, j: (i, 0)),
                  pl.BlockSpec((1, 1, d), lambda i, j: (i // tpb, 0, 0)),
                  pl.BlockSpec((1, d), lambda i, j: (0, 0))],
        out_specs=pl.BlockSpec((tm, d), lambda i, j: (i, 0)),
        scratch_shapes=[pltpu.VMEM((tm, d), F32), pltpu.VMEM((2, te, tm), BF16)],
        compiler_params=_cparams(("parallel", "arbitrary")),
        name="peer_dense",
    )(x_t, u_bf, v_bf, *prep, h2, g2, final_g)


def peer_residual(h, norm_g, sh, sc, g2, wq_t, keys_bf, u_bf, v_bf, final_g, final_norm):
    bx, lx, d = h.shape
    a_t = norm_mod_t(h, norm_g, sh, sc)
    prep = peer_prep(a_t, wq_t, keys_bf)
    out = peer_main(a_t, u_bf, v_bf, prep, h.reshape(bx * lx, d), g2, lx, final_g, final_norm)
    return out.reshape(bx, lx, d)


def kernel(x, c, ctx, c_ctx, ada_w, ada_b, norm1_g, norm2_g, final_g, ab_w_in, ab_w_out, hy_conv_w, hy_conv_b,
           hy_filt_w1, hy_filt_b1, hy_filt_w2, hy_filt_b2, hy_filt_w3, hy_filt_freq, hy_skip, attn_sink,
           hg_w_in, hg_w_out, hg_lb_logits, hg_onorm_g, peer_wq, peer_keys, peer_u, peer_v):
    bsz, seq, d = x.shape
    depth = ada_w.shape[0]
    assert depth == 2
    lb_p = jax.nn.softmax(hg_lb_logits.astype(F32), axis=0)
    lb_all = jnp.cumsum(lb_p, axis=0) - lb_p[0:1]

    c16 = jnp.concatenate([c, c_ctx[None], jnp.zeros((16 - bsz - 1, d), F32)], axis=0)
    final_row = final_g[None]
    h_lat, h_ctx = x, ctx

    for l in range(depth):
        need_ctx = l < depth - 1
        mod = ada_mod(c16, ada_w[l], ada_b[l][None])
        lat = [mod[:bsz, i * d:(i + 1) * d][:, None, :] for i in range(6)]
        cx = [jnp.broadcast_to(mod[bsz:bsz + 1, i * d:(i + 1) * d][:, None, :], (bsz, 1, d)) for i in range(6)]
        n1 = norm1_g[l][None]
        n2 = norm2_g[l][None]
        wq_t = peer_wq[l].T.astype(BF16)
        keys_bf = peer_keys[l].astype(BF16)
        u_bf = peer_u[l].astype(BF16)
        v_bf = peer_v[l].astype(BF16)
        j = l // 2
        if l % 2 == 0:
            w_in = ab_w_in[j].astype(BF16)
            w_out = ab_w_out[j].astype(BF16)
            p_lat = norm_mod_matmul(h_lat, n1, lat[0], lat[1], w_in)
            p_ctx = norm_mod_matmul(h_ctx, n1, cx[0], cx[1], w_in)
            hy = (hy_conv_w[j], hy_conv_b[j][None], hy_filt_w1[j], hy_filt_b1[j], hy_filt_w2[j], hy_filt_b2[j],
                  hy_filt_w3[j], hy_filt_freq[j], hy_skip[j])
            cos, sin = _rope_tables(seq)
            qr, kr = rope_qk(p_lat, cos, sin, 3, 16)
            sink = attn_sink[j]
            at_lat = window_attention(qr, kr, p_lat, p_ctx, sink, 16, 17)
            hy_lat = hyena_mixer(p_lat, *hy)
            h_lat = out_proj_residual(hy_lat, at_lat, w_out, h_lat, lat[2])
            if need_ctx:
                at_ctx = context_attention(p_ctx, sink, 3, 16, 17)
                hy_ctx = hyena_mixer(p_ctx, *hy)
                h_ctx = out_proj_residual(hy_ctx, at_ctx, w_out, h_ctx, cx[2])
        else:
            w_in = hg_w_in[j].astype(BF16)
            w_out = hg_w_out[j].astype(BF16)
            lb3 = lb_all[l].astype(F32)[:, None, :]
            p_lat = norm_mod_matmul(h_lat, n1, lat[0], lat[1], w_in)
            p_ctx = norm_mod_matmul(h_ctx, n1, cx[0], cx[1], w_in)
            s0 = jnp.zeros((bsz, HG_HEADS, HG_DK, HG_DK), F32)
            o_cf, s_cf = gla_scan(p_ctx, lb3, s0, 0, 0, 1, 3)
            o_cb, s_cb = gla_scan(p_ctx, lb3, s0, 1, 0, 2, 3)
            o_lf, _ = gla_scan(p_lat, lb3, s_cf, 0, 0, 1, 3)
            o_lb, _ = gla_scan(p_lat, lb3, s_cb, 1, 0, 2, 3)
            onorm = hg_onorm_g[j].astype(F32)[None]
            h_lat = hgrn_readout_residual(o_lf, o_lb, p_lat, 4, onorm, w_out, h_lat, lat[2])
            if need_ctx:
                h_ctx = hgrn_readout_residual(o_cf, o_cb, p_ctx, 4, onorm, w_out, h_ctx, cx[2])
        last = l == depth - 1
        h_lat = peer_residual(h_lat, n2, lat[3], lat[4], lat[5], wq_t, keys_bf, u_bf, v_bf, final_row, last)
        if need_ctx:
            h_ctx = peer_residual(h_ctx, n2, cx[3], cx[4], cx[5], wq_t, keys_bf, u_bf, v_bf, final_row, False)
    return h_lat
```

```python
import functools
import math

import jax
import jax.numpy as jnp
import numpy as np
from jax import lax
from jax.experimental import pallas as pl
from jax.experimental.pallas import tpu as pltpu

F32 = jnp.float32
BF16 = jnp.bfloat16
EPS = 1e-6

HY_C = 512
HY_EMB = 33
HY_EMB_PAD = 40
HY_FILT_W = 64
HY_DECAY_SHORT_PCT = 0.3
HY_DECAY_LONG_PCT = 1.5
HY_TARGET = 1e-2
HEAD_DIM = 64
N_Q_HEADS = 8
N_KV_HEADS = 2
GQA_GROUP = 4
WINDOW = 128
ATTN_BLK = 128
GRID_W = 64
ROPE_BASE = 10000.0
HG_HEADS = 8
HG_DK = 128
PEER_HEADS = 8
PEER_TOPK = 16
N_KEYS = 128

LANES = 128
BF16_SUBLANES = 16
VMEM_LIMIT_BYTES = 56 * 1024 * 1024
ROW_TILE = 512
GLA_CHUNK = 128
GLA_SEQ_BLOCK = 256
PEER_PREP_TOK = 256
PEER_PREP_CW = 256
PEER_TOK = 512
PEER_ROWS = 8
PEER_SUB_ROWS = 2
PEER_CW = 256
NEG_BIG = -1e30


def _cparams(sem):
    return pltpu.CompilerParams(dimension_semantics=sem, vmem_limit_bytes=VMEM_LIMIT_BYTES)


def _sigmoid(x):
    return 1.0 / (1.0 + jnp.exp(-x))


def _tile(n, t):
    t = min(n, t)
    assert n % t == 0, (n, t)
    return t


def _ada_body(c_ref, w_ref, b_ref, o_ref):
    c = c_ref[...]
    s = c * _sigmoid(c)
    o_ref[...] = jnp.dot(s.astype(BF16), w_ref[...].astype(BF16), preferred_element_type=F32) + b_ref[...]


def ada_mod(c16, w_all, b_all, layer):
    _, d, n = w_all.shape
    tn = _tile(n, 1536)
    return pl.pallas_call(
        _ada_body,
        out_shape=jax.ShapeDtypeStruct((c16.shape[0], n), F32),
        grid=(n // tn,),
        in_specs=[pl.BlockSpec(c16.shape, lambda j: (0, 0)),
                  pl.BlockSpec((None, d, tn), lambda j: (layer, 0, j)),
                  pl.BlockSpec((None, 1, tn), lambda j: (layer, 0, j))],
        out_specs=pl.BlockSpec((c16.shape[0], tn), lambda j: (0, j)),
        compiler_params=_cparams(("arbitrary",)),
        name="ada_mod",
    )(c16, w_all, b_all)


def _norm_mod(x, g, sh, sc):
    y = x * lax.rsqrt(jnp.mean(x * x, axis=-1, keepdims=True) + EPS)
    return (y * g) * (1.0 + sc) + sh


def _nm_matmul_body(h_ref, g_ref, sh_ref, sc_ref, w_ref, p_ref, a_scr):
    @pl.when(pl.program_id(2) == 0)
    def _():
        a_scr[...] = _norm_mod(h_ref[0], g_ref[...], sh_ref[0], sc_ref[0]).astype(BF16)

    p_ref[0] = jnp.dot(a_scr[...], w_ref[...], preferred_element_type=F32)


def norm_mod_matmul(h, g, sh, sc, w_bf):
    bx, lx, d = h.shape
    n = w_bf.shape[1]
    tm = _tile(lx, ROW_TILE)
    tn = _tile(n, 1280 if n % 1280 == 0 else (1152 if n % 1152 == 0 else 1024))
    return pl.pallas_call(
        _nm_matmul_body,
        out_shape=jax.ShapeDtypeStruct((bx, lx, n), F32),
        grid=(bx, lx // tm, n // tn),
        in_specs=[pl.BlockSpec((1, tm, d), lambda b, i, j: (b, i, 0)),
                  pl.BlockSpec((1, d), lambda b, i, j: (0, 0)),
                  pl.BlockSpec((1, 1, d), lambda b, i, j: (b, 0, 0)),
                  pl.BlockSpec((1, 1, d), lambda b, i, j: (b, 0, 0)),
                  pl.BlockSpec((d, tn), lambda b, i, j: (0, j))],
        out_specs=pl.BlockSpec((1, tm, tn), lambda b, i, j: (b, i, j)),
        scratch_shapes=[pltpu.VMEM((tm, d), BF16)],
        compiler_params=_cparams(("parallel", "parallel", "arbitrary")),
        name="norm_mod_matmul",
    )(h, g, sh, sc, w_bf)


def _nm_only_body(h_ref, g_ref, sh_ref, sc_ref, a_ref):
    a_ref[...] = _norm_mod(h_ref[0], g_ref[...], sh_ref[0], sc_ref[0]).T.astype(BF16)


def norm_mod_t(h, g, sh, sc):
    bx, lx, d = h.shape
    tm = _tile(lx, ROW_TILE)
    nt = lx // tm
    return pl.pallas_call(
        _nm_only_body,
        out_shape=jax.ShapeDtypeStruct((d, bx * lx), BF16),
        grid=(bx, nt),
        in_specs=[pl.BlockSpec((1, tm, d), lambda b, i: (b, i, 0)),
                  pl.BlockSpec((1, d), lambda b, i: (0, 0)),
                  pl.BlockSpec((1, 1, d), lambda b, i: (b, 0, 0)),
                  pl.BlockSpec((1, 1, d), lambda b, i: (b, 0, 0))],
        out_specs=pl.BlockSpec((d, tm), lambda b, i: (0, b * nt + i)),
        compiler_params=_cparams(("parallel", "parallel")),
        name="norm_mod_t",
    )(h, g, sh, sc)


def _mm_body(a_ref, b_ref, o_ref):
    o_ref[...] = jnp.dot(a_ref[...], b_ref[...], preferred_element_type=F32)


def matmul_bf16(a, b):
    m, k = a.shape
    n = b.shape[1]
    tm = _tile(m, 1024)
    tn = _tile(n, 1024)
    return pl.pallas_call(
        _mm_body,
        out_shape=jax.ShapeDtypeStruct((m, n), F32),
        grid=(m // tm, n // tn),
        in_specs=[pl.BlockSpec((tm, k), lambda i, j: (i, 0)),
                  pl.BlockSpec((k, tn), lambda i, j: (0, j))],
        out_specs=pl.BlockSpec((tm, tn), lambda i, j: (i, j)),
        compiler_params=_cparams(("parallel", "parallel")),
        name="matmul_bf16",
    )(a, b)


def _filter_body(z_ref, w1_ref, b1_ref, w2_ref, b2_ref, w3_ref, fr_ref, dec_ref, o_ref):
    hi = lax.Precision.HIGHEST
    fr = fr_ref[...]
    hdn = jnp.sin(fr * (jnp.dot(z_ref[...], w1_ref[...], precision=hi, preferred_element_type=F32) + b1_ref[...]))
    hdn = jnp.sin(fr * (jnp.dot(hdn, w2_ref[...], precision=hi, preferred_element_type=F32) + b2_ref[...]))
    h = jnp.dot(hdn, w3_ref[...], precision=hi, preferred_element_type=F32)
    dec = dec_ref[...]
    c = dec.shape[1]
    h0 = h[:, :c] * dec
    h1 = h[:, c:] * dec
    nrm = jnp.sum(jnp.abs(h0) + jnp.abs(h1), axis=0, keepdims=True)
    inv = 1.0 / nrm
    ri = lax.broadcasted_iota(jnp.int32, h1.shape, 0)
    o_ref[:, :c] = (h0 * inv).astype(BF16)
    o_ref[:, c:] = jnp.where(ri == 0, 0.0, h1 * inv).astype(BF16)


def hyena_filters(zfeat, w1p, b1, w2, b2, w3, freq, decay):
    l = zfeat.shape[0]
    c = decay.shape[1]
    n_order = w3.shape[1] // (2 * c)
    full = lambda shape: pl.BlockSpec(shape, lambda o: (0,) * len(shape))
    return pl.pallas_call(
        _filter_body,
        out_shape=jax.ShapeDtypeStruct((l, n_order * 2 * c), BF16),
        grid=(n_order,),
        in_specs=[full(zfeat.shape), full(w1p.shape), full(b1.shape), full(w2.shape), full(b2.shape),
                  pl.BlockSpec((w3.shape[0], 2 * c), lambda o: (0, o)),
                  full(freq.shape), full(decay.shape)],
        out_specs=pl.BlockSpec((l, 2 * c), lambda o: (0, o)),
        compiler_params=_cparams(("arbitrary",)),
        name="hyena_filters",
    )(zfeat, w1p, b1, w2, b2, w3, freq, decay)


def _short_conv_body(u_ref, w_ref, b_ref, o_ref):
    u = u_ref[0]
    l = u.shape[0]
    ri = lax.broadcasted_iota(jnp.int32, u.shape, 0)
    prev = jnp.where(ri == 0, 0.0, pltpu.roll(u, 1, 0))
    nxt = jnp.where(ri == l - 1, 0.0, pltpu.roll(u, l - 1, 0))
    w = w_ref[...]
    o_ref[0] = prev * w[0:1] + u * w[1:2] + nxt * w[2:3] + b_ref[...]


def short_conv(p, conv_w, conv_b):
    bx, lx, _ = p.shape
    c3 = conv_w.shape[1]
    tc = _tile(c3, 512)
    return pl.pallas_call(
        _short_conv_body,
        out_shape=jax.ShapeDtypeStruct((bx, lx, c3), F32),
        grid=(bx, c3 // tc),
        in_specs=[pl.BlockSpec((1, lx, tc), lambda b, j: (b, 0, j)),
                  pl.BlockSpec((3, tc), lambda b, j: (0, j)),
                  pl.BlockSpec((1, tc), lambda b, j: (0, j))],
        out_specs=pl.BlockSpec((1, lx, tc), lambda b, j: (b, 0, j)),
        compiler_params=_cparams(("parallel", "parallel")),
        name="hyena_short_conv",
    )(p, conv_w, conv_b)


def _hy_fwd_body(z_ref, f_ref, h0_ref, h1_ref, y_ref):
    z = z_ref[0].astype(BF16)
    kt = f_ref.shape[1]
    zre = jnp.dot(f_ref[0], z, preferred_element_type=F32)
    zim = jnp.dot(f_ref[1], z, preferred_element_type=F32)
    k0 = (pl.program_id(0) * kt + lax.broadcasted_iota(jnp.int32, zre.shape, 0)) == 0
    hre = h0_ref[0] + h1_ref[0]
    him = jnp.where(k0, h0_ref[1] + h1_ref[1], h0_ref[1] - h1_ref[1])
    yre = jnp.where(k0, zre * hre, zre * hre - zim * him)
    yim = jnp.where(k0, zim * him, zre * him + zim * hre)
    y_ref[0, 0] = yre.astype(BF16)
    y_ref[0, 1] = yim.astype(BF16)


def hyena_fwd(z_src, z_col, f3, fh3, order, c):
    bx, lx, _ = z_src.shape
    kt = _tile(lx, 512)
    return pl.pallas_call(
        _hy_fwd_body,
        out_shape=jax.ShapeDtypeStruct((bx, 2, lx, c), BF16),
        grid=(lx // kt, bx),
        in_specs=[pl.BlockSpec((1, lx, c), lambda k, b: (b, 0, z_col)),
                  pl.BlockSpec((2, kt, lx), lambda k, b: (0, k, 0)),
                  pl.BlockSpec((2, kt, c), lambda k, b: (0, k, 2 * order)),
                  pl.BlockSpec((2, kt, c), lambda k, b: (0, k, 2 * order + 1))],
        out_specs=pl.BlockSpec((1, 2, kt, c), lambda k, b: (b, 0, k, 0)),
        compiler_params=_cparams(("parallel", "parallel")),
        name="hyena_dft_fwd",
    )(z_src, f3, fh3, fh3)


def _hy_inv_body(y_ref, fi_ref, gate_ref, z_ref, skip_ref, o_ref):
    conv = jnp.dot(fi_ref[...], y_ref[0], preferred_element_type=F32)
    o_ref[0] = gate_ref[0] * (conv + skip_ref[...] * z_ref[0])


def hyena_inv(yf, finv, gate_src, gate_col, z_src, z_col, skip_row):
    bx, l2, c = yf.shape
    lx = l2 // 2
    tl = _tile(lx, 512)
    return pl.pallas_call(
        _hy_inv_body,
        out_shape=jax.ShapeDtypeStruct((bx, lx, c), F32),
        grid=(lx // tl, bx),
        in_specs=[pl.BlockSpec((1, l2, c), lambda t, b: (b, 0, 0)),
                  pl.BlockSpec((tl, l2), lambda t, b: (t, 0)),
                  pl.BlockSpec((1, tl, c), lambda t, b: (b, t, gate_col)),
                  pl.BlockSpec((1, tl, c), lambda t, b: (b, t, z_col)),
                  pl.BlockSpec((1, c), lambda t, b: (0, 0))],
        out_specs=pl.BlockSpec((1, tl, c), lambda t, b: (b, t, 0)),
        compiler_params=_cparams(("parallel", "parallel")),
        name="hyena_dft_inv",
    )(yf, finv, gate_src, z_src, skip_row)


def _dft_mats(l):
    n = 2 * l
    k = jnp.arange(l, dtype=jnp.int32)[:, None]
    s = jnp.arange(l, dtype=jnp.int32)[None, :]

    def table(rows):
        ang = ((rows[:, None] * s) % n).astype(F32) * (2.0 * math.pi / n)
        return jnp.cos(ang), jnp.sin(ang)

    r = 1
    while r * r < l:
        r *= 2
    ch, sh = table(jnp.arange(l // r, dtype=jnp.int32) * r)
    cl, sl = table(jnp.arange(r, dtype=jnp.int32))
    cosm = (ch[:, None, :] * cl[None, :, :] - sh[:, None, :] * sl[None, :, :]).reshape(l, l)
    sinm = -(sh[:, None, :] * cl[None, :, :] + ch[:, None, :] * sl[None, :, :]).reshape(l, l)
    nyq = jnp.where(s % 2 == 0, 1.0, -1.0).astype(F32)
    imag = jnp.where(k == 0, nyq, sinm)
    fwd = jnp.concatenate([cosm, imag], axis=0)
    scale = jnp.where(jnp.arange(l) == 0, 1.0 / n, 2.0 / n).astype(F32)
    inv = jnp.concatenate([cosm.T * scale[None, :], imag.T * scale[None, :]], axis=1)
    return fwd.astype(BF16), inv.astype(BF16)


def _filter_features(l):
    bands = (HY_EMB - 1) // 2
    t = jnp.linspace(0.0, 1.0, l, dtype=F32)[:, None]
    w = 2.0 * math.pi * jnp.arange(l, dtype=F32)[:, None] / l
    f = jnp.linspace(1e-4, bands - 1, bands, dtype=F32)[None, :]
    z = jnp.concatenate([t, jnp.cos(f * w), -jnp.sin(f * w)], axis=-1)
    z = jnp.pad(z, ((0, 0), (0, HY_EMB_PAD - HY_EMB)))
    max_decay = math.log(HY_TARGET) / HY_DECAY_SHORT_PCT
    min_decay = math.log(HY_TARGET) / HY_DECAY_LONG_PCT
    deltas = jnp.abs(jnp.linspace(min_decay, max_decay, HY_C, dtype=F32))
    return z, jnp.exp(-t * deltas)


def hyena_mixer(p, conv_w, conv_b, fw1, fb1, fw2, fb2, fw3, ffreq, d_skip):
    bx, lx, _ = p.shape
    c = HY_C
    zfeat, decay = _filter_features(lx)
    w1p = jnp.pad(fw1, ((0, HY_EMB_PAD - HY_EMB), (0, 0)))
    filt = hyena_filters(zfeat, w1p, fb1[None], fw2, fb2[None], fw3, ffreq[None], decay)
    fwd, inv = _dft_mats(lx)
    fh3 = matmul_bf16(fwd, filt).reshape(2, lx, filt.shape[1])
    f3 = fwd.reshape(2, lx, lx)
    uc = short_conv(p, conv_w, conv_b)
    y0 = hyena_fwd(uc, 0, f3, fh3, 0, c).reshape(bx, 2 * lx, c)
    z1 = hyena_inv(y0, inv, uc, 1, uc, 0, d_skip[0:1])
    y1 = hyena_fwd(z1, 0, f3, fh3, 1, c).reshape(bx, 2 * lx, c)
    return hyena_inv(y1, inv, uc, 2, z1, 0, d_skip[1:2])


def _rope_tables(l):
    rows = l // GRID_W
    r = jnp.broadcast_to(jnp.arange(rows)[:, None], (rows, GRID_W)).reshape(-1).astype(F32)
    col = jnp.broadcast_to(jnp.arange(GRID_W)[None, :], (rows, GRID_W)).reshape(-1).astype(F32)
    nf = HEAD_DIM // 4
    inv = ROPE_BASE ** (-jnp.arange(nf, dtype=F32) / nf)
    ar = r[:, None] * inv
    ac = col[:, None] * inv
    ang = jnp.concatenate([ar, ar, ac, ac], axis=-1)
    sign = jnp.concatenate([-jnp.ones((nf,)), jnp.ones((nf,))] * 2).astype(F32)
    cos = jnp.tile(jnp.cos(ang), (1, N_Q_HEADS))
    sin = jnp.tile(jnp.sin(ang) * sign[None, :], (1, N_Q_HEADS))
    return cos, sin


def _rope_body(q_ref, k_ref, cos_ref, sin_ref, qo_ref, ko_ref):
    def rot(x, cos, sin):
        n = x.shape[1]
        lane = lax.broadcasted_iota(jnp.int32, x.shape, 1)
        first = (lane % 32) < 16
        partner = jnp.where(first, pltpu.roll(x, n - 16, 1), pltpu.roll(x, 16, 1))
        return x * cos + partner * sin

    nk = k_ref.shape[2]
    qo_ref[0] = (rot(q_ref[0], cos_ref[...], sin_ref[...]) * (HEAD_DIM ** -0.5)).astype(BF16)
    ko_ref[0] = rot(k_ref[0], cos_ref[:, :nk], sin_ref[:, :nk]).astype(BF16)


def rope_qk(p, cos, sin, q_col, k_col):
    bx, lx, _ = p.shape
    nq = N_Q_HEADS * HEAD_DIM
    nk = N_KV_HEADS * HEAD_DIM
    tr = _tile(lx, ROW_TILE)
    return pl.pallas_call(
        _rope_body,
        out_shape=(jax.ShapeDtypeStruct((bx, lx, nq), BF16), jax.ShapeDtypeStruct((bx, lx, nk), BF16)),
        grid=(bx, lx // tr),
        in_specs=[pl.BlockSpec((1, tr, nq), lambda b, i: (b, i, q_col)),
                  pl.BlockSpec((1, tr, nk), lambda b, i: (b, i, k_col)),
                  pl.BlockSpec((tr, nq), lambda b, i: (i, 0)),
                  pl.BlockSpec((tr, nq), lambda b, i: (i, 0))],
        out_specs=(pl.BlockSpec((1, tr, nq), lambda b, i: (b, i, 0)),
                   pl.BlockSpec((1, tr, nk), lambda b, i: (b, i, 0))),
        compiler_params=_cparams(("parallel", "parallel")),
        name="rope_qk",
    )(p, p, cos, sin)


def _attn_heads(q, sink_ref, parts_for_head, o_ref):
    t = q.shape[0]
    nt = (((1,), (1,)), ((), ()))
    heads = range(N_KV_HEADS)
    q4, sink, parts = [], [], []
    for hk in heads:
        q4.append(jnp.concatenate(
            [q[:, (hk * GQA_GROUP + g) * HEAD_DIM:(hk * GQA_GROUP + g + 1) * HEAD_DIM] for g in range(GQA_GROUP)],
            axis=0))
        sink.append(jnp.concatenate(
            [jnp.full((t, 1), sink_ref[hk * GQA_GROUP + g], F32) for g in range(GQA_GROUP)], axis=0))
        parts.append(parts_for_head(hk))
    ss = []
    for hk in heads:
        row = []
        for k, _, mask in parts[hk]:
            sc = lax.dot_general(q4[hk], k, nt, preferred_element_type=F32)
            row.append(sc if mask is None else jnp.where(mask, sc, NEG_BIG))
        ss.append(row)
    m = []
    for hk in heads:
        mh = sink[hk]
        for sc in ss[hk]:
            mh = jnp.maximum(mh, jnp.max(sc, axis=-1, keepdims=True))
        m.append(mh)
    es = [[jnp.exp(sc - m[hk]) for sc in ss[hk]] for hk in heads]
    den = []
    for hk in heads:
        dh = jnp.exp(sink[hk] - m[hk])
        for e in es[hk]:
            dh = dh + jnp.sum(e, axis=-1, keepdims=True)
        den.append(dh)
    outs = []
    for hk in heads:
        acc = None
        for e, (_, v, _) in zip(es[hk], parts[hk]):
            o = jnp.dot(e.astype(BF16), v, preferred_element_type=F32)
            acc = o if acc is None else acc + o
        o4 = acc / den[hk]
        outs.extend([o4[g * t:(g + 1) * t] for g in range(GQA_GROUP)])
    o_ref[0] = jnp.concatenate(outs, axis=1)


def _win_attn_body(sink_ref, q_ref, k0_ref, k1_ref, k2_ref, v0_ref, v1_ref, v2_ref, kc_ref, vc_ref, o_ref, *, seq_len):
    n = pl.program_id(1)
    blk = q_ref.shape[1]
    kb = jnp.concatenate([k0_ref[0], k1_ref[0], k2_ref[0]], axis=0)
    vb = jnp.concatenate([v0_ref[0], v1_ref[0], v2_ref[0]], axis=0).astype(BF16)
    kc = kc_ref[0].astype(BF16)
    vc = vc_ref[0].astype(BF16)
    qi = lax.broadcasted_iota(jnp.int32, (blk, 3 * blk), 0)
    kj = lax.broadcasted_iota(jnp.int32, (blk, 3 * blk), 1)
    kpos = (n - 1) * blk + kj
    diff = qi + blk - kj
    valid = (kpos >= 0) & (kpos < seq_len) & (diff <= WINDOW) & (diff >= -WINDOW)
    valid4 = jnp.concatenate([valid] * GQA_GROUP, axis=0)

    def parts(hk):
        sl = slice(hk * HEAD_DIM, (hk + 1) * HEAD_DIM)
        return [(kb[:, sl], vb[:, sl], valid4), (kc[:, sl], vc[:, sl], None)]

    _attn_heads(q_ref[0], sink_ref, parts, o_ref)


def window_attention(qr, kr, p, p_ctx, sink, k_col, v_col):
    bx, lx, nq = qr.shape
    nk = kr.shape[2]
    lc = p_ctx.shape[1]
    blk = ATTN_BLK
    nb = lx // blk
    lo = lambda b, n: (b, jnp.maximum(n - 1, 0), 0)
    mid = lambda b, n: (b, n, 0)
    hi = lambda b, n: (b, jnp.minimum(n + 1, nb - 1), 0)
    vlo = lambda b, n: (b, jnp.maximum(n - 1, 0), v_col)
    vmid = lambda b, n: (b, n, v_col)
    vhi = lambda b, n: (b, jnp.minimum(n + 1, nb - 1), v_col)
    return pl.pallas_call(
        functools.partial(_win_attn_body, seq_len=lx),
        out_shape=jax.ShapeDtypeStruct((bx, lx, nq), F32),
        grid=(bx, nb),
        in_specs=[pl.BlockSpec(memory_space=pltpu.SMEM),
                  pl.BlockSpec((1, blk, nq), mid),
                  pl.BlockSpec((1, blk, nk), lo), pl.BlockSpec((1, blk, nk), mid), pl.BlockSpec((1, blk, nk), hi),
                  pl.BlockSpec((1, blk, nk), vlo), pl.BlockSpec((1, blk, nk), vmid), pl.BlockSpec((1, blk, nk), vhi),
                  pl.BlockSpec((1, lc, nk), lambda b, n: (b, 0, k_col)),
                  pl.BlockSpec((1, lc, nk), lambda b, n: (b, 0, v_col))],
        out_specs=pl.BlockSpec((1, blk, nq), mid),
        compiler_params=_cparams(("parallel", "parallel")),
        name="window_attention",
    )(sink, qr, kr, kr, kr, p, p, p, p_ctx, p_ctx)


def _ctx_attn_body(sink_ref, q_ref, kc_ref, vc_ref, o_ref):
    kc = kc_ref[0].astype(BF16)
    vc = vc_ref[0].astype(BF16)
    q = (q_ref[0] * (HEAD_DIM ** -0.5)).astype(BF16)

    def parts(hk):
        sl = slice(hk * HEAD_DIM, (hk + 1) * HEAD_DIM)
        return [(kc[:, sl], vc[:, sl], None)]

    _attn_heads(q, sink_ref, parts, o_ref)


def context_attention(p_ctx, sink, q_col, k_col, v_col):
    bx, lc, _ = p_ctx.shape
    nq = N_Q_HEADS * HEAD_DIM
    nk = N_KV_HEADS * HEAD_DIM
    return pl.pallas_call(
        _ctx_attn_body,
        out_shape=jax.ShapeDtypeStruct((bx, lc, nq), F32),
        grid=(bx,),
        in_specs=[pl.BlockSpec(memory_space=pltpu.SMEM),
                  pl.BlockSpec((1, lc, nq), lambda b: (b, 0, q_col)),
                  pl.BlockSpec((1, lc, nk), lambda b: (b, 0, k_col)),
                  pl.BlockSpec((1, lc, nk), lambda b: (b, 0, v_col))],
        out_specs=pl.BlockSpec((1, lc, nq), lambda b: (b, 0, 0)),
        compiler_params=_cparams(("parallel",)),
        name="context_attention",
    )(sink, p_ctx, p_ctx, p_ctx)


def _out_proj_body(ya_ref, yb_ref, wa_ref, wb_ref, h_ref, g_ref, o_ref):
    y = jnp.dot(ya_ref[0].astype(BF16), wa_ref[...], preferred_element_type=F32)
    y = y + jnp.dot(yb_ref[0].astype(BF16), wb_ref[...], preferred_element_type=F32)
    o_ref[0] = h_ref[0] + g_ref[0] * y


def out_proj_residual(ya, yb, w_bf, h, g1):
    bx, lx, d = h.shape
    ca = ya.shape[2]
    cb = yb.shape[2]
    assert ca == cb and w_bf.shape[0] == ca + cb
    tm = _tile(lx, ROW_TILE)
    return pl.pallas_call(
        _out_proj_body,
        out_shape=jax.ShapeDtypeStruct((bx, lx, d), F32),
        grid=(bx, lx // tm),
        in_specs=[pl.BlockSpec((1, tm, ca), lambda b, i: (b, i, 0)),
                  pl.BlockSpec((1, tm, cb), lambda b, i: (b, i, 0)),
                  pl.BlockSpec((ca, d), lambda b, i: (0, 0)),
                  pl.BlockSpec((cb, d), lambda b, i: (1, 0)),
                  pl.BlockSpec((1, tm, d), lambda b, i: (b, i, 0)),
                  pl.BlockSpec((1, 1, d), lambda b, i: (b, 0, 0))],
        out_specs=pl.BlockSpec((1, tm, d), lambda b, i: (b, i, 0)),
        compiler_params=_cparams(("parallel", "parallel")),
        name="out_proj_residual",
    )(ya, yb, w_bf, w_bf, h, g1)


def _gla_consts(c, reverse):
    t = np.arange(c)
    tau = (c - 1 - t) if reverse else t
    mats = [tau[None, :] <= tau[:, None]]
    ups, masks = [], []
    m = 1
    while m < c:
        blk = tau // (2 * m)
        ref = blk * 2 * m + m - 1
        upper = (tau % (2 * m)) >= m
        mats.append(tau[None, :] <= ref[:, None])
        ups.append(upper[:, None])
        masks.append((blk[:, None] == blk[None, :]) & upper[:, None] & (~upper)[None, :])
        m *= 2
    sgn = np.broadcast_to(2.0 * np.stack(ups).astype(np.float32) - 1.0, (len(ups), c, LANES))
    return jnp.asarray(mats[0], BF16), jnp.asarray(sgn, F32), jnp.asarray(np.stack(masks), F32)


def _ref_rows(b, m, reverse):
    c, w = b.shape
    off = m if reverse else m - 1
    if 2 * m >= 8:
        g = b.reshape(c // (2 * m), 2 * m, w)
        return jnp.broadcast_to(g[:, off:off + 1, :], g.shape).reshape(c, w)
    pos = lax.broadcasted_iota(jnp.int32, b.shape, 0) % (2 * m)
    out = b
    for p in range(2 * m):
        if p != off:
            out = jnp.where(pos == p, pltpu.roll(b, (p - off) % c, 0), out)
    return out


def _gla_body(q_ref, f_ref, v_ref, lb_ref, s0_ref, cum_ref, up_ref, mask_ref, o_ref, sout_ref,
              st_scr, qs_scr, kk_scr, lf_scr, *, reverse):
    s_idx = pl.program_id(1)
    ts = q_ref.shape[1]
    c = GLA_CHUNK
    nch = ts // c
    nlev = up_ref.shape[0]
    dk = HG_DK
    nt = (((1,), (1,)), ((), ()))
    tn = (((0,), (0,)), ((), ()))

    @pl.when(s_idx == 0)
    def _():
        st_scr[...] = s0_ref[0]

    q = q_ref[0]
    lb = lb_ref[0]
    fg = lb + (1.0 - lb) * _sigmoid(f_ref[0])
    qs_scr[...] = q * _sigmoid(q)
    kk_scr[...] = 1.0 - fg
    lf_scr[...] = jnp.log(fg)

    def chunk(i, carry):
        ci = (nch - 1 - i) if reverse else i
        r0 = pl.multiple_of(ci * c, c)
        heads = range(HG_HEADS)
        cols = [slice(hd * dk, (hd + 1) * dk) for hd in heads]
        qc = [qs_scr[pl.ds(r0, c), cs] for cs in cols]
        kc = [kk_scr[pl.ds(r0, c), cs] for cs in cols]
        vc = [v_ref[0, pl.ds(r0, c), cs] for cs in cols]
        b = []
        for cs in cols:
            lf = lf_scr[pl.ds(r0, c), cs]
            l1 = lf.astype(BF16)
            r1 = lf - l1.astype(F32)
            l2 = r1.astype(BF16)
            l3 = (r1 - l2.astype(F32)).astype(BF16)
            sums = jnp.dot(cum_ref[...], jnp.concatenate([l1, l2, l3], axis=1), preferred_element_type=F32)
            b.append(sums[:, :dk] + sums[:, dk:2 * dk] + sums[:, 2 * dk:])
        st = [st_scr[hd] for hd in heads]
        o = []
        for hd in heads:
            oi = lax.dot_general((qc[hd] * jnp.exp(b[hd])).astype(BF16), st[hd].astype(BF16), nt,
                                 preferred_element_type=F32)
            o.append(oi + jnp.sum(qc[hd] * kc[hd], axis=-1, keepdims=True) * vc[hd])
        a = [jnp.zeros((c, c), F32) for _ in heads]
        for li in range(nlev):
            for hd in heads:
                e = jnp.exp((b[hd] - _ref_rows(b[hd], 1 << li, reverse)) * up_ref[li])
                qd = (qc[hd] * e).astype(BF16)
                kd = (kc[hd] * e).astype(BF16)
                a[hd] = a[hd] + mask_ref[li] * lax.dot_general(qd, kd, nt, preferred_element_type=F32)
        for hd in heads:
            o_ref[0, pl.ds(r0, c), cols[hd]] = o[hd] + jnp.dot(a[hd].astype(BF16), vc[hd].astype(BF16),
                                                                preferred_element_type=F32)
            b_end = b[hd][0:1, :] if reverse else b[hd][c - 1:c, :]
            kdec = kc[hd] * jnp.exp(b_end - b[hd])
            upd = lax.dot_general(vc[hd].astype(BF16), kdec.astype(BF16), tn, preferred_element_type=F32)
            st_scr[hd] = st[hd] * jnp.exp(b_end) + upd
        return carry

    lax.fori_loop(0, nch, chunk, 0)

    @pl.when(s_idx == pl.num_programs(1) - 1)
    def _():
        sout_ref[0] = st_scr[...]


def gla_scan(p, lb3, s0, direction, q_col, f_col, v_col):
    bx, lx, _ = p.shape
    hh = HG_HEADS
    dk = HG_DK
    f = hh * dk
    ts = _tile(lx, GLA_SEQ_BLOCK)
    ns = lx // ts
    rev = direction == 1
    sblk = (lambda s: ns - 1 - s) if rev else (lambda s: s)
    cum, up, mask = _gla_consts(GLA_CHUNK, rev)
    const = lambda a: pl.BlockSpec(a.shape, lambda b, s: (0,) * a.ndim)
    return pl.pallas_call(
        functools.partial(_gla_body, reverse=rev),
        out_shape=(jax.ShapeDtypeStruct((bx, lx, f), F32), jax.ShapeDtypeStruct((bx, hh, dk, dk), F32)),
        grid=(bx, ns),
        in_specs=[pl.BlockSpec((1, ts, f), lambda b, s: (b, sblk(s), q_col)),
                  pl.BlockSpec((1, ts, f), lambda b, s: (b, sblk(s), f_col)),
                  pl.BlockSpec((1, ts, f), lambda b, s: (b, sblk(s), v_col)),
                  pl.BlockSpec((1, 1, f), lambda b, s: (direction, 0, 0)),
                  pl.BlockSpec((1, hh, dk, dk), lambda b, s: (b, 0, 0, 0)),
                  const(cum), const(up), const(mask)],
        out_specs=(pl.BlockSpec((1, ts, f), lambda b, s: (b, sblk(s), 0)),
                   pl.BlockSpec((1, hh, dk, dk), lambda b, s: (b, 0, 0, 0))),
        scratch_shapes=[pltpu.VMEM((hh, dk, dk), F32), pltpu.VMEM((ts, f), F32),
                        pltpu.VMEM((ts, f), F32), pltpu.VMEM((ts, f), F32)],
        compiler_params=_cparams(("parallel", "arbitrary")),
        name="gla_scan_bwd" if rev else "gla_scan_fwd",
    )(p, p, p, lb3, s0, cum, up, mask)


def _hg_readout_body(of_ref, ob_ref, g_ref, gn_ref, w_ref, h_ref, g1_ref, o_ref):
    o = of_ref[0] + ob_ref[0]
    gn = gn_ref[...]
    pieces = []
    for hh in range(HG_HEADS):
        oh = o[:, hh * HG_DK:(hh + 1) * HG_DK]
        y = oh * lax.rsqrt(jnp.mean(oh * oh, axis=-1, keepdims=True) + EPS)
        pieces.append(y * gn)
    on = jnp.concatenate(pieces, axis=1)
    g = g_ref[0]
    on = on * (g * _sigmoid(g))
    y = jnp.dot(on.astype(BF16), w_ref[...], preferred_element_type=F32)
    o_ref[0] = h_ref[0] + g1_ref[0] * y


def hgrn_readout_residual(o_f, o_b, p, g_col, onorm_row, w_bf, h, g1):
    bx, lx, d = h.shape
    f = o_f.shape[2]
    tm = _tile(lx, ROW_TILE)
    return pl.pallas_call(
        _hg_readout_body,
        out_shape=jax.ShapeDtypeStruct((bx, lx, d), F32),
        grid=(bx, lx // tm),
        in_specs=[pl.BlockSpec((1, tm, f), lambda b, i: (b, i, 0)),
                  pl.BlockSpec((1, tm, f), lambda b, i: (b, i, 0)),
                  pl.BlockSpec((1, tm, f), lambda b, i: (b, i, g_col)),
                  pl.BlockSpec((1, HG_DK), lambda b, i: (0, 0)),
                  pl.BlockSpec((f, d), lambda b, i: (0, 0)),
                  pl.BlockSpec((1, tm, d), lambda b, i: (b, i, 0)),
                  pl.BlockSpec((1, 1, d), lambda b, i: (b, 0, 0))],
        out_specs=pl.BlockSpec((1, tm, d), lambda b, i: (b, i, 0)),
        compiler_params=_cparams(("parallel", "parallel")),
        name="hgrn_readout_residual",
    )(o_f, o_b, p, onorm_row, w_bf, h, g1)


def _pair_rows():
    rows = [(0, b) for b in range(16)] + [(1, b) for b in range(8)]
    for a in range(2, 8):
        rows += [(a, b) for b in range(8)]
    rows += [(a, 0) for a in range(8, 16)]
    return rows


def _top16_rows(s):
    n = s.shape[0]
    ri = lax.broadcasted_iota(jnp.int32, s.shape, 0)
    rank = jnp.full(s.shape, float(PEER_TOPK), F32)
    vals = []
    for a in range(PEER_TOPK):
        m = jnp.max(s, axis=0, keepdims=True)
        first = jnp.min(jnp.where(s == m, ri, n), axis=0, keepdims=True)
        sel = ri == first
        rank = jnp.where(sel, float(a), rank)
        s = jnp.where(sel, -jnp.inf, s)
        vals.append(m)
    return rank, jnp.concatenate(vals, axis=0)


def _pair_candidates(v0, v1, okf):
    blocks = [v0[0:1] + v1, v0[1:2] + v1[0:8]]
    blocks += [v0[a:a + 1] + v1[0:8] for a in range(2, 8)]
    blocks += [v0[8:16] + v1[0:1]]
    return jnp.where(okf > 0.0, jnp.concatenate(blocks, axis=0), -jnp.inf)


def _gate_arrays(s0, s1, rank0, rank1, v0, v1, cand, selm):
    top = cand[0:1]
    z = jnp.sum(selm * jnp.exp(jnp.where(selm > 0.0, cand - top, 0.0)), axis=0, keepdims=True)
    cnt = [jnp.sum(selm[0:16], axis=0, keepdims=True)]
    cnt += [jnp.sum(selm[16 + 8 * (a - 1):24 + 8 * (a - 1)], axis=0, keepdims=True) for a in range(1, 8)]
    cnt8 = selm[72:80]
    crow = jnp.zeros(rank0.shape, F32)
    for a in range(8):
        crow = jnp.where(rank0 == float(a), cnt[a], crow)
    for a in range(8, 16):
        crow = jnp.where(rank0 == float(a), cnt8[a - 8:a - 7], crow)
    av = jnp.where(rank0 < float(PEER_TOPK), jnp.exp(s0 - v0[0:1]), 0.0)
    bn = jnp.where(rank1 < float(PEER_TOPK), jnp.exp(s1 - v1[0:1]), 0.0) / z
    return rank1, bn, crow, av


def _select_exact(s0, s1, flat, okf):
    rank0, v0 = _top16_rows(s0)
    rank1, v1 = _top16_rows(s1)
    cand = _pair_candidates(v0, v1, okf)
    work = cand
    selm = jnp.zeros(cand.shape, F32)
    for _ in range(PEER_TOPK):
        m = jnp.max(work, axis=0, keepdims=True)
        first = jnp.min(jnp.where(work == m, flat, 1 << 20), axis=0, keepdims=True)
        sel = flat == first
        selm = jnp.where(sel, 1.0, selm)
        work = jnp.where(sel, -jnp.inf, work)
    return _gate_arrays(s0, s1, rank0, rank1, v0, v1, cand, selm)


def _cmp_exchange(xs, i, l, larger_first):
    hi = jnp.maximum(xs[i], xs[l])
    lo = jnp.minimum(xs[i], xs[l])
    xs[i], xs[l] = (hi, lo) if larger_first else (lo, hi)


def _bitonic_merge_desc(xs):
    n = len(xs)
    j = n // 2
    while j >= 1:
        for i in range(n):
            if (i ^ j) > i:
                _cmp_exchange(xs, i, i ^ j, True)
        j //= 2


def _sorted_top16(s):
    n = PEER_TOPK
    xs = [s[g * 8:(g + 1) * 8] for g in range(n)]
    k = 2
    while k <= n:
        j = k // 2
        while j >= 1:
            for i in range(n):
                if (i ^ j) > i:
                    _cmp_exchange(xs, i, i ^ j, (i & k) == 0)
            j //= 2
        k *= 2
    for shift in (4, 6, 7):
        other = [pltpu.roll(x, shift, 0) for x in xs]
        xs = [jnp.maximum(xs[i], other[n - 1 - i]) for i in range(n)]
        _bitonic_merge_desc(xs)
    return jnp.concatenate([x[0:1] for x in xs], axis=0)


def _count_greater(s, v):
    r = [v[a:a + 1] for a in range(PEER_TOPK)]
    c8 = r[7] > s
    c4 = jnp.where(c8, r[11], r[3]) > s
    c2 = jnp.where(c8, jnp.where(c4, r[13], r[9]), jnp.where(c4, r[5], r[1])) > s
    p1 = jnp.where(c8,
                   jnp.where(c4, jnp.where(c2, r[14], r[12]), jnp.where(c2, r[10], r[8])),
                   jnp.where(c4, jnp.where(c2, r[6], r[4]), jnp.where(c2, r[2], r[0])))
    c1 = p1 > s
    g = (jnp.where(c8, 8.0, 0.0) + jnp.where(c4, 4.0, 0.0)) + (jnp.where(c2, 2.0, 0.0) + jnp.where(c1, 1.0, 0.0))
    return jnp.where(r[15] > s, float(PEER_TOPK), g)


def _select_fast(s0, s1, okf):
    w = s0.shape[1]
    s = jnp.concatenate([s0, s1], axis=1)
    v = _sorted_top16(s)
    rank = _count_greater(s, v)
    dup = jnp.max(jnp.where(v[0:15] == v[1:16], 1.0, 0.0), axis=0, keepdims=True)
    members = jnp.sum(jnp.where(s >= v[15:16], 1.0, 0.0), axis=0, keepdims=True)
    tie = dup + jnp.where(members != float(PEER_TOPK), 1.0, 0.0)
    v0, rank0, tie0 = v[:, :w], rank[:, :w], tie[:, :w]
    v1, rank1, tie1 = v[:, w:], rank[:, w:], tie[:, w:]
    cand = _pair_candidates(v0, v1, okf)
    work = cand
    m = cand[0:1]
    for _ in range(PEER_TOPK):
        m = jnp.max(work, axis=0, keepdims=True)
        work = jnp.where(work == m, -jnp.inf, work)
    selm = jnp.where(cand >= m, 1.0, 0.0)
    tie2 = jnp.where(jnp.sum(selm, axis=0, keepdims=True) != float(PEER_TOPK), 1.0, 0.0)
    return _gate_arrays(s0, s1, rank0, rank1, v0, v1, cand, selm), tie0 + tie1 + tie2


def _peer_prep_body(x_ref, wq_ref, keys_ref, flat_ref, okf_ref, rank1_ref, bn_ref, crow_ref, av_ref, q_scr):
    q_scr[...] = jnp.dot(wq_ref[...], x_ref[...], preferred_element_type=F32).astype(BF16)
    nk = N_KEYS
    tm = x_ref.shape[1]
    flat = flat_ref[...]
    okf = okf_ref[...]

    def store(h, cs, outs):
        rank1, bn, crow, av = outs
        rank1_ref[h, :, cs] = rank1.astype(BF16)
        bn_ref[h, :, cs] = bn.astype(BF16)
        crow_ref[h, :, cs] = crow
        av_ref[h, :, cs] = av

    def head(h, carry):
        r0 = pl.multiple_of(h * 2 * nk, 2 * nk)
        for c0 in range(0, tm, PEER_PREP_CW):
            cs = slice(c0, c0 + PEER_PREP_CW)
            s0 = jnp.dot(keys_ref[h, 0], q_scr[pl.ds(r0, nk), cs], preferred_element_type=F32)
            s1 = jnp.dot(keys_ref[h, 1], q_scr[pl.ds(r0 + nk, nk), cs], preferred_element_type=F32)
            outs, tie = _select_fast(s0, s1, okf)
            store(h, cs, outs)

            @pl.when(jnp.max(tie) > 0.0)
            def _():
                store(h, cs, _select_exact(s0, s1, flat, okf))
        return carry

    lax.fori_loop(0, PEER_HEADS, head, 0)


def peer_prep(x_t, wq_t, keys_bf):
    d, t = x_t.shape
    tm = _tile(t, PEER_PREP_TOK)
    rows = _pair_rows()
    flat = jnp.asarray([[a * 16 + b] for a, b in rows], jnp.int32)
    okf = jnp.asarray([[1.0 if (a + 1) * (b + 1) <= PEER_TOPK else 0.0] for a, b in rows], F32)
    out = jax.ShapeDtypeStruct((PEER_HEADS, N_KEYS, t), F32)
    out_bf = jax.ShapeDtypeStruct((PEER_HEADS, N_KEYS, t), BF16)
    ospec = pl.BlockSpec((PEER_HEADS, N_KEYS, tm), lambda i: (0, 0, i))
    return pl.pallas_call(
        _peer_prep_body,
        out_shape=(out_bf, out_bf, out, out),
        grid=(t // tm,),
        in_specs=[pl.BlockSpec((d, tm), lambda i: (0, i)),
                  pl.BlockSpec(wq_t.shape, lambda i: (0, 0)),
                  pl.BlockSpec(keys_bf.shape, lambda i: (0, 0, 0, 0)),
                  pl.BlockSpec(flat.shape, lambda i: (0, 0)),
                  pl.BlockSpec(okf.shape, lambda i: (0, 0))],
        out_specs=(ospec, ospec, ospec, ospec),
        scratch_shapes=[pltpu.VMEM((wq_t.shape[0], tm), BF16)],
        compiler_params=_cparams(("parallel",)),
        name="peer_prep",
    )(x_t, wq_t, keys_bf, flat, okf)


def _row_tile_bf16(row):
    tile = jnp.broadcast_to(row, (BF16_SUBLANES, row.shape[1])).astype(BF16)
    return jnp.concatenate([tile] * (N_KEYS // BF16_SUBLANES), axis=0)


def _peer_main_body(x_ref, u_ref, v_ref, vlast_ref, rank1_ref, bn_ref, crow_ref, av_ref, h_ref, g2_ref, fg_ref,
                    o_ref, acc_scr, pt_scr, *, final_norm):
    j = pl.program_id(1)
    nj = pl.num_programs(1)
    tm = x_ref.shape[1]
    nrow = u_ref.shape[0] // N_KEYS
    sub = PEER_SUB_ROWS * N_KEYS
    nsb = nrow // PEER_SUB_ROWS
    x = x_ref[...]
    rd = (j + 1) % 2
    wr = j % 2
    row0 = j * nrow

    @pl.when(j == 0)
    def _():
        acc_scr[...] = jnp.zeros(acc_scr.shape, F32)
        pt_scr[1] = jnp.zeros(pt_scr.shape[1:], BF16)

    def pre_act(sb):
        return jnp.dot(u_ref[sb * sub:(sb + 1) * sub, :], x, preferred_element_type=F32)

    pt_prev = pt_scr[rd]
    dn = v_ref.shape[1] // nsb
    acts, pvs = [], []
    for sb in range(nsb):
        acts.append(pre_act(sb))
        pvs.append(lax.dot_general(pt_prev, v_ref[:, sb * dn:(sb + 1) * dn], (((0,), (0,)), ((), ())),
                                   preferred_element_type=F32))
    for sb in range(nsb):
        act = acts[sb]
        for rr in range(PEER_SUB_ROWS):
            e1 = row0 + sb * PEER_SUB_ROWS + rr
            for c0 in range(0, tm, PEER_CW):
                cs = slice(c0, c0 + PEER_CW)
                w = None
                for hd in range(PEER_HEADS):
                    cr = _row_tile_bf16(crow_ref[hd, pl.ds(e1, 1), cs])
                    ar = _row_tile_bf16(av_ref[hd, pl.ds(e1, 1), cs])
                    t = jnp.where(rank1_ref[hd, :, cs] < cr, bn_ref[hd, :, cs], jnp.zeros((), BF16)) * ar
                    w = t if w is None else w + t
                a = act[rr * N_KEYS:(rr + 1) * N_KEYS, cs]
                gelu = 0.5 * a * (1.0 + lax.erf(a * (2.0 ** -0.5)))
                r0 = (sb * PEER_SUB_ROWS + rr) * N_KEYS
                pt_scr[wr, r0:r0 + N_KEYS, cs] = gelu.astype(BF16) * w
    for sb in range(nsb):
        acc_scr[:, sb * dn:(sb + 1) * dn] += pvs[sb]

    @pl.when(j == nj - 1)
    def _():
        last = lax.dot_general(pt_scr[wr], vlast_ref[...], (((0,), (0,)), ((), ())), preferred_element_type=F32)
        hn = h_ref[...] + g2_ref[0] * (acc_scr[...] + last)
        if final_norm:
            hn = hn * lax.rsqrt(jnp.mean(hn * hn, axis=-1, keepdims=True) + EPS) * fg_ref[...]
        o_ref[...] = hn


def peer_main(x_t, u_bf, v_bf, layer, prep, h2, g2, tokens_per_batch, final_g, final_norm):
    d, t = x_t.shape
    e = u_bf.shape[1]
    tm = _tile(tokens_per_batch, PEER_TOK)
    te = PEER_ROWS * N_KEYS
    assert e % te == 0 and tm % PEER_CW == 0
    tpb = tokens_per_batch // tm
    nj = e // te
    assert PEER_ROWS % PEER_SUB_ROWS == 0 and PEER_ROWS // PEER_SUB_ROWS >= 2
    pspec = pl.BlockSpec((PEER_HEADS, N_KEYS, tm), lambda i, j: (0, 0, i))
    return pl.pallas_call(
        functools.partial(_peer_main_body, final_norm=final_norm),
        out_shape=jax.ShapeDtypeStruct((t, d), F32),
        grid=(t // tm, nj),
        in_specs=[pl.BlockSpec((d, tm), lambda i, j: (0, i)),
                  pl.BlockSpec((None, te, d), lambda i, j: (layer, j, 0)),
                  pl.BlockSpec((None, te, d), lambda i, j: (layer, jnp.maximum(j - 1, 0), 0)),
                  pl.BlockSpec((None, te, d), lambda i, j: (layer, nj - 1, 0)),
                  pspec, pspec, pspec, pspec,
                  pl.BlockSpec((tm, d), lambda i, j: (i, 0)),
                  pl.BlockSpec((1, 1, d), lambda i, j: (i // tpb, 0, 0)),
                  pl.BlockSpec((1, d), lambda i, j: (0, 0))],
        out_specs=pl.BlockSpec((tm, d), lambda i, j: (i, 0)),
        scratch_shapes=[pltpu.VMEM((tm, d), F32), pltpu.VMEM((2, te, tm), BF16)],
        compiler_params=_cparams(("parallel", "arbitrary")),
        name="peer_dense",
    )(x_t, u_bf, v_bf, v_bf, *prep, h2, g2, final_g)


def peer_residual(h, norm_g, sh, sc, g2, wq_t, keys_bf, u_bf, v_bf, layer, final_g, final_norm):
    bx, lx, d = h.shape
    a_t = norm_mod_t(h, norm_g, sh, sc)
    prep = peer_prep(a_t, wq_t, keys_bf)
    out = peer_main(a_t, u_bf, v_bf, layer, prep, h.reshape(bx * lx, d), g2, lx, final_g, final_norm)
    return out.reshape(bx, lx, d)


def kernel(x, c, ctx, c_ctx, ada_w, ada_b, norm1_g, norm2_g, final_g, ab_w_in, ab_w_out, hy_conv_w, hy_conv_b,
           hy_filt_w1, hy_filt_b1, hy_filt_w2, hy_filt_b2, hy_filt_w3, hy_filt_freq, hy_skip, attn_sink,
           hg_w_in, hg_w_out, hg_lb_logits, hg_onorm_g, peer_wq, peer_keys, peer_u, peer_v):
    bsz, seq, d = x.shape
    depth = ada_w.shape[0]
    assert depth == 2
    lb_p = jax.nn.softmax(hg_lb_logits.astype(F32), axis=0)
    lb_all = jnp.cumsum(lb_p, axis=0) - lb_p[0:1]

    c16 = jnp.concatenate([c, c_ctx[None], jnp.zeros((16 - bsz - 1, d), F32)], axis=0)
    final_row = final_g[None]
    u_bf = peer_u.astype(BF16)
    v_bf = peer_v.astype(BF16)
    h_lat, h_ctx = x, ctx

    for l in range(depth):
        need_ctx = l < depth - 1
        mod = ada_mod(c16, ada_w, ada_b[:, None, :], l)
        lat = [mod[:bsz, i * d:(i + 1) * d][:, None, :] for i in range(6)]
        cx = [jnp.broadcast_to(mod[bsz:bsz + 1, i * d:(i + 1) * d][:, None, :], (bsz, 1, d)) for i in range(6)]
        n1 = norm1_g[l][None]
        n2 = norm2_g[l][None]
        wq_t = peer_wq[l].T.astype(BF16)
        keys_bf = peer_keys[l].astype(BF16)
        j = l // 2
        if l % 2 == 0:
            w_in = ab_w_in[j].astype(BF16)
            w_out = ab_w_out[j].astype(BF16)
            p_lat = norm_mod_matmul(h_lat, n1, lat[0], lat[1], w_in)
            p_ctx = norm_mod_matmul(h_ctx, n1, cx[0], cx[1], w_in)
            hy = (hy_conv_w[j], hy_conv_b[j][None], hy_filt_w1[j], hy_filt_b1[j], hy_filt_w2[j], hy_filt_b2[j],
                  hy_filt_w3[j], hy_filt_freq[j], hy_skip[j])
            cos, sin = _rope_tables(seq)
            qr, kr = rope_qk(p_lat, cos, sin, 3, 16)
            sink = attn_sink[j]
            at_lat = window_attention(qr, kr, p_lat, p_ctx, sink, 16, 17)
            hy_lat = hyena_mixer(p_lat, *hy)
            h_lat = out_proj_residual(hy_lat, at_lat, w_out, h_lat, lat[2])
            if need_ctx:
                at_ctx = context_attention(p_ctx, sink, 3, 16, 17)
                hy_ctx = hyena_mixer(p_ctx, *hy)
                h_ctx = out_proj_residual(hy_ctx, at_ctx, w_out, h_ctx, cx[2])
        else:
            w_in = hg_w_in[j].astype(BF16)
            w_out = hg_w_out[j].astype(BF16)
            lb3 = lb_all[l].astype(F32)[:, None, :]
            p_lat = norm_mod_matmul(h_lat, n1, lat[0], lat[1], w_in)
            p_ctx = norm_mod_matmul(h_ctx, n1, cx[0], cx[1], w_in)
            s0 = jnp.zeros((bsz, HG_HEADS, HG_DK, HG_DK), F32)
            o_cf, s_cf = gla_scan(p_ctx, lb3, s0, 0, 0, 1, 3)
            o_cb, s_cb = gla_scan(p_ctx, lb3, s0, 1, 0, 2, 3)
            o_lf, _ = gla_scan(p_lat, lb3, s_cf, 0, 0, 1, 3)
            o_lb, _ = gla_scan(p_lat, lb3, s_cb, 1, 0, 2, 3)
            onorm = hg_onorm_g[j].astype(F32)[None]
            h_lat = hgrn_readout_residual(o_lf, o_lb, p_lat, 4, onorm, w_out, h_lat, lat[2])
            if need_ctx:
                h_ctx = hgrn_readout_residual(o_cf, o_cb, p_ctx, 4, onorm, w_out, h_ctx, cx[2])
        last = l == depth - 1
        h_lat = peer_residual(h_lat, n2, lat[3], lat[4], lat[5], wq_t, keys_bf, u_bf, v_bf, l, final_row, last)
        if need_ctx:
            h_ctx = peer_residual(h_ctx, n2, cx[3], cx[4], cx[5], wq_t, keys_bf, u_bf, v_bf, l, final_row, False)
    return h_lat
```

```python
import functools
import math

import jax
import jax.numpy as jnp
import numpy as np
from jax import lax
from jax.experimental import pallas as pl
from jax.experimental.pallas import tpu as pltpu

F32 = jnp.float32
BF16 = jnp.bfloat16
EPS = 1e-6

HY_C = 512
HY_EMB = 33
HY_EMB_PAD = 40
HY_FILT_W = 64
HY_DECAY_SHORT_PCT = 0.3
HY_DECAY_LONG_PCT = 1.5
HY_TARGET = 1e-2
HEAD_DIM = 64
N_Q_HEADS = 8
N_KV_HEADS = 2
GQA_GROUP = 4
WINDOW = 128
ATTN_BLK = 128
GRID_W = 64
ROPE_BASE = 10000.0
HG_HEADS = 8
HG_DK = 128
PEER_HEADS = 8
PEER_TOPK = 16
N_KEYS = 128

LANES = 128
BF16_SUBLANES = 16
VMEM_LIMIT_BYTES = 56 * 1024 * 1024
ROW_TILE = 512
PROJ_ROW_TILE = 1024
GLA_CHUNK = 128
GLA_SEQ_BLOCK = 256
PEER_PREP_TOK = 256
PEER_PREP_CW = 256
PEER_TOK = 512
PEER_ROWS = 8
PEER_SUB_ROWS = 2
PEER_CW = 256
NEG_BIG = -1e30


def _cparams(sem):
    return pltpu.CompilerParams(dimension_semantics=sem, vmem_limit_bytes=VMEM_LIMIT_BYTES)


def _sigmoid(x):
    return 1.0 / (1.0 + jnp.exp(-x))


def _tile(n, t):
    t = min(n, t)
    assert n % t == 0, (n, t)
    return t


def _ada_body(c_ref, w_ref, b_ref, o_ref):
    c = c_ref[...]
    s = c * _sigmoid(c)
    o_ref[...] = jnp.dot(s.astype(BF16), w_ref[...].astype(BF16), preferred_element_type=F32) + b_ref[...]


def ada_mod(c16, w_all, b_all, layer):
    _, d, n = w_all.shape
    tn = _tile(n, 1536)
    return pl.pallas_call(
        _ada_body,
        out_shape=jax.ShapeDtypeStruct((c16.shape[0], n), F32),
        grid=(n // tn,),
        in_specs=[pl.BlockSpec(c16.shape, lambda j: (0, 0)),
                  pl.BlockSpec((None, d, tn), lambda j: (layer, 0, j)),
                  pl.BlockSpec((None, 1, tn), lambda j: (layer, 0, j))],
        out_specs=pl.BlockSpec((c16.shape[0], tn), lambda j: (0, j)),
        compiler_params=_cparams(("arbitrary",)),
        name="ada_mod",
    )(c16, w_all, b_all)


def _norm_mod(x, g, sh, sc):
    y = x * lax.rsqrt(jnp.mean(x * x, axis=-1, keepdims=True) + EPS)
    return (y * g) * (1.0 + sc) + sh


def _nm_matmul_body(h_ref, g_ref, sh_ref, sc_ref, w_ref, p_ref, a_scr):
    @pl.when(pl.program_id(2) == 0)
    def _():
        a_scr[...] = _norm_mod(h_ref[0], g_ref[...], sh_ref[0], sc_ref[0]).astype(BF16)

    p_ref[0] = jnp.dot(a_scr[...], w_ref[...], preferred_element_type=F32)


def norm_mod_matmul(h, g, sh, sc, w_bf):
    bx, lx, d = h.shape
    n = w_bf.shape[1]
    tm = _tile(lx, PROJ_ROW_TILE)
    tn = _tile(n, 1280 if n % 1280 == 0 else (1152 if n % 1152 == 0 else 1024))
    return pl.pallas_call(
        _nm_matmul_body,
        out_shape=jax.ShapeDtypeStruct((bx, lx, n), F32),
        grid=(bx, lx // tm, n // tn),
        in_specs=[pl.BlockSpec((1, tm, d), lambda b, i, j: (b, i, 0)),
                  pl.BlockSpec((1, d), lambda b, i, j: (0, 0)),
                  pl.BlockSpec((1, 1, d), lambda b, i, j: (b, 0, 0)),
                  pl.BlockSpec((1, 1, d), lambda b, i, j: (b, 0, 0)),
                  pl.BlockSpec((d, tn), lambda b, i, j: (0, j))],
        out_specs=pl.BlockSpec((1, tm, tn), lambda b, i, j: (b, i, j)),
        scratch_shapes=[pltpu.VMEM((tm, d), BF16)],
        compiler_params=_cparams(("parallel", "parallel", "arbitrary")),
        name="norm_mod_matmul",
    )(h, g, sh, sc, w_bf)


def _nm_only_body(h_ref, g_ref, sh_ref, sc_ref, a_ref):
    a_ref[...] = _norm_mod(h_ref[0], g_ref[...], sh_ref[0], sc_ref[0]).T.astype(BF16)


def norm_mod_t(h, g, sh, sc):
    bx, lx, d = h.shape
    tm = _tile(lx, ROW_TILE)
    nt = lx // tm
    return pl.pallas_call(
        _nm_only_body,
        out_shape=jax.ShapeDtypeStruct((d, bx * lx), BF16),
        grid=(bx, nt),
        in_specs=[pl.BlockSpec((1, tm, d), lambda b, i: (b, i, 0)),
                  pl.BlockSpec((1, d), lambda b, i: (0, 0)),
                  pl.BlockSpec((1, 1, d), lambda b, i: (b, 0, 0)),
                  pl.BlockSpec((1, 1, d), lambda b, i: (b, 0, 0))],
        out_specs=pl.BlockSpec((d, tm), lambda b, i: (0, b * nt + i)),
        compiler_params=_cparams(("parallel", "parallel")),
        name="norm_mod_t",
    )(h, g, sh, sc)


def _mm_body(a_ref, b_ref, o_ref):
    o_ref[...] = jnp.dot(a_ref[...], b_ref[...], preferred_element_type=F32)


def matmul_bf16(a, b):
    m, k = a.shape
    n = b.shape[1]
    tm = _tile(m, 1024)
    tn = _tile(n, 1024)
    return pl.pallas_call(
        _mm_body,
        out_shape=jax.ShapeDtypeStruct((m, n), F32),
        grid=(m // tm, n // tn),
        in_specs=[pl.BlockSpec((tm, k), lambda i, j: (i, 0)),
                  pl.BlockSpec((k, tn), lambda i, j: (0, j))],
        out_specs=pl.BlockSpec((tm, tn), lambda i, j: (i, j)),
        compiler_params=_cparams(("parallel", "parallel")),
        name="matmul_bf16",
    )(a, b)


def _filter_body(z_ref, w1_ref, b1_ref, w2_ref, b2_ref, w3_ref, fr_ref, dec_ref, o_ref):
    hi = lax.Precision.HIGHEST
    fr = fr_ref[...]
    hdn = jnp.sin(fr * (jnp.dot(z_ref[...], w1_ref[...], precision=hi, preferred_element_type=F32) + b1_ref[...]))
    hdn = jnp.sin(fr * (jnp.dot(hdn, w2_ref[...], precision=hi, preferred_element_type=F32) + b2_ref[...]))
    h = jnp.dot(hdn, w3_ref[...], precision=hi, preferred_element_type=F32)
    dec = dec_ref[...]
    c = dec.shape[1]
    h0 = h[:, :c] * dec
    h1 = h[:, c:] * dec
    nrm = jnp.sum(jnp.abs(h0) + jnp.abs(h1), axis=0, keepdims=True)
    inv = 1.0 / nrm
    ri = lax.broadcasted_iota(jnp.int32, h1.shape, 0)
    o_ref[:, :c] = (h0 * inv).astype(BF16)
    o_ref[:, c:] = jnp.where(ri == 0, 0.0, h1 * inv).astype(BF16)


def hyena_filters(zfeat, w1p, b1, w2, b2, w3, freq, decay):
    l = zfeat.shape[0]
    c = decay.shape[1]
    n_order = w3.shape[1] // (2 * c)
    full = lambda shape: pl.BlockSpec(shape, lambda o: (0,) * len(shape))
    return pl.pallas_call(
        _filter_body,
        out_shape=jax.ShapeDtypeStruct((l, n_order * 2 * c), BF16),
        grid=(n_order,),
        in_specs=[full(zfeat.shape), full(w1p.shape), full(b1.shape), full(w2.shape), full(b2.shape),
                  pl.BlockSpec((w3.shape[0], 2 * c), lambda o: (0, o)),
                  full(freq.shape), full(decay.shape)],
        out_specs=pl.BlockSpec((l, 2 * c), lambda o: (0, o)),
        compiler_params=_cparams(("arbitrary",)),
        name="hyena_filters",
    )(zfeat, w1p, b1, w2, b2, w3, freq, decay)


def _short_conv_body(u_ref, w_ref, b_ref, o_ref):
    u = u_ref[0]
    l = u.shape[0]
    ri = lax.broadcasted_iota(jnp.int32, u.shape, 0)
    prev = jnp.where(ri == 0, 0.0, pltpu.roll(u, 1, 0))
    nxt = jnp.where(ri == l - 1, 0.0, pltpu.roll(u, l - 1, 0))
    w = w_ref[...]
    o_ref[0] = prev * w[0:1] + u * w[1:2] + nxt * w[2:3] + b_ref[...]


def short_conv(p, conv_w, conv_b):
    bx, lx, _ = p.shape
    c3 = conv_w.shape[1]
    tc = _tile(c3, 512)
    return pl.pallas_call(
        _short_conv_body,
        out_shape=jax.ShapeDtypeStruct((bx, lx, c3), F32),
        grid=(bx, c3 // tc),
        in_specs=[pl.BlockSpec((1, lx, tc), lambda b, j: (b, 0, j)),
                  pl.BlockSpec((3, tc), lambda b, j: (0, j)),
                  pl.BlockSpec((1, tc), lambda b, j: (0, j))],
        out_specs=pl.BlockSpec((1, lx, tc), lambda b, j: (b, 0, j)),
        compiler_params=_cparams(("parallel", "parallel")),
        name="hyena_short_conv",
    )(p, conv_w, conv_b)


def _hy_fwd_body(z_ref, f_ref, h0_ref, h1_ref, y_ref):
    z = z_ref[0].astype(BF16)
    kt = f_ref.shape[1]
    zre = jnp.dot(f_ref[0], z, preferred_element_type=F32)
    zim = jnp.dot(f_ref[1], z, preferred_element_type=F32)
    k0 = (pl.program_id(0) * kt + lax.broadcasted_iota(jnp.int32, zre.shape, 0)) == 0
    hre = h0_ref[0] + h1_ref[0]
    him = jnp.where(k0, h0_ref[1] + h1_ref[1], h0_ref[1] - h1_ref[1])
    yre = jnp.where(k0, zre * hre, zre * hre - zim * him)
    yim = jnp.where(k0, zim * him, zre * him + zim * hre)
    y_ref[0, 0] = yre.astype(BF16)
    y_ref[0, 1] = yim.astype(BF16)


def hyena_fwd(z_src, z_col, f3, fh3, order, c):
    bx, lx, _ = z_src.shape
    kt = _tile(lx, 512)
    return pl.pallas_call(
        _hy_fwd_body,
        out_shape=jax.ShapeDtypeStruct((bx, 2, lx, c), BF16),
        grid=(lx // kt, bx),
        in_specs=[pl.BlockSpec((1, lx, c), lambda k, b: (b, 0, z_col)),
                  pl.BlockSpec((2, kt, lx), lambda k, b: (0, k, 0)),
                  pl.BlockSpec((2, kt, c), lambda k, b: (0, k, 2 * order)),
                  pl.BlockSpec((2, kt, c), lambda k, b: (0, k, 2 * order + 1))],
        out_specs=pl.BlockSpec((1, 2, kt, c), lambda k, b: (b, 0, k, 0)),
        compiler_params=_cparams(("parallel", "parallel")),
        name="hyena_dft_fwd",
    )(z_src, f3, fh3, fh3)


def _hy_inv_body(y_ref, fi_ref, gate_ref, z_ref, skip_ref, o_ref):
    conv = jnp.dot(fi_ref[...], y_ref[0], preferred_element_type=F32)
    o_ref[0] = gate_ref[0] * (conv + skip_ref[...] * z_ref[0])


def hyena_inv(yf, finv, gate_src, gate_col, z_src, z_col, skip_row):
    bx, l2, c = yf.shape
    lx = l2 // 2
    tl = _tile(lx, 512)
    return pl.pallas_call(
        _hy_inv_body,
        out_shape=jax.ShapeDtypeStruct((bx, lx, c), F32),
        grid=(lx // tl, bx),
        in_specs=[pl.BlockSpec((1, l2, c), lambda t, b: (b, 0, 0)),
                  pl.BlockSpec((tl, l2), lambda t, b: (t, 0)),
                  pl.BlockSpec((1, tl, c), lambda t, b: (b, t, gate_col)),
                  pl.BlockSpec((1, tl, c), lambda t, b: (b, t, z_col)),
                  pl.BlockSpec((1, c), lambda t, b: (0, 0))],
        out_specs=pl.BlockSpec((1, tl, c), lambda t, b: (b, t, 0)),
        compiler_params=_cparams(("parallel", "parallel")),
        name="hyena_dft_inv",
    )(yf, finv, gate_src, z_src, skip_row)


def _dft_mats(l):
    n = 2 * l
    k = jnp.arange(l, dtype=jnp.int32)[:, None]
    s = jnp.arange(l, dtype=jnp.int32)[None, :]

    def table(rows):
        ang = ((rows[:, None] * s) % n).astype(F32) * (2.0 * math.pi / n)
        return jnp.cos(ang), jnp.sin(ang)

    r = 1
    while r * r < l:
        r *= 2
    ch, sh = table(jnp.arange(l // r, dtype=jnp.int32) * r)
    cl, sl = table(jnp.arange(r, dtype=jnp.int32))
    cosm = (ch[:, None, :] * cl[None, :, :] - sh[:, None, :] * sl[None, :, :]).reshape(l, l)
    sinm = -(sh[:, None, :] * cl[None, :, :] + ch[:, None, :] * sl[None, :, :]).reshape(l, l)
    nyq = jnp.where(s % 2 == 0, 1.0, -1.0).astype(F32)
    imag = jnp.where(k == 0, nyq, sinm)
    fwd = jnp.concatenate([cosm, imag], axis=0)
    scale = jnp.where(jnp.arange(l) == 0, 1.0 / n, 2.0 / n).astype(F32)
    inv = jnp.concatenate([cosm.T * scale[None, :], imag.T * scale[None, :]], axis=1)
    return fwd.astype(BF16), inv.astype(BF16)


def _filter_features(l):
    bands = (HY_EMB - 1) // 2
    t = jnp.linspace(0.0, 1.0, l, dtype=F32)[:, None]
    w = 2.0 * math.pi * jnp.arange(l, dtype=F32)[:, None] / l
    f = jnp.linspace(1e-4, bands - 1, bands, dtype=F32)[None, :]
    z = jnp.concatenate([t, jnp.cos(f * w), -jnp.sin(f * w)], axis=-1)
    z = jnp.pad(z, ((0, 0), (0, HY_EMB_PAD - HY_EMB)))
    max_decay = math.log(HY_TARGET) / HY_DECAY_SHORT_PCT
    min_decay = math.log(HY_TARGET) / HY_DECAY_LONG_PCT
    deltas = jnp.abs(jnp.linspace(min_decay, max_decay, HY_C, dtype=F32))
    return z, jnp.exp(-t * deltas)


def hyena_mixer(p, conv_w, conv_b, fw1, fb1, fw2, fb2, fw3, ffreq, d_skip):
    bx, lx, _ = p.shape
    c = HY_C
    zfeat, decay = _filter_features(lx)
    w1p = jnp.pad(fw1, ((0, HY_EMB_PAD - HY_EMB), (0, 0)))
    filt = hyena_filters(zfeat, w1p, fb1[None], fw2, fb2[None], fw3, ffreq[None], decay)
    fwd, inv = _dft_mats(lx)
    fh3 = matmul_bf16(fwd, filt).reshape(2, lx, filt.shape[1])
    f3 = fwd.reshape(2, lx, lx)
    uc = short_conv(p, conv_w, conv_b)
    y0 = hyena_fwd(uc, 0, f3, fh3, 0, c).reshape(bx, 2 * lx, c)
    z1 = hyena_inv(y0, inv, uc, 1, uc, 0, d_skip[0:1])
    y1 = hyena_fwd(z1, 0, f3, fh3, 1, c).reshape(bx, 2 * lx, c)
    return hyena_inv(y1, inv, uc, 2, z1, 0, d_skip[1:2])


def _rope_tables(l):
    rows = l // GRID_W
    r = jnp.broadcast_to(jnp.arange(rows)[:, None], (rows, GRID_W)).reshape(-1).astype(F32)
    col = jnp.broadcast_to(jnp.arange(GRID_W)[None, :], (rows, GRID_W)).reshape(-1).astype(F32)
    nf = HEAD_DIM // 4
    inv = ROPE_BASE ** (-jnp.arange(nf, dtype=F32) / nf)
    ar = r[:, None] * inv
    ac = col[:, None] * inv
    ang = jnp.concatenate([ar, ar, ac, ac], axis=-1)
    sign = jnp.concatenate([-jnp.ones((nf,)), jnp.ones((nf,))] * 2).astype(F32)
    cos = jnp.tile(jnp.cos(ang), (1, N_Q_HEADS))
    sin = jnp.tile(jnp.sin(ang) * sign[None, :], (1, N_Q_HEADS))
    return cos, sin


def _rope_body(q_ref, k_ref, cos_ref, sin_ref, qo_ref, ko_ref):
    def rot(x, cos, sin):
        n = x.shape[1]
        lane = lax.broadcasted_iota(jnp.int32, x.shape, 1)
        first = (lane % 32) < 16
        partner = jnp.where(first, pltpu.roll(x, n - 16, 1), pltpu.roll(x, 16, 1))
        return x * cos + partner * sin

    nk = k_ref.shape[2]
    qo_ref[0] = (rot(q_ref[0], cos_ref[...], sin_ref[...]) * (HEAD_DIM ** -0.5)).astype(BF16)
    ko_ref[0] = rot(k_ref[0], cos_ref[:, :nk], sin_ref[:, :nk]).astype(BF16)


def rope_qk(p, cos, sin, q_col, k_col):
    bx, lx, _ = p.shape
    nq = N_Q_HEADS * HEAD_DIM
    nk = N_KV_HEADS * HEAD_DIM
    tr = _tile(lx, ROW_TILE)
    return pl.pallas_call(
        _rope_body,
        out_shape=(jax.ShapeDtypeStruct((bx, lx, nq), BF16), jax.ShapeDtypeStruct((bx, lx, nk), BF16)),
        grid=(bx, lx // tr),
        in_specs=[pl.BlockSpec((1, tr, nq), lambda b, i: (b, i, q_col)),
                  pl.BlockSpec((1, tr, nk), lambda b, i: (b, i, k_col)),
                  pl.BlockSpec((tr, nq), lambda b, i: (i, 0)),
                  pl.BlockSpec((tr, nq), lambda b, i: (i, 0))],
        out_specs=(pl.BlockSpec((1, tr, nq), lambda b, i: (b, i, 0)),
                   pl.BlockSpec((1, tr, nk), lambda b, i: (b, i, 0))),
        compiler_params=_cparams(("parallel", "parallel")),
        name="rope_qk",
    )(p, p, cos, sin)


def _attn_heads(q, sink_ref, parts_for_head, o_ref):
    t = q.shape[0]
    nt = (((1,), (1,)), ((), ()))
    heads = range(N_KV_HEADS)
    q4, sink, parts = [], [], []
    for hk in heads:
        q4.append(jnp.concatenate(
            [q[:, (hk * GQA_GROUP + g) * HEAD_DIM:(hk * GQA_GROUP + g + 1) * HEAD_DIM] for g in range(GQA_GROUP)],
            axis=0))
        sink.append(jnp.concatenate(
            [jnp.full((t, 1), sink_ref[hk * GQA_GROUP + g], F32) for g in range(GQA_GROUP)], axis=0))
        parts.append(parts_for_head(hk))
    ss = []
    for hk in heads:
        row = []
        for k, _, mask in parts[hk]:
            sc = lax.dot_general(q4[hk], k, nt, preferred_element_type=F32)
            row.append(sc if mask is None else jnp.where(mask, sc, NEG_BIG))
        ss.append(row)
    m = []
    for hk in heads:
        mh = sink[hk]
        for sc in ss[hk]:
            mh = jnp.maximum(mh, jnp.max(sc, axis=-1, keepdims=True))
        m.append(mh)
    es = [[jnp.exp(sc - m[hk]) for sc in ss[hk]] for hk in heads]
    den = []
    for hk in heads:
        dh = jnp.exp(sink[hk] - m[hk])
        for e in es[hk]:
            dh = dh + jnp.sum(e, axis=-1, keepdims=True)
        den.append(dh)
    outs = []
    for hk in heads:
        acc = None
        for e, (_, v, _) in zip(es[hk], parts[hk]):
            o = jnp.dot(e.astype(BF16), v, preferred_element_type=F32)
            acc = o if acc is None else acc + o
        o4 = acc / den[hk]
        outs.extend([o4[g * t:(g + 1) * t] for g in range(GQA_GROUP)])
    o_ref[0] = jnp.concatenate(outs, axis=1)


def _win_attn_body(sink_ref, q_ref, k0_ref, k1_ref, k2_ref, v0_ref, v1_ref, v2_ref, kc_ref, vc_ref, o_ref, *, seq_len):
    n = pl.program_id(1)
    blk = q_ref.shape[1]
    kb = jnp.concatenate([k0_ref[0], k1_ref[0], k2_ref[0]], axis=0)
    vb = jnp.concatenate([v0_ref[0], v1_ref[0], v2_ref[0]], axis=0).astype(BF16)
    kc = kc_ref[0].astype(BF16)
    vc = vc_ref[0].astype(BF16)
    qi = lax.broadcasted_iota(jnp.int32, (blk, 3 * blk), 0)
    kj = lax.broadcasted_iota(jnp.int32, (blk, 3 * blk), 1)
    kpos = (n - 1) * blk + kj
    diff = qi + blk - kj
    valid = (kpos >= 0) & (kpos < seq_len) & (diff <= WINDOW) & (diff >= -WINDOW)
    valid4 = jnp.concatenate([valid] * GQA_GROUP, axis=0)

    def parts(hk):
        sl = slice(hk * HEAD_DIM, (hk + 1) * HEAD_DIM)
        return [(kb[:, sl], vb[:, sl], valid4), (kc[:, sl], vc[:, sl], None)]

    _attn_heads(q_ref[0], sink_ref, parts, o_ref)


def window_attention(qr, kr, p, p_ctx, sink, k_col, v_col):
    bx, lx, nq = qr.shape
    nk = kr.shape[2]
    lc = p_ctx.shape[1]
    blk = ATTN_BLK
    nb = lx // blk
    lo = lambda b, n: (b, jnp.maximum(n - 1, 0), 0)
    mid = lambda b, n: (b, n, 0)
    hi = lambda b, n: (b, jnp.minimum(n + 1, nb - 1), 0)
    vlo = lambda b, n: (b, jnp.maximum(n - 1, 0), v_col)
    vmid = lambda b, n: (b, n, v_col)
    vhi = lambda b, n: (b, jnp.minimum(n + 1, nb - 1), v_col)
    return pl.pallas_call(
        functools.partial(_win_attn_body, seq_len=lx),
        out_shape=jax.ShapeDtypeStruct((bx, lx, nq), F32),
        grid=(bx, nb),
        in_specs=[pl.BlockSpec(memory_space=pltpu.SMEM),
                  pl.BlockSpec((1, blk, nq), mid),
                  pl.BlockSpec((1, blk, nk), lo), pl.BlockSpec((1, blk, nk), mid), pl.BlockSpec((1, blk, nk), hi),
                  pl.BlockSpec((1, blk, nk), vlo), pl.BlockSpec((1, blk, nk), vmid), pl.BlockSpec((1, blk, nk), vhi),
                  pl.BlockSpec((1, lc, nk), lambda b, n: (b, 0, k_col)),
                  pl.BlockSpec((1, lc, nk), lambda b, n: (b, 0, v_col))],
        out_specs=pl.BlockSpec((1, blk, nq), mid),
        compiler_params=_cparams(("parallel", "parallel")),
        name="window_attention",
    )(sink, qr, kr, kr, kr, p, p, p, p_ctx, p_ctx)


def _ctx_attn_body(sink_ref, q_ref, kc_ref, vc_ref, o_ref):
    kc = kc_ref[0].astype(BF16)
    vc = vc_ref[0].astype(BF16)
    q = (q_ref[0] * (HEAD_DIM ** -0.5)).astype(BF16)

    def parts(hk):
        sl = slice(hk * HEAD_DIM, (hk + 1) * HEAD_DIM)
        return [(kc[:, sl], vc[:, sl], None)]

    _attn_heads(q, sink_ref, parts, o_ref)


def context_attention(p_ctx, sink, q_col, k_col, v_col):
    bx, lc, _ = p_ctx.shape
    nq = N_Q_HEADS * HEAD_DIM
    nk = N_KV_HEADS * HEAD_DIM
    return pl.pallas_call(
        _ctx_attn_body,
        out_shape=jax.ShapeDtypeStruct((bx, lc, nq), F32),
        grid=(bx,),
        in_specs=[pl.BlockSpec(memory_space=pltpu.SMEM),
                  pl.BlockSpec((1, lc, nq), lambda b: (b, 0, q_col)),
                  pl.BlockSpec((1, lc, nk), lambda b: (b, 0, k_col)),
                  pl.BlockSpec((1, lc, nk), lambda b: (b, 0, v_col))],
        out_specs=pl.BlockSpec((1, lc, nq), lambda b: (b, 0, 0)),
        compiler_params=_cparams(("parallel",)),
        name="context_attention",
    )(sink, p_ctx, p_ctx, p_ctx)


def _out_proj_body(ya_ref, yb_ref, wa_ref, wb_ref, h_ref, g_ref, o_ref):
    y = jnp.dot(ya_ref[0].astype(BF16), wa_ref[...], preferred_element_type=F32)
    y = y + jnp.dot(yb_ref[0].astype(BF16), wb_ref[...], preferred_element_type=F32)
    o_ref[0] = h_ref[0] + g_ref[0] * y


def out_proj_residual(ya, yb, w_bf, h, g1):
    bx, lx, d = h.shape
    ca = ya.shape[2]
    cb = yb.shape[2]
    assert ca == cb and w_bf.shape[0] == ca + cb
    tm = _tile(lx, ROW_TILE)
    return pl.pallas_call(
        _out_proj_body,
        out_shape=jax.ShapeDtypeStruct((bx, lx, d), F32),
        grid=(bx, lx // tm),
        in_specs=[pl.BlockSpec((1, tm, ca), lambda b, i: (b, i, 0)),
                  pl.BlockSpec((1, tm, cb), lambda b, i: (b, i, 0)),
                  pl.BlockSpec((ca, d), lambda b, i: (0, 0)),
                  pl.BlockSpec((cb, d), lambda b, i: (1, 0)),
                  pl.BlockSpec((1, tm, d), lambda b, i: (b, i, 0)),
                  pl.BlockSpec((1, 1, d), lambda b, i: (b, 0, 0))],
        out_specs=pl.BlockSpec((1, tm, d), lambda b, i: (b, i, 0)),
        compiler_params=_cparams(("parallel", "parallel")),
        name="out_proj_residual",
    )(ya, yb, w_bf, w_bf, h, g1)


def _gla_consts(c, reverse):
    t = np.arange(c)
    tau = (c - 1 - t) if reverse else t
    mats = [tau[None, :] <= tau[:, None]]
    ups, masks = [], []
    m = 1
    while m < c:
        blk = tau // (2 * m)
        ref = blk * 2 * m + m - 1
        upper = (tau % (2 * m)) >= m
        mats.append(tau[None, :] <= ref[:, None])
        ups.append(upper[:, None])
        masks.append((blk[:, None] == blk[None, :]) & upper[:, None] & (~upper)[None, :])
        m *= 2
    sgn = np.broadcast_to(2.0 * np.stack(ups).astype(np.float32) - 1.0, (len(ups), c, LANES))
    return jnp.asarray(mats[0], BF16), jnp.asarray(sgn, F32), jnp.asarray(np.stack(masks), F32)


def _ref_rows(b, m, reverse):
    c, w = b.shape
    off = m if reverse else m - 1
    if 2 * m >= 8:
        g = b.reshape(c // (2 * m), 2 * m, w)
        return jnp.broadcast_to(g[:, off:off + 1, :], g.shape).reshape(c, w)
    pos = lax.broadcasted_iota(jnp.int32, b.shape, 0) % (2 * m)
    out = b
    for p in range(2 * m):
        if p != off:
            out = jnp.where(pos == p, pltpu.roll(b, (p - off) % c, 0), out)
    return out


def _gla_body(q_ref, f_ref, v_ref, lb_ref, s0_ref, cum_ref, up_ref, mask_ref, *rest, reverse, need_o):
    if need_o:
        o_ref, sout_ref, st_scr, qs_scr, kk_scr, lf_scr = rest
    else:
        sout_ref, st_scr, qs_scr, kk_scr, lf_scr = rest
    s_idx = pl.program_id(1)
    ts = q_ref.shape[1]
    c = GLA_CHUNK
    nch = ts // c
    nlev = up_ref.shape[0]
    dk = HG_DK
    nt = (((1,), (1,)), ((), ()))
    tn = (((0,), (0,)), ((), ()))

    @pl.when(s_idx == 0)
    def _():
        st_scr[...] = s0_ref[0]

    q = q_ref[0]
    lb = lb_ref[0]
    fg = lb + (1.0 - lb) * _sigmoid(f_ref[0])
    qs_scr[...] = q * _sigmoid(q)
    kk_scr[...] = 1.0 - fg
    lf_scr[...] = jnp.log(fg)

    def chunk(i, carry):
        ci = (nch - 1 - i) if reverse else i
        r0 = pl.multiple_of(ci * c, c)
        heads = range(HG_HEADS)
        cols = [slice(hd * dk, (hd + 1) * dk) for hd in heads]
        qc = [qs_scr[pl.ds(r0, c), cs] for cs in cols]
        kc = [kk_scr[pl.ds(r0, c), cs] for cs in cols]
        vc = [v_ref[0, pl.ds(r0, c), cs] for cs in cols]
        b = []
        for cs in cols:
            lf = lf_scr[pl.ds(r0, c), cs]
            l1 = lf.astype(BF16)
            r1 = lf - l1.astype(F32)
            l2 = r1.astype(BF16)
            l3 = (r1 - l2.astype(F32)).astype(BF16)
            sums = jnp.dot(cum_ref[...], jnp.concatenate([l1, l2, l3], axis=1), preferred_element_type=F32)
            b.append(sums[:, :dk] + sums[:, dk:2 * dk] + sums[:, 2 * dk:])
        st = [st_scr[hd] for hd in heads]
        if need_o:
            o = []
            for hd in heads:
                oi = lax.dot_general((qc[hd] * jnp.exp(b[hd])).astype(BF16), st[hd].astype(BF16), nt,
                                     preferred_element_type=F32)
                o.append(oi + jnp.sum(qc[hd] * kc[hd], axis=-1, keepdims=True) * vc[hd])
            a = [jnp.zeros((c, c), F32) for _ in heads]
            for li in range(nlev):
                for hd in heads:
                    e = jnp.exp((b[hd] - _ref_rows(b[hd], 1 << li, reverse)) * up_ref[li])
                    qd = (qc[hd] * e).astype(BF16)
                    kd = (kc[hd] * e).astype(BF16)
                    a[hd] = a[hd] + mask_ref[li] * lax.dot_general(qd, kd, nt, preferred_element_type=F32)
            for hd in heads:
                o_ref[0, pl.ds(r0, c), cols[hd]] = o[hd] + jnp.dot(a[hd].astype(BF16), vc[hd].astype(BF16),
                                                                    preferred_element_type=F32)
        for hd in heads:
            b_end = b[hd][0:1, :] if reverse else b[hd][c - 1:c, :]
            kdec = kc[hd] * jnp.exp(b_end - b[hd])
            upd = lax.dot_general(vc[hd].astype(BF16), kdec.astype(BF16), tn, preferred_element_type=F32)
            st_scr[hd] = st[hd] * jnp.exp(b_end) + upd
        return carry

    lax.fori_loop(0, nch, chunk, 0)

    @pl.when(s_idx == pl.num_programs(1) - 1)
    def _():
        sout_ref[0] = st_scr[...]


def gla_scan(p, lb3, s0, direction, q_col, f_col, v_col, need_o=True):
    bx, lx, _ = p.shape
    hh = HG_HEADS
    dk = HG_DK
    f = hh * dk
    ts = _tile(lx, GLA_SEQ_BLOCK)
    ns = lx // ts
    rev = direction == 1
    sblk = (lambda s: ns - 1 - s) if rev else (lambda s: s)
    cum, up, mask = _gla_consts(GLA_CHUNK, rev)
    const = lambda a: pl.BlockSpec(a.shape, lambda b, s: (0,) * a.ndim)
    o_shape = (jax.ShapeDtypeStruct((bx, lx, f), F32),) if need_o else ()
    o_spec = (pl.BlockSpec((1, ts, f), lambda b, s: (b, sblk(s), 0)),) if need_o else ()
    outs = pl.pallas_call(
        functools.partial(_gla_body, reverse=rev, need_o=need_o),
        out_shape=o_shape + (jax.ShapeDtypeStruct((bx, hh, dk, dk), F32),),
        grid=(bx, ns),
        in_specs=[pl.BlockSpec((1, ts, f), lambda b, s: (b, sblk(s), q_col)),
                  pl.BlockSpec((1, ts, f), lambda b, s: (b, sblk(s), f_col)),
                  pl.BlockSpec((1, ts, f), lambda b, s: (b, sblk(s), v_col)),
                  pl.BlockSpec((1, 1, f), lambda b, s: (direction, 0, 0)),
                  pl.BlockSpec((1, hh, dk, dk), lambda b, s: (b, 0, 0, 0)),
                  const(cum), const(up), const(mask)],
        out_specs=o_spec + (pl.BlockSpec((1, hh, dk, dk), lambda b, s: (b, 0, 0, 0)),),
        scratch_shapes=[pltpu.VMEM((hh, dk, dk), F32), pltpu.VMEM((ts, f), F32),
                        pltpu.VMEM((ts, f), F32), pltpu.VMEM((ts, f), F32)],
        compiler_params=_cparams(("parallel", "arbitrary")),
        name="gla_scan_bwd" if rev else "gla_scan_fwd",
    )(p, p, p, lb3, s0, cum, up, mask)
    return outs if need_o else (None, outs[0])


def _hg_readout_body(of_ref, ob_ref, g_ref, gn_ref, w_ref, h_ref, g1_ref, o_ref):
    o = of_ref[0] + ob_ref[0]
    gn = gn_ref[...]
    pieces = []
    for hh in range(HG_HEADS):
        oh = o[:, hh * HG_DK:(hh + 1) * HG_DK]
        y = oh * lax.rsqrt(jnp.mean(oh * oh, axis=-1, keepdims=True) + EPS)
        pieces.append(y * gn)
    on = jnp.concatenate(pieces, axis=1)
    g = g_ref[0]
    on = on * (g * _sigmoid(g))
    y = jnp.dot(on.astype(BF16), w_ref[...], preferred_element_type=F32)
    o_ref[0] = h_ref[0] + g1_ref[0] * y


def hgrn_readout_residual(o_f, o_b, p, g_col, onorm_row, w_bf, h, g1):
    bx, lx, d = h.shape
    f = o_f.shape[2]
    tm = _tile(lx, ROW_TILE)
    return pl.pallas_call(
        _hg_readout_body,
        out_shape=jax.ShapeDtypeStruct((bx, lx, d), F32),
        grid=(bx, lx // tm),
        in_specs=[pl.BlockSpec((1, tm, f), lambda b, i: (b, i, 0)),
                  pl.BlockSpec((1, tm, f), lambda b, i: (b, i, 0)),
                  pl.BlockSpec((1, tm, f), lambda b, i: (b, i, g_col)),
                  pl.BlockSpec((1, HG_DK), lambda b, i: (0, 0)),
                  pl.BlockSpec((f, d), lambda b, i: (0, 0)),
                  pl.BlockSpec((1, tm, d), lambda b, i: (b, i, 0)),
                  pl.BlockSpec((1, 1, d), lambda b, i: (b, 0, 0))],
        out_specs=pl.BlockSpec((1, tm, d), lambda b, i: (b, i, 0)),
        compiler_params=_cparams(("parallel", "parallel")),
        name="hgrn_readout_residual",
    )(o_f, o_b, p, onorm_row, w_bf, h, g1)


def _pair_rows():
    rows = [(0, b) for b in range(16)] + [(1, b) for b in range(8)]
    for a in range(2, 8):
        rows += [(a, b) for b in range(8)]
    rows += [(a, 0) for a in range(8, 16)]
    return rows


def _top16_rows(s):
    n = s.shape[0]
    ri = lax.broadcasted_iota(jnp.int32, s.shape, 0)
    rank = jnp.full(s.shape, float(PEER_TOPK), F32)
    vals = []
    for a in range(PEER_TOPK):
        m = jnp.max(s, axis=0, keepdims=True)
        first = jnp.min(jnp.where(s == m, ri, n), axis=0, keepdims=True)
        sel = ri == first
        rank = jnp.where(sel, float(a), rank)
        s = jnp.where(sel, -jnp.inf, s)
        vals.append(m)
    return rank, jnp.concatenate(vals, axis=0)


def _pair_candidates(v0, v1, okf):
    blocks = [v0[0:1] + v1, v0[1:2] + v1[0:8]]
    blocks += [v0[a:a + 1] + v1[0:8] for a in range(2, 8)]
    blocks += [v0[8:16] + v1[0:1]]
    return jnp.where(okf > 0.0, jnp.concatenate(blocks, axis=0), -jnp.inf)


def _gate_arrays(s0, s1, rank0, rank1, v0, v1, cand, selm):
    top = cand[0:1]
    z = jnp.sum(selm * jnp.exp(jnp.where(selm > 0.0, cand - top, 0.0)), axis=0, keepdims=True)
    cnt = [jnp.sum(selm[0:16], axis=0, keepdims=True)]
    cnt += [jnp.sum(selm[16 + 8 * (a - 1):24 + 8 * (a - 1)], axis=0, keepdims=True) for a in range(1, 8)]
    cnt8 = selm[72:80]
    crow = jnp.zeros(rank0.shape, F32)
    for a in range(8):
        crow = jnp.where(rank0 == float(a), cnt[a], crow)
    for a in range(8, 16):
        crow = jnp.where(rank0 == float(a), cnt8[a - 8:a - 7], crow)
    av = jnp.where(rank0 < float(PEER_TOPK), jnp.exp(s0 - v0[0:1]), 0.0)
    bn = jnp.where(rank1 < float(PEER_TOPK), jnp.exp(s1 - v1[0:1]), 0.0) / z
    return rank1, bn, crow, av


def _select_exact(s0, s1, flat, okf):
    rank0, v0 = _top16_rows(s0)
    rank1, v1 = _top16_rows(s1)
    cand = _pair_candidates(v0, v1, okf)
    work = cand
    selm = jnp.zeros(cand.shape, F32)
    for _ in range(PEER_TOPK):
        m = jnp.max(work, axis=0, keepdims=True)
        first = jnp.min(jnp.where(work == m, flat, 1 << 20), axis=0, keepdims=True)
        sel = flat == first
        selm = jnp.where(sel, 1.0, selm)
        work = jnp.where(sel, -jnp.inf, work)
    return _gate_arrays(s0, s1, rank0, rank1, v0, v1, cand, selm)


def _cmp_exchange(xs, i, l, larger_first):
    hi = jnp.maximum(xs[i], xs[l])
    lo = jnp.minimum(xs[i], xs[l])
    xs[i], xs[l] = (hi, lo) if larger_first else (lo, hi)


def _bitonic_merge_desc(xs):
    n = len(xs)
    j = n // 2
    while j >= 1:
        for i in range(n):
            if (i ^ j) > i:
                _cmp_exchange(xs, i, i ^ j, True)
        j //= 2


def _sorted_top16(s):
    n = PEER_TOPK
    xs = [s[g * 8:(g + 1) * 8] for g in range(n)]
    k = 2
    while k <= n:
        j = k // 2
        while j >= 1:
            for i in range(n):
                if (i ^ j) > i:
                    _cmp_exchange(xs, i, i ^ j, (i & k) == 0)
            j //= 2
        k *= 2
    for shift in (4, 6, 7):
        other = [pltpu.roll(x, shift, 0) for x in xs]
        xs = [jnp.maximum(xs[i], other[n - 1 - i]) for i in range(n)]
        _bitonic_merge_desc(xs)
    return jnp.concatenate([x[0:1] for x in xs], axis=0)


def _count_greater(s, v):
    r = [v[a:a + 1] for a in range(PEER_TOPK)]
    c8 = r[7] > s
    c4 = jnp.where(c8, r[11], r[3]) > s
    c2 = jnp.where(c8, jnp.where(c4, r[13], r[9]), jnp.where(c4, r[5], r[1])) > s
    p1 = jnp.where(c8,
                   jnp.where(c4, jnp.where(c2, r[14], r[12]), jnp.where(c2, r[10], r[8])),
                   jnp.where(c4, jnp.where(c2, r[6], r[4]), jnp.where(c2, r[2], r[0])))
    c1 = p1 > s
    g = (jnp.where(c8, 8.0, 0.0) + jnp.where(c4, 4.0, 0.0)) + (jnp.where(c2, 2.0, 0.0) + jnp.where(c1, 1.0, 0.0))
    return jnp.where(r[15] > s, float(PEER_TOPK), g)


def _select_fast(s0, s1, okf):
    w = s0.shape[1]
    s = jnp.concatenate([s0, s1], axis=1)
    v = _sorted_top16(s)
    rank = _count_greater(s, v)
    dup = jnp.max(jnp.where(v[0:15] == v[1:16], 1.0, 0.0), axis=0, keepdims=True)
    members = jnp.sum(jnp.where(s >= v[15:16], 1.0, 0.0), axis=0, keepdims=True)
    tie = dup + jnp.where(members != float(PEER_TOPK), 1.0, 0.0)
    v0, rank0, tie0 = v[:, :w], rank[:, :w], tie[:, :w]
    v1, rank1, tie1 = v[:, w:], rank[:, w:], tie[:, w:]
    cand = _pair_candidates(v0, v1, okf)
    work = cand
    m = cand[0:1]
    for _ in range(PEER_TOPK):
        m = jnp.max(work, axis=0, keepdims=True)
        work = jnp.where(work == m, -jnp.inf, work)
    selm = jnp.where(cand >= m, 1.0, 0.0)
    tie2 = jnp.where(jnp.sum(selm, axis=0, keepdims=True) != float(PEER_TOPK), 1.0, 0.0)
    return _gate_arrays(s0, s1, rank0, rank1, v0, v1, cand, selm), tie0 + tie1 + tie2


def _peer_prep_body(x_ref, wq_ref, keys_ref, flat_ref, okf_ref, rank1_ref, bn_ref, crow_ref, av_ref, q_scr):
    q_scr[...] = jnp.dot(wq_ref[...], x_ref[...], preferred_element_type=F32).astype(BF16)
    nk = N_KEYS
    tm = x_ref.shape[1]
    flat = flat_ref[...]
    okf = okf_ref[...]

    def store(h, cs, outs):
        rank1, bn, crow, av = outs
        rank1_ref[h, :, cs] = rank1.astype(BF16)
        bn_ref[h, :, cs] = bn.astype(BF16)
        crow_ref[h, :, cs] = crow
        av_ref[h, :, cs] = av

    def head(h, carry):
        r0 = pl.multiple_of(h * 2 * nk, 2 * nk)
        for c0 in range(0, tm, PEER_PREP_CW):
            cs = slice(c0, c0 + PEER_PREP_CW)
            s0 = jnp.dot(keys_ref[h, 0], q_scr[pl.ds(r0, nk), cs], preferred_element_type=F32)
            s1 = jnp.dot(keys_ref[h, 1], q_scr[pl.ds(r0 + nk, nk), cs], preferred_element_type=F32)
            outs, tie = _select_fast(s0, s1, okf)
            store(h, cs, outs)

            @pl.when(jnp.max(tie) > 0.0)
            def _():
                store(h, cs, _select_exact(s0, s1, flat, okf))
        return carry

    lax.fori_loop(0, PEER_HEADS, head, 0)


def peer_prep(x_t, wq_t, keys_bf):
    d, t = x_t.shape
    tm = _tile(t, PEER_PREP_TOK)
    rows = _pair_rows()
    flat = jnp.asarray([[a * 16 + b] for a, b in rows], jnp.int32)
    okf = jnp.asarray([[1.0 if (a + 1) * (b + 1) <= PEER_TOPK else 0.0] for a, b in rows], F32)
    out = jax.ShapeDtypeStruct((PEER_HEADS, N_KEYS, t), F32)
    out_bf = jax.ShapeDtypeStruct((PEER_HEADS, N_KEYS, t), BF16)
    ospec = pl.BlockSpec((PEER_HEADS, N_KEYS, tm), lambda i: (0, 0, i))
    return pl.pallas_call(
        _peer_prep_body,
        out_shape=(out_bf, out_bf, out, out),
        grid=(t // tm,),
        in_specs=[pl.BlockSpec((d, tm), lambda i: (0, i)),
                  pl.BlockSpec(wq_t.shape, lambda i: (0, 0)),
                  pl.BlockSpec(keys_bf.shape, lambda i: (0, 0, 0, 0)),
                  pl.BlockSpec(flat.shape, lambda i: (0, 0)),
                  pl.BlockSpec(okf.shape, lambda i: (0, 0))],
        out_specs=(ospec, ospec, ospec, ospec),
        scratch_shapes=[pltpu.VMEM((wq_t.shape[0], tm), BF16)],
        compiler_params=_cparams(("parallel",)),
        name="peer_prep",
    )(x_t, wq_t, keys_bf, flat, okf)


def _row_tile_bf16(row):
    tile = jnp.broadcast_to(row, (BF16_SUBLANES, row.shape[1])).astype(BF16)
    return jnp.concatenate([tile] * (N_KEYS // BF16_SUBLANES), axis=0)


def _peer_main_body(x_ref, u_ref, v_ref, vlast_ref, rank1_ref, bn_ref, crow_ref, av_ref, h_ref, g2_ref, fg_ref,
                    o_ref, acc_scr, pt_scr, *, final_norm):
    j = pl.program_id(1)
    nj = pl.num_programs(1)
    tm = x_ref.shape[1]
    nrow = u_ref.shape[0] // N_KEYS
    sub = PEER_SUB_ROWS * N_KEYS
    nsb = nrow // PEER_SUB_ROWS
    x = x_ref[...]
    rd = (j + 1) % 2
    wr = j % 2
    row0 = j * nrow

    @pl.when(j == 0)
    def _():
        acc_scr[...] = jnp.zeros(acc_scr.shape, F32)
        pt_scr[1] = jnp.zeros(pt_scr.shape[1:], BF16)

    def pre_act(sb):
        return jnp.dot(u_ref[sb * sub:(sb + 1) * sub, :], x, preferred_element_type=F32)

    pt_prev = pt_scr[rd]
    dn = v_ref.shape[1] // nsb
    acts, pvs = [], []
    for sb in range(nsb):
        acts.append(pre_act(sb))
        pvs.append(lax.dot_general(pt_prev, v_ref[:, sb * dn:(sb + 1) * dn], (((0,), (0,)), ((), ())),
                                   preferred_element_type=F32))
    for sb in range(nsb):
        act = acts[sb]
        for rr in range(PEER_SUB_ROWS):
            e1 = row0 + sb * PEER_SUB_ROWS + rr
            for c0 in range(0, tm, PEER_CW):
                cs = slice(c0, c0 + PEER_CW)
                w = None
                for hd in range(PEER_HEADS):
                    cr = _row_tile_bf16(crow_ref[hd, pl.ds(e1, 1), cs])
                    ar = _row_tile_bf16(av_ref[hd, pl.ds(e1, 1), cs])
                    t = jnp.where(rank1_ref[hd, :, cs] < cr, bn_ref[hd, :, cs], jnp.zeros((), BF16)) * ar
                    w = t if w is None else w + t
                a = act[rr * N_KEYS:(rr + 1) * N_KEYS, cs]
                gelu = 0.5 * a * (1.0 + lax.erf(a * (2.0 ** -0.5)))
                r0 = (sb * PEER_SUB_ROWS + rr) * N_KEYS
                pt_scr[wr, r0:r0 + N_KEYS, cs] = gelu.astype(BF16) * w
    for sb in range(nsb):
        acc_scr[:, sb * dn:(sb + 1) * dn] += pvs[sb]

    @pl.when(j == nj - 1)
    def _():
        last = lax.dot_general(pt_scr[wr], vlast_ref[...], (((0,), (0,)), ((), ())), preferred_element_type=F32)
        hn = h_ref[...] + g2_ref[0] * (acc_scr[...] + last)
        if final_norm:
            hn = hn * lax.rsqrt(jnp.mean(hn * hn, axis=-1, keepdims=True) + EPS) * fg_ref[...]
        o_ref[...] = hn


def peer_main(x_t, u_bf, v_bf, layer, prep, h2, g2, tokens_per_batch, final_g, final_norm):
    d, t = x_t.shape
    e = u_bf.shape[1]
    tm = _tile(tokens_per_batch, PEER_TOK)
    te = PEER_ROWS * N_KEYS
    assert e % te == 0 and tm % PEER_CW == 0
    tpb = tokens_per_batch // tm
    nj = e // te
    assert PEER_ROWS % PEER_SUB_ROWS == 0 and PEER_ROWS // PEER_SUB_ROWS >= 2
    pspec = pl.BlockSpec((PEER_HEADS, N_KEYS, tm), lambda i, j: (0, 0, i))
    return pl.pallas_call(
        functools.partial(_peer_main_body, final_norm=final_norm),
        out_shape=jax.ShapeDtypeStruct((t, d), F32),
        grid=(t // tm, nj),
        in_specs=[pl.BlockSpec((d, tm), lambda i, j: (0, i)),
                  pl.BlockSpec((None, te, d), lambda i, j: (layer, j, 0)),
                  pl.BlockSpec((None, te, d), lambda i, j: (layer, jnp.maximum(j - 1, 0), 0)),
                  pl.BlockSpec((None, te, d), lambda i, j: (layer, nj - 1, 0)),
                  pspec, pspec, pspec, pspec,
                  pl.BlockSpec((tm, d), lambda i, j: (i, 0)),
                  pl.BlockSpec((1, 1, d), lambda i, j: (i // tpb, 0, 0)),
                  pl.BlockSpec((1, d), lambda i, j: (0, 0))],
        out_specs=pl.BlockSpec((tm, d), lambda i, j: (i, 0)),
        scratch_shapes=[pltpu.VMEM((tm, d), F32), pltpu.VMEM((2, te, tm), BF16)],
        compiler_params=_cparams(("parallel", "arbitrary")),
        name="peer_dense",
    )(x_t, u_bf, v_bf, v_bf, *prep, h2, g2, final_g)


def peer_residual(h, norm_g, sh, sc, g2, wq_t, keys_bf, u_bf, v_bf, layer, final_g, final_norm):
    bx, lx, d = h.shape
    a_t = norm_mod_t(h, norm_g, sh, sc)
    prep = peer_prep(a_t, wq_t, keys_bf)
    out = peer_main(a_t, u_bf, v_bf, layer, prep, h.reshape(bx * lx, d), g2, lx, final_g, final_norm)
    return out.reshape(bx, lx, d)


def kernel(x, c, ctx, c_ctx, ada_w, ada_b, norm1_g, norm2_g, final_g, ab_w_in, ab_w_out, hy_conv_w, hy_conv_b,
           hy_filt_w1, hy_filt_b1, hy_filt_w2, hy_filt_b2, hy_filt_w3, hy_filt_freq, hy_skip, attn_sink,
           hg_w_in, hg_w_out, hg_lb_logits, hg_onorm_g, peer_wq, peer_keys, peer_u, peer_v):
    bsz, seq, d = x.shape
    depth = ada_w.shape[0]
    assert depth == 2
    lb_p = jax.nn.softmax(hg_lb_logits.astype(F32), axis=0)
    lb_all = jnp.cumsum(lb_p, axis=0) - lb_p[0:1]

    c16 = jnp.concatenate([c, c_ctx[None], jnp.zeros((16 - bsz - 1, d), F32)], axis=0)
    final_row = final_g[None]
    u_bf = peer_u.astype(BF16)
    v_bf = peer_v.astype(BF16)
    h_lat, h_ctx = x, ctx

    for l in range(depth):
        need_ctx = l < depth - 1
        mod = ada_mod(c16, ada_w, ada_b[:, None, :], l)
        lat = [mod[:bsz, i * d:(i + 1) * d][:, None, :] for i in range(6)]
        cx = [jnp.broadcast_to(mod[bsz:bsz + 1, i * d:(i + 1) * d][:, None, :], (bsz, 1, d)) for i in range(6)]
        n1 = norm1_g[l][None]
        n2 = norm2_g[l][None]
        wq_t = peer_wq[l].T.astype(BF16)
        keys_bf = peer_keys[l].astype(BF16)
        j = l // 2
        if l % 2 == 0:
            w_in = ab_w_in[j].astype(BF16)
            w_out = ab_w_out[j].astype(BF16)
            p_lat = norm_mod_matmul(h_lat, n1, lat[0], lat[1], w_in)
            p_ctx = norm_mod_matmul(h_ctx, n1, cx[0], cx[1], w_in)
            hy = (hy_conv_w[j], hy_conv_b[j][None], hy_filt_w1[j], hy_filt_b1[j], hy_filt_w2[j], hy_filt_b2[j],
                  hy_filt_w3[j], hy_filt_freq[j], hy_skip[j])
            cos, sin = _rope_tables(seq)
            qr, kr = rope_qk(p_lat, cos, sin, 3, 16)
            sink = attn_sink[j]
            at_lat = window_attention(qr, kr, p_lat, p_ctx, sink, 16, 17)
            hy_lat = hyena_mixer(p_lat, *hy)
            h_lat = out_proj_residual(hy_lat, at_lat, w_out, h_lat, lat[2])
            if need_ctx:
                at_ctx = context_attention(p_ctx, sink, 3, 16, 17)
                hy_ctx = hyena_mixer(p_ctx, *hy)
                h_ctx = out_proj_residual(hy_ctx, at_ctx, w_out, h_ctx, cx[2])
        else:
            w_in = hg_w_in[j].astype(BF16)
            w_out = hg_w_out[j].astype(BF16)
            lb3 = lb_all[l].astype(F32)[:, None, :]
            p_lat = norm_mod_matmul(h_lat, n1, lat[0], lat[1], w_in)
            p_ctx = norm_mod_matmul(h_ctx, n1, cx[0], cx[1], w_in)
            s0 = jnp.zeros((bsz, HG_HEADS, HG_DK, HG_DK), F32)
            o_cf, s_cf = gla_scan(p_ctx, lb3, s0, 0, 0, 1, 3, need_o=need_ctx)
            o_cb, s_cb = gla_scan(p_ctx, lb3, s0, 1, 0, 2, 3, need_o=need_ctx)
            o_lf, _ = gla_scan(p_lat, lb3, s_cf, 0, 0, 1, 3)
            o_lb, _ = gla_scan(p_lat, lb3, s_cb, 1, 0, 2, 3)
            onorm = hg_onorm_g[j].astype(F32)[None]
            h_lat = hgrn_readout_residual(o_lf, o_lb, p_lat, 4, onorm, w_out, h_lat, lat[2])
            if need_ctx:
                h_ctx = hgrn_readout_residual(o_cf, o_cb, p_ctx, 4, onorm, w_out, h_ctx, cx[2])
        last = l == depth - 1
        h_lat = peer_residual(h_lat, n2, lat[3], lat[4], lat[5], wq_t, keys_bf, u_bf, v_bf, l, final_row, last)
        if need_ctx:
            h_ctx = peer_residual(h_ctx, n2, cx[3], cx[4], cx[5], wq_t, keys_bf, u_bf, v_bf, l, final_row, False)
    return h_lat
```

```python
import functools
import math

import jax
import jax.numpy as jnp
import numpy as np
from jax import lax
from jax.experimental import pallas as pl
from jax.experimental.pallas import tpu as pltpu

F32 = jnp.float32
BF16 = jnp.bfloat16
EPS = 1e-6

HY_C = 512
HY_EMB = 33
HY_EMB_PAD = 40
HY_FILT_W = 64
HY_DECAY_SHORT_PCT = 0.3
HY_DECAY_LONG_PCT = 1.5
HY_TARGET = 1e-2
HEAD_DIM = 64
N_Q_HEADS = 8
N_KV_HEADS = 2
GQA_GROUP = 4
WINDOW = 128
ATTN_BLK = 128
GRID_W = 64
ROPE_BASE = 10000.0
HG_HEADS = 8
HG_DK = 128
PEER_HEADS = 8
PEER_TOPK = 16
N_KEYS = 128

LANES = 128
BF16_SUBLANES = 16
VMEM_LIMIT_BYTES = 56 * 1024 * 1024
ROW_TILE = 512
PROJ_ROW_TILE = 1024
GLA_CHUNK = 128
GLA_SEQ_BLOCK = 256
PEER_PREP_TOK = 256
PEER_PREP_CW = 256
PEER_TOK = 512
PEER_ROWS = 8
PEER_SUB_ROWS = 2
PEER_CW = 256
NEG_BIG = -1e30


def _cparams(sem):
    return pltpu.CompilerParams(dimension_semantics=sem, vmem_limit_bytes=VMEM_LIMIT_BYTES)


def _sigmoid(x):
    return 1.0 / (1.0 + jnp.exp(-x))


def _tile(n, t):
    t = min(n, t)
    assert n % t == 0, (n, t)
    return t


def _ada_body(c_ref, w_ref, b_ref, o_ref):
    c = c_ref[...]
    s = c * _sigmoid(c)
    o_ref[...] = jnp.dot(s.astype(BF16), w_ref[...].astype(BF16), preferred_element_type=F32) + b_ref[...]


def ada_mod(c16, w_all, b_all, layer):
    _, d, n = w_all.shape
    tn = _tile(n, 1536)
    return pl.pallas_call(
        _ada_body,
        out_shape=jax.ShapeDtypeStruct((c16.shape[0], n), F32),
        grid=(n // tn,),
        in_specs=[pl.BlockSpec(c16.shape, lambda j: (0, 0)),
                  pl.BlockSpec((None, d, tn), lambda j: (layer, 0, j)),
                  pl.BlockSpec((None, 1, tn), lambda j: (layer, 0, j))],
        out_specs=pl.BlockSpec((c16.shape[0], tn), lambda j: (0, j)),
        compiler_params=_cparams(("arbitrary",)),
        name="ada_mod",
    )(c16, w_all, b_all)


def _norm_mod(x, g, sh, sc):
    y = x * lax.rsqrt(jnp.mean(x * x, axis=-1, keepdims=True) + EPS)
    return (y * g) * (1.0 + sc) + sh


def _nm_matmul_body(h_ref, g_ref, sh_ref, sc_ref, w_ref, p_ref, a_scr):
    @pl.when(pl.program_id(2) == 0)
    def _():
        a_scr[...] = _norm_mod(h_ref[0], g_ref[...], sh_ref[0], sc_ref[0]).astype(BF16)

    p_ref[0] = jnp.dot(a_scr[...], w_ref[...], preferred_element_type=F32)


def norm_mod_matmul(h, g, sh, sc, w_bf):
    bx, lx, d = h.shape
    n = w_bf.shape[1]
    tm = _tile(lx, PROJ_ROW_TILE)
    tn = _tile(n, 1280 if n % 1280 == 0 else (1152 if n % 1152 == 0 else 1024))
    return pl.pallas_call(
        _nm_matmul_body,
        out_shape=jax.ShapeDtypeStruct((bx, lx, n), F32),
        grid=(bx, lx // tm, n // tn),
        in_specs=[pl.BlockSpec((1, tm, d), lambda b, i, j: (b, i, 0)),
                  pl.BlockSpec((1, d), lambda b, i, j: (0, 0)),
                  pl.BlockSpec((1, 1, d), lambda b, i, j: (b, 0, 0)),
                  pl.BlockSpec((1, 1, d), lambda b, i, j: (b, 0, 0)),
                  pl.BlockSpec((d, tn), lambda b, i, j: (0, j))],
        out_specs=pl.BlockSpec((1, tm, tn), lambda b, i, j: (b, i, j)),
        scratch_shapes=[pltpu.VMEM((tm, d), BF16)],
        compiler_params=_cparams(("parallel", "parallel", "arbitrary")),
        name="norm_mod_matmul",
    )(h, g, sh, sc, w_bf)


def _nm_only_body(h_ref, g_ref, sh_ref, sc_ref, a_ref):
    a_ref[...] = _norm_mod(h_ref[0], g_ref[...], sh_ref[0], sc_ref[0]).T.astype(BF16)


def norm_mod_t(h, g, sh, sc):
    bx, lx, d = h.shape
    tm = _tile(lx, ROW_TILE)
    nt = lx // tm
    return pl.pallas_call(
        _nm_only_body,
        out_shape=jax.ShapeDtypeStruct((d, bx * lx), BF16),
        grid=(bx, nt),
        in_specs=[pl.BlockSpec((1, tm, d), lambda b, i: (b, i, 0)),
                  pl.BlockSpec((1, d), lambda b, i: (0, 0)),
                  pl.BlockSpec((1, 1, d), lambda b, i: (b, 0, 0)),
                  pl.BlockSpec((1, 1, d), lambda b, i: (b, 0, 0))],
        out_specs=pl.BlockSpec((d, tm), lambda b, i: (0, b * nt + i)),
        compiler_params=_cparams(("parallel", "parallel")),
        name="norm_mod_t",
    )(h, g, sh, sc)


def _mm_body(a_ref, b_ref, o_ref):
    o_ref[...] = jnp.dot(a_ref[...], b_ref[...], preferred_element_type=F32)


def matmul_bf16(a, b):
    m, k = a.shape
    n = b.shape[1]
    tm = _tile(m, 1024)
    tn = _tile(n, 1024)
    return pl.pallas_call(
        _mm_body,
        out_shape=jax.ShapeDtypeStruct((m, n), F32),
        grid=(m // tm, n // tn),
        in_specs=[pl.BlockSpec((tm, k), lambda i, j: (i, 0)),
                  pl.BlockSpec((k, tn), lambda i, j: (0, j))],
        out_specs=pl.BlockSpec((tm, tn), lambda i, j: (i, j)),
        compiler_params=_cparams(("parallel", "parallel")),
        name="matmul_bf16",
    )(a, b)


def _filter_body(z_ref, w1_ref, b1_ref, w2_ref, b2_ref, w3_ref, fr_ref, dec_ref, o_ref):
    hi = lax.Precision.HIGHEST
    fr = fr_ref[...]
    hdn = jnp.sin(fr * (jnp.dot(z_ref[...], w1_ref[...], precision=hi, preferred_element_type=F32) + b1_ref[...]))
    hdn = jnp.sin(fr * (jnp.dot(hdn, w2_ref[...], precision=hi, preferred_element_type=F32) + b2_ref[...]))
    h = jnp.dot(hdn, w3_ref[...], precision=hi, preferred_element_type=F32)
    dec = dec_ref[...]
    c = dec.shape[1]
    h0 = h[:, :c] * dec
    h1 = h[:, c:] * dec
    nrm = jnp.sum(jnp.abs(h0) + jnp.abs(h1), axis=0, keepdims=True)
    inv = 1.0 / nrm
    ri = lax.broadcasted_iota(jnp.int32, h1.shape, 0)
    o_ref[:, :c] = (h0 * inv).astype(BF16)
    o_ref[:, c:] = jnp.where(ri == 0, 0.0, h1 * inv).astype(BF16)


def hyena_filters(zfeat, w1p, b1, w2, b2, w3, freq, decay):
    l = zfeat.shape[0]
    c = decay.shape[1]
    n_order = w3.shape[1] // (2 * c)
    full = lambda shape: pl.BlockSpec(shape, lambda o: (0,) * len(shape))
    return pl.pallas_call(
        _filter_body,
        out_shape=jax.ShapeDtypeStruct((l, n_order * 2 * c), BF16),
        grid=(n_order,),
        in_specs=[full(zfeat.shape), full(w1p.shape), full(b1.shape), full(w2.shape), full(b2.shape),
                  pl.BlockSpec((w3.shape[0], 2 * c), lambda o: (0, o)),
                  full(freq.shape), full(decay.shape)],
        out_specs=pl.BlockSpec((l, 2 * c), lambda o: (0, o)),
        compiler_params=_cparams(("arbitrary",)),
        name="hyena_filters",
    )(zfeat, w1p, b1, w2, b2, w3, freq, decay)


def _short_conv_body(u_ref, w_ref, b_ref, o_ref):
    u = u_ref[0]
    l = u.shape[0]
    ri = lax.broadcasted_iota(jnp.int32, u.shape, 0)
    prev = jnp.where(ri == 0, 0.0, pltpu.roll(u, 1, 0))
    nxt = jnp.where(ri == l - 1, 0.0, pltpu.roll(u, l - 1, 0))
    w = w_ref[...]
    o_ref[0] = prev * w[0:1] + u * w[1:2] + nxt * w[2:3] + b_ref[...]


def short_conv(p, conv_w, conv_b):
    bx, lx, _ = p.shape
    c3 = conv_w.shape[1]
    tc = _tile(c3, 512)
    return pl.pallas_call(
        _short_conv_body,
        out_shape=jax.ShapeDtypeStruct((bx, lx, c3), F32),
        grid=(bx, c3 // tc),
        in_specs=[pl.BlockSpec((1, lx, tc), lambda b, j: (b, 0, j)),
                  pl.BlockSpec((3, tc), lambda b, j: (0, j)),
                  pl.BlockSpec((1, tc), lambda b, j: (0, j))],
        out_specs=pl.BlockSpec((1, lx, tc), lambda b, j: (b, 0, j)),
        compiler_params=_cparams(("parallel", "parallel")),
        name="hyena_short_conv",
    )(p, conv_w, conv_b)


def _hy_fwd_body(z_ref, f_ref, h0_ref, h1_ref, y_ref):
    z = z_ref[0].astype(BF16)
    kt = f_ref.shape[1]
    zre = jnp.dot(f_ref[0], z, preferred_element_type=F32)
    zim = jnp.dot(f_ref[1], z, preferred_element_type=F32)
    k0 = (pl.program_id(0) * kt + lax.broadcasted_iota(jnp.int32, zre.shape, 0)) == 0
    hre = h0_ref[0] + h1_ref[0]
    him = jnp.where(k0, h0_ref[1] + h1_ref[1], h0_ref[1] - h1_ref[1])
    yre = jnp.where(k0, zre * hre, zre * hre - zim * him)
    yim = jnp.where(k0, zim * him, zre * him + zim * hre)
    y_ref[0, 0] = yre.astype(BF16)
    y_ref[0, 1] = yim.astype(BF16)


def hyena_fwd(z_src, z_col, f3, fh3, order, c):
    bx, lx, _ = z_src.shape
    kt = _tile(lx, 512)
    return pl.pallas_call(
        _hy_fwd_body,
        out_shape=jax.ShapeDtypeStruct((bx, 2, lx, c), BF16),
        grid=(lx // kt, bx),
        in_specs=[pl.BlockSpec((1, lx, c), lambda k, b: (b, 0, z_col)),
                  pl.BlockSpec((2, kt, lx), lambda k, b: (0, k, 0)),
                  pl.BlockSpec((2, kt, c), lambda k, b: (0, k, 2 * order)),
                  pl.BlockSpec((2, kt, c), lambda k, b: (0, k, 2 * order + 1))],
        out_specs=pl.BlockSpec((1, 2, kt, c), lambda k, b: (b, 0, k, 0)),
        compiler_params=_cparams(("parallel", "parallel")),
        name="hyena_dft_fwd",
    )(z_src, f3, fh3, fh3)


def _hy_inv_body(y_ref, fi_ref, gate_ref, z_ref, skip_ref, o_ref):
    conv = jnp.dot(fi_ref[...], y_ref[0], preferred_element_type=F32)
    o_ref[0] = gate_ref[0] * (conv + skip_ref[...] * z_ref[0])


def hyena_inv(yf, finv, gate_src, gate_col, z_src, z_col, skip_row):
    bx, l2, c = yf.shape
    lx = l2 // 2
    tl = _tile(lx, 512)
    return pl.pallas_call(
        _hy_inv_body,
        out_shape=jax.ShapeDtypeStruct((bx, lx, c), F32),
        grid=(lx // tl, bx),
        in_specs=[pl.BlockSpec((1, l2, c), lambda t, b: (b, 0, 0)),
                  pl.BlockSpec((tl, l2), lambda t, b: (t, 0)),
                  pl.BlockSpec((1, tl, c), lambda t, b: (b, t, gate_col)),
                  pl.BlockSpec((1, tl, c), lambda t, b: (b, t, z_col)),
                  pl.BlockSpec((1, c), lambda t, b: (0, 0))],
        out_specs=pl.BlockSpec((1, tl, c), lambda t, b: (b, t, 0)),
        compiler_params=_cparams(("parallel", "parallel")),
        name="hyena_dft_inv",
    )(yf, finv, gate_src, z_src, skip_row)


def _dft_mats(l):
    n = 2 * l
    k = jnp.arange(l, dtype=jnp.int32)[:, None]
    s = jnp.arange(l, dtype=jnp.int32)[None, :]

    def table(rows):
        ang = ((rows[:, None] * s) % n).astype(F32) * (2.0 * math.pi / n)
        return jnp.cos(ang), jnp.sin(ang)

    r = 1
    while r * r < l:
        r *= 2
    ch, sh = table(jnp.arange(l // r, dtype=jnp.int32) * r)
    cl, sl = table(jnp.arange(r, dtype=jnp.int32))
    cosm = (ch[:, None, :] * cl[None, :, :] - sh[:, None, :] * sl[None, :, :]).reshape(l, l)
    sinm = -(sh[:, None, :] * cl[None, :, :] + ch[:, None, :] * sl[None, :, :]).reshape(l, l)
    nyq = jnp.where(s % 2 == 0, 1.0, -1.0).astype(F32)
    imag = jnp.where(k == 0, nyq, sinm)
    fwd = jnp.concatenate([cosm, imag], axis=0)
    scale = jnp.where(jnp.arange(l) == 0, 1.0 / n, 2.0 / n).astype(F32)
    inv = jnp.concatenate([cosm.T * scale[None, :], imag.T * scale[None, :]], axis=1)
    return fwd.astype(BF16), inv.astype(BF16)


def _filter_features(l):
    bands = (HY_EMB - 1) // 2
    t = jnp.linspace(0.0, 1.0, l, dtype=F32)[:, None]
    w = 2.0 * math.pi * jnp.arange(l, dtype=F32)[:, None] / l
    f = jnp.linspace(1e-4, bands - 1, bands, dtype=F32)[None, :]
    z = jnp.concatenate([t, jnp.cos(f * w), -jnp.sin(f * w)], axis=-1)
    z = jnp.pad(z, ((0, 0), (0, HY_EMB_PAD - HY_EMB)))
    max_decay = math.log(HY_TARGET) / HY_DECAY_SHORT_PCT
    min_decay = math.log(HY_TARGET) / HY_DECAY_LONG_PCT
    deltas = jnp.abs(jnp.linspace(min_decay, max_decay, HY_C, dtype=F32))
    return z, jnp.exp(-t * deltas)


def hyena_mixer(p, conv_w, conv_b, fw1, fb1, fw2, fb2, fw3, ffreq, d_skip):
    bx, lx, _ = p.shape
    c = HY_C
    zfeat, decay = _filter_features(lx)
    w1p = jnp.pad(fw1, ((0, HY_EMB_PAD - HY_EMB), (0, 0)))
    filt = hyena_filters(zfeat, w1p, fb1[None], fw2, fb2[None], fw3, ffreq[None], decay)
    fwd, inv = _dft_mats(lx)
    fh3 = matmul_bf16(fwd, filt).reshape(2, lx, filt.shape[1])
    f3 = fwd.reshape(2, lx, lx)
    uc = short_conv(p, conv_w, conv_b)
    y0 = hyena_fwd(uc, 0, f3, fh3, 0, c).reshape(bx, 2 * lx, c)
    z1 = hyena_inv(y0, inv, uc, 1, uc, 0, d_skip[0:1])
    y1 = hyena_fwd(z1, 0, f3, fh3, 1, c).reshape(bx, 2 * lx, c)
    return hyena_inv(y1, inv, uc, 2, z1, 0, d_skip[1:2])


def _rope_tables(l):
    rows = l // GRID_W
    r = jnp.broadcast_to(jnp.arange(rows)[:, None], (rows, GRID_W)).reshape(-1).astype(F32)
    col = jnp.broadcast_to(jnp.arange(GRID_W)[None, :], (rows, GRID_W)).reshape(-1).astype(F32)
    nf = HEAD_DIM // 4
    inv = ROPE_BASE ** (-jnp.arange(nf, dtype=F32) / nf)
    ar = r[:, None] * inv
    ac = col[:, None] * inv
    ang = jnp.concatenate([ar, ar, ac, ac], axis=-1)
    sign = jnp.concatenate([-jnp.ones((nf,)), jnp.ones((nf,))] * 2).astype(F32)
    cos = jnp.tile(jnp.cos(ang), (1, N_Q_HEADS))
    sin = jnp.tile(jnp.sin(ang) * sign[None, :], (1, N_Q_HEADS))
    return cos, sin


def _rope_body(q_ref, k_ref, cos_ref, sin_ref, qo_ref, ko_ref):
    def rot(x, cos, sin):
        n = x.shape[1]
        lane = lax.broadcasted_iota(jnp.int32, x.shape, 1)
        first = (lane % 32) < 16
        partner = jnp.where(first, pltpu.roll(x, n - 16, 1), pltpu.roll(x, 16, 1))
        return x * cos + partner * sin

    nk = k_ref.shape[2]
    qo_ref[0] = (rot(q_ref[0], cos_ref[...], sin_ref[...]) * (HEAD_DIM ** -0.5)).astype(BF16)
    ko_ref[0] = rot(k_ref[0], cos_ref[:, :nk], sin_ref[:, :nk]).astype(BF16)


def rope_qk(p, cos, sin, q_col, k_col):
    bx, lx, _ = p.shape
    nq = N_Q_HEADS * HEAD_DIM
    nk = N_KV_HEADS * HEAD_DIM
    tr = _tile(lx, ROW_TILE)
    return pl.pallas_call(
        _rope_body,
        out_shape=(jax.ShapeDtypeStruct((bx, lx, nq), BF16), jax.ShapeDtypeStruct((bx, lx, nk), BF16)),
        grid=(bx, lx // tr),
        in_specs=[pl.BlockSpec((1, tr, nq), lambda b, i: (b, i, q_col)),
                  pl.BlockSpec((1, tr, nk), lambda b, i: (b, i, k_col)),
                  pl.BlockSpec((tr, nq), lambda b, i: (i, 0)),
                  pl.BlockSpec((tr, nq), lambda b, i: (i, 0))],
        out_specs=(pl.BlockSpec((1, tr, nq), lambda b, i: (b, i, 0)),
                   pl.BlockSpec((1, tr, nk), lambda b, i: (b, i, 0))),
        compiler_params=_cparams(("parallel", "parallel")),
        name="rope_qk",
    )(p, p, cos, sin)


def _attn_heads(q, sink_ref, parts_for_head, o_ref):
    t = q.shape[0]
    nt = (((1,), (1,)), ((), ()))
    heads = range(N_KV_HEADS)
    q4, sink, parts = [], [], []
    for hk in heads:
        q4.append(jnp.concatenate(
            [q[:, (hk * GQA_GROUP + g) * HEAD_DIM:(hk * GQA_GROUP + g + 1) * HEAD_DIM] for g in range(GQA_GROUP)],
            axis=0))
        sink.append(jnp.concatenate(
            [jnp.full((t, 1), sink_ref[hk * GQA_GROUP + g], F32) for g in range(GQA_GROUP)], axis=0))
        parts.append(parts_for_head(hk))
    ss = []
    for hk in heads:
        row = []
        for k, _, mask in parts[hk]:
            sc = lax.dot_general(q4[hk], k, nt, preferred_element_type=F32)
            row.append(sc if mask is None else jnp.where(mask, sc, NEG_BIG))
        ss.append(row)
    m = []
    for hk in heads:
        mh = sink[hk]
        for sc in ss[hk]:
            mh = jnp.maximum(mh, jnp.max(sc, axis=-1, keepdims=True))
        m.append(mh)
    es = [[jnp.exp(sc - m[hk]) for sc in ss[hk]] for hk in heads]
    den = []
    for hk in heads:
        dh = jnp.exp(sink[hk] - m[hk])
        for e in es[hk]:
            dh = dh + jnp.sum(e, axis=-1, keepdims=True)
        den.append(dh)
    outs = []
    for hk in heads:
        acc = None
        for e, (_, v, _) in zip(es[hk], parts[hk]):
            o = jnp.dot(e.astype(BF16), v, preferred_element_type=F32)
            acc = o if acc is None else acc + o
        o4 = acc / den[hk]
        outs.extend([o4[g * t:(g + 1) * t] for g in range(GQA_GROUP)])
    o_ref[0] = jnp.concatenate(outs, axis=1)


def _win_attn_body(sink_ref, q_ref, k0_ref, k1_ref, k2_ref, v0_ref, v1_ref, v2_ref, kc_ref, vc_ref, o_ref, *, seq_len):
    n = pl.program_id(1)
    blk = q_ref.shape[1]
    kb = jnp.concatenate([k0_ref[0], k1_ref[0], k2_ref[0]], axis=0)
    vb = jnp.concatenate([v0_ref[0], v1_ref[0], v2_ref[0]], axis=0).astype(BF16)
    kc = kc_ref[0].astype(BF16)
    vc = vc_ref[0].astype(BF16)
    qi = lax.broadcasted_iota(jnp.int32, (blk, 3 * blk), 0)
    kj = lax.broadcasted_iota(jnp.int32, (blk, 3 * blk), 1)
    kpos = (n - 1) * blk + kj
    diff = qi + blk - kj
    valid = (kpos >= 0) & (kpos < seq_len) & (diff <= WINDOW) & (diff >= -WINDOW)
    valid4 = jnp.concatenate([valid] * GQA_GROUP, axis=0)

    def parts(hk):
        sl = slice(hk * HEAD_DIM, (hk + 1) * HEAD_DIM)
        return [(kb[:, sl], vb[:, sl], valid4), (kc[:, sl], vc[:, sl], None)]

    _attn_heads(q_ref[0], sink_ref, parts, o_ref)


def window_attention(qr, kr, p, p_ctx, sink, k_col, v_col):
    bx, lx, nq = qr.shape
    nk = kr.shape[2]
    lc = p_ctx.shape[1]
    blk = ATTN_BLK
    nb = lx // blk
    lo = lambda b, n: (b, jnp.maximum(n - 1, 0), 0)
    mid = lambda b, n: (b, n, 0)
    hi = lambda b, n: (b, jnp.minimum(n + 1, nb - 1), 0)
    vlo = lambda b, n: (b, jnp.maximum(n - 1, 0), v_col)
    vmid = lambda b, n: (b, n, v_col)
    vhi = lambda b, n: (b, jnp.minimum(n + 1, nb - 1), v_col)
    return pl.pallas_call(
        functools.partial(_win_attn_body, seq_len=lx),
        out_shape=jax.ShapeDtypeStruct((bx, lx, nq), F32),
        grid=(bx, nb),
        in_specs=[pl.BlockSpec(memory_space=pltpu.SMEM),
                  pl.BlockSpec((1, blk, nq), mid),
                  pl.BlockSpec((1, blk, nk), lo), pl.BlockSpec((1, blk, nk), mid), pl.BlockSpec((1, blk, nk), hi),
                  pl.BlockSpec((1, blk, nk), vlo), pl.BlockSpec((1, blk, nk), vmid), pl.BlockSpec((1, blk, nk), vhi),
                  pl.BlockSpec((1, lc, nk), lambda b, n: (b, 0, k_col)),
                  pl.BlockSpec((1, lc, nk), lambda b, n: (b, 0, v_col))],
        out_specs=pl.BlockSpec((1, blk, nq), mid),
        compiler_params=_cparams(("parallel", "parallel")),
        name="window_attention",
    )(sink, qr, kr, kr, kr, p, p, p, p_ctx, p_ctx)


def _ctx_attn_body(sink_ref, q_ref, kc_ref, vc_ref, o_ref):
    kc = kc_ref[0].astype(BF16)
    vc = vc_ref[0].astype(BF16)
    q = (q_ref[0] * (HEAD_DIM ** -0.5)).astype(BF16)

    def parts(hk):
        sl = slice(hk * HEAD_DIM, (hk + 1) * HEAD_DIM)
        return [(kc[:, sl], vc[:, sl], None)]

    _attn_heads(q, sink_ref, parts, o_ref)


def context_attention(p_ctx, sink, q_col, k_col, v_col):
    bx, lc, _ = p_ctx.shape
    nq = N_Q_HEADS * HEAD_DIM
    nk = N_KV_HEADS * HEAD_DIM
    return pl.pallas_call(
        _ctx_attn_body,
        out_shape=jax.ShapeDtypeStruct((bx, lc, nq), F32),
        grid=(bx,),
        in_specs=[pl.BlockSpec(memory_space=pltpu.SMEM),
                  pl.BlockSpec((1, lc, nq), lambda b: (b, 0, q_col)),
                  pl.BlockSpec((1, lc, nk), lambda b: (b, 0, k_col)),
                  pl.BlockSpec((1, lc, nk), lambda b: (b, 0, v_col))],
        out_specs=pl.BlockSpec((1, lc, nq), lambda b: (b, 0, 0)),
        compiler_params=_cparams(("parallel",)),
        name="context_attention",
    )(sink, p_ctx, p_ctx, p_ctx)


def _out_proj_body(ya_ref, yb_ref, wa_ref, wb_ref, h_ref, g_ref, o_ref):
    y = jnp.dot(ya_ref[0].astype(BF16), wa_ref[...], preferred_element_type=F32)
    y = y + jnp.dot(yb_ref[0].astype(BF16), wb_ref[...], preferred_element_type=F32)
    o_ref[0] = h_ref[0] + g_ref[0] * y


def out_proj_residual(ya, yb, w_bf, h, g1):
    bx, lx, d = h.shape
    ca = ya.shape[2]
    cb = yb.shape[2]
    assert ca == cb and w_bf.shape[0] == ca + cb
    tm = _tile(lx, ROW_TILE)
    return pl.pallas_call(
        _out_proj_body,
        out_shape=jax.ShapeDtypeStruct((bx, lx, d), F32),
        grid=(bx, lx // tm),
        in_specs=[pl.BlockSpec((1, tm, ca), lambda b, i: (b, i, 0)),
                  pl.BlockSpec((1, tm, cb), lambda b, i: (b, i, 0)),
                  pl.BlockSpec((ca, d), lambda b, i: (0, 0)),
                  pl.BlockSpec((cb, d), lambda b, i: (1, 0)),
                  pl.BlockSpec((1, tm, d), lambda b, i: (b, i, 0)),
                  pl.BlockSpec((1, 1, d), lambda b, i: (b, 0, 0))],
        out_specs=pl.BlockSpec((1, tm, d), lambda b, i: (b, i, 0)),
        compiler_params=_cparams(("parallel", "parallel")),
        name="out_proj_residual",
    )(ya, yb, w_bf, w_bf, h, g1)


def _gla_consts(c, reverse):
    t = np.arange(c)
    tau = (c - 1 - t) if reverse else t
    mats = [tau[None, :] <= tau[:, None]]
    ups, masks = [], []
    m = 1
    while m < c:
        blk = tau // (2 * m)
        ref = blk * 2 * m + m - 1
        upper = (tau % (2 * m)) >= m
        mats.append(tau[None, :] <= ref[:, None])
        ups.append(upper[:, None])
        masks.append((blk[:, None] == blk[None, :]) & upper[:, None] & (~upper)[None, :])
        m *= 2
    sgn = np.broadcast_to(2.0 * np.stack(ups).astype(np.float32) - 1.0, (len(ups), c, LANES))
    return jnp.asarray(mats[0], BF16), jnp.asarray(sgn, F32), jnp.asarray(np.stack(masks), F32)


def _ref_rows(b, m, reverse):
    c, w = b.shape
    off = m if reverse else m - 1
    if 2 * m >= 8:
        g = b.reshape(c // (2 * m), 2 * m, w)
        return jnp.broadcast_to(g[:, off:off + 1, :], g.shape).reshape(c, w)
    pos = lax.broadcasted_iota(jnp.int32, b.shape, 0) % (2 * m)
    out = b
    for p in range(2 * m):
        if p != off:
            out = jnp.where(pos == p, pltpu.roll(b, (p - off) % c, 0), out)
    return out


def _gla_body(q_ref, f_ref, v_ref, lb_ref, s0_ref, cum_ref, up_ref, mask_ref, *rest, reverse, need_o):
    if need_o:
        o_ref, sout_ref, st_scr, qs_scr, kk_scr, lf_scr = rest
    else:
        sout_ref, st_scr, qs_scr, kk_scr, lf_scr = rest
    s_idx = pl.program_id(1)
    ts = q_ref.shape[1]
    c = GLA_CHUNK
    nch = ts // c
    nlev = up_ref.shape[0]
    dk = HG_DK
    nt = (((1,), (1,)), ((), ()))
    tn = (((0,), (0,)), ((), ()))

    @pl.when(s_idx == 0)
    def _():
        st_scr[...] = s0_ref[0]

    q = q_ref[0]
    lb = lb_ref[0]
    fg = lb + (1.0 - lb) * _sigmoid(f_ref[0])
    qs_scr[...] = q * _sigmoid(q)
    kk_scr[...] = 1.0 - fg
    lf_scr[...] = jnp.log(fg)

    def chunk(i, carry):
        ci = (nch - 1 - i) if reverse else i
        r0 = pl.multiple_of(ci * c, c)
        heads = range(HG_HEADS)
        cols = [slice(hd * dk, (hd + 1) * dk) for hd in heads]
        qc = [qs_scr[pl.ds(r0, c), cs] for cs in cols]
        kc = [kk_scr[pl.ds(r0, c), cs] for cs in cols]
        vc = [v_ref[0, pl.ds(r0, c), cs] for cs in cols]
        b = []
        for cs in cols:
            lf = lf_scr[pl.ds(r0, c), cs]
            l1 = lf.astype(BF16)
            r1 = lf - l1.astype(F32)
            l2 = r1.astype(BF16)
            l3 = (r1 - l2.astype(F32)).astype(BF16)
            sums = jnp.dot(cum_ref[...], jnp.concatenate([l1, l2, l3], axis=1), preferred_element_type=F32)
            b.append(sums[:, :dk] + sums[:, dk:2 * dk] + sums[:, 2 * dk:])
        st = [st_scr[hd] for hd in heads]
        if need_o:
            o = []
            for hd in heads:
                oi = lax.dot_general((qc[hd] * jnp.exp(b[hd])).astype(BF16), st[hd].astype(BF16), nt,
                                     preferred_element_type=F32)
                o.append(oi + jnp.sum(qc[hd] * kc[hd], axis=-1, keepdims=True) * vc[hd])
            a = [jnp.zeros((c, c), F32) for _ in heads]
            for li in range(nlev):
                for hd in heads:
                    e = jnp.exp((b[hd] - _ref_rows(b[hd], 1 << li, reverse)) * up_ref[li])
                    qd = (qc[hd] * e).astype(BF16)
                    kd = (kc[hd] * e).astype(BF16)
                    a[hd] = a[hd] + mask_ref[li] * lax.dot_general(qd, kd, nt, preferred_element_type=F32)
            for hd in heads:
                o_ref[0, pl.ds(r0, c), cols[hd]] = o[hd] + jnp.dot(a[hd].astype(BF16), vc[hd].astype(BF16),
                                                                    preferred_element_type=F32)
        for hd in heads:
            b_end = b[hd][0:1, :] if reverse else b[hd][c - 1:c, :]
            kdec = kc[hd] * jnp.exp(b_end - b[hd])
            upd = lax.dot_general(vc[hd].astype(BF16), kdec.astype(BF16), tn, preferred_element_type=F32)
            st_scr[hd] = st[hd] * jnp.exp(b_end) + upd
        return carry

    lax.fori_loop(0, nch, chunk, 0)

    @pl.when(s_idx == pl.num_programs(1) - 1)
    def _():
        sout_ref[0] = st_scr[...]


def gla_scan(p, lb3, s0, direction, q_col, f_col, v_col, need_o=True):
    bx, lx, _ = p.shape
    hh = HG_HEADS
    dk = HG_DK
    f = hh * dk
    ts = _tile(lx, GLA_SEQ_BLOCK)
    ns = lx // ts
    rev = direction == 1
    sblk = (lambda s: ns - 1 - s) if rev else (lambda s: s)
    cum, up, mask = _gla_consts(GLA_CHUNK, rev)
    const = lambda a: pl.BlockSpec(a.shape, lambda b, s: (0,) * a.ndim)
    o_shape = (jax.ShapeDtypeStruct((bx, lx, f), F32),) if need_o else ()
    o_spec = (pl.BlockSpec((1, ts, f), lambda b, s: (b, sblk(s), 0)),) if need_o else ()
    outs = pl.pallas_call(
        functools.partial(_gla_body, reverse=rev, need_o=need_o),
        out_shape=o_shape + (jax.ShapeDtypeStruct((bx, hh, dk, dk), F32),),
        grid=(bx, ns),
        in_specs=[pl.BlockSpec((1, ts, f), lambda b, s: (b, sblk(s), q_col)),
                  pl.BlockSpec((1, ts, f), lambda b, s: (b, sblk(s), f_col)),
                  pl.BlockSpec((1, ts, f), lambda b, s: (b, sblk(s), v_col)),
                  pl.BlockSpec((1, 1, f), lambda b, s: (direction, 0, 0)),
                  pl.BlockSpec((1, hh, dk, dk), lambda b, s: (b, 0, 0, 0)),
                  const(cum), const(up), const(mask)],
        out_specs=o_spec + (pl.BlockSpec((1, hh, dk, dk), lambda b, s: (b, 0, 0, 0)),),
        scratch_shapes=[pltpu.VMEM((hh, dk, dk), F32), pltpu.VMEM((ts, f), F32),
                        pltpu.VMEM((ts, f), F32), pltpu.VMEM((ts, f), F32)],
        compiler_params=_cparams(("parallel", "arbitrary")),
        name="gla_scan_bwd" if rev else "gla_scan_fwd",
    )(p, p, p, lb3, s0, cum, up, mask)
    return outs if need_o else (None, outs[0])


def _hg_readout_body(of_ref, ob_ref, g_ref, gn_ref, w_ref, h_ref, g1_ref, o_ref):
    o = of_ref[0] + ob_ref[0]
    gn = gn_ref[...]
    pieces = []
    for hh in range(HG_HEADS):
        oh = o[:, hh * HG_DK:(hh + 1) * HG_DK]
        y = oh * lax.rsqrt(jnp.mean(oh * oh, axis=-1, keepdims=True) + EPS)
        pieces.append(y * gn)
    on = jnp.concatenate(pieces, axis=1)
    g = g_ref[0]
    on = on * (g * _sigmoid(g))
    y = jnp.dot(on.astype(BF16), w_ref[...], preferred_element_type=F32)
    o_ref[0] = h_ref[0] + g1_ref[0] * y


def hgrn_readout_residual(o_f, o_b, p, g_col, onorm_row, w_bf, h, g1):
    bx, lx, d = h.shape
    f = o_f.shape[2]
    tm = _tile(lx, ROW_TILE)
    return pl.pallas_call(
        _hg_readout_body,
        out_shape=jax.ShapeDtypeStruct((bx, lx, d), F32),
        grid=(bx, lx // tm),
        in_specs=[pl.BlockSpec((1, tm, f), lambda b, i: (b, i, 0)),
                  pl.BlockSpec((1, tm, f), lambda b, i: (b, i, 0)),
                  pl.BlockSpec((1, tm, f), lambda b, i: (b, i, g_col)),
                  pl.BlockSpec((1, HG_DK), lambda b, i: (0, 0)),
                  pl.BlockSpec((f, d), lambda b, i: (0, 0)),
                  pl.BlockSpec((1, tm, d), lambda b, i: (b, i, 0)),
                  pl.BlockSpec((1, 1, d), lambda b, i: (b, 0, 0))],
        out_specs=pl.BlockSpec((1, tm, d), lambda b, i: (b, i, 0)),
        compiler_params=_cparams(("parallel", "parallel")),
        name="hgrn_readout_residual",
    )(o_f, o_b, p, onorm_row, w_bf, h, g1)


def _pair_rows():
    rows = [(0, b) for b in range(16)] + [(1, b) for b in range(8)]
    for a in range(2, 8):
        rows += [(a, b) for b in range(8)]
    rows += [(a, 0) for a in range(8, 16)]
    return rows


def _top16_rows(s):
    n = s.shape[0]
    ri = lax.broadcasted_iota(jnp.int32, s.shape, 0)
    rank = jnp.full(s.shape, float(PEER_TOPK), F32)
    vals = []
    for a in range(PEER_TOPK):
        m = jnp.max(s, axis=0, keepdims=True)
        first = jnp.min(jnp.where(s == m, ri, n), axis=0, keepdims=True)
        sel = ri == first
        rank = jnp.where(sel, float(a), rank)
        s = jnp.where(sel, -jnp.inf, s)
        vals.append(m)
    return rank, jnp.concatenate(vals, axis=0)


def _pair_candidates(v0, v1, okf):
    blocks = [v0[0:1] + v1, v0[1:2] + v1[0:8]]
    blocks += [v0[a:a + 1] + v1[0:8] for a in range(2, 8)]
    blocks += [v0[8:16] + v1[0:1]]
    return jnp.where(okf > 0.0, jnp.concatenate(blocks, axis=0), -jnp.inf)


def _gate_arrays(s0, s1, rank0, rank1, v0, v1, cand, selm):
    top = cand[0:1]
    z = jnp.sum(selm * jnp.exp(jnp.where(selm > 0.0, cand - top, 0.0)), axis=0, keepdims=True)
    cnt = [jnp.sum(selm[0:16], axis=0, keepdims=True)]
    cnt += [jnp.sum(selm[16 + 8 * (a - 1):24 + 8 * (a - 1)], axis=0, keepdims=True) for a in range(1, 8)]
    cnt8 = selm[72:80]
    crow = jnp.zeros(rank0.shape, F32)
    for a in range(8):
        crow = jnp.where(rank0 == float(a), cnt[a], crow)
    for a in range(8, 16):
        crow = jnp.where(rank0 == float(a), cnt8[a - 8:a - 7], crow)
    av = jnp.where(rank0 < float(PEER_TOPK), jnp.exp(s0 - v0[0:1]), 0.0)
    bn = jnp.where(rank1 < float(PEER_TOPK), jnp.exp(s1 - v1[0:1]), 0.0) / z
    return rank1, bn, crow, av


def _select_exact(s0, s1, flat, okf):
    rank0, v0 = _top16_rows(s0)
    rank1, v1 = _top16_rows(s1)
    cand = _pair_candidates(v0, v1, okf)
    work = cand
    selm = jnp.zeros(cand.shape, F32)
    for _ in range(PEER_TOPK):
        m = jnp.max(work, axis=0, keepdims=True)
        first = jnp.min(jnp.where(work == m, flat, 1 << 20), axis=0, keepdims=True)
        sel = flat == first
        selm = jnp.where(sel, 1.0, selm)
        work = jnp.where(sel, -jnp.inf, work)
    return _gate_arrays(s0, s1, rank0, rank1, v0, v1, cand, selm)


def _cmp_exchange(xs, i, l, larger_first):
    hi = jnp.maximum(xs[i], xs[l])
    lo = jnp.minimum(xs[i], xs[l])
    xs[i], xs[l] = (hi, lo) if larger_first else (lo, hi)


def _bitonic_merge_desc(xs):
    n = len(xs)
    j = n // 2
    while j >= 1:
        for i in range(n):
            if (i ^ j) > i:
                _cmp_exchange(xs, i, i ^ j, True)
        j //= 2


def _sorted_top16(s):
    n = PEER_TOPK
    xs = [s[g * 8:(g + 1) * 8] for g in range(n)]
    k = 2
    while k <= n:
        j = k // 2
        while j >= 1:
            for i in range(n):
                if (i ^ j) > i:
                    _cmp_exchange(xs, i, i ^ j, (i & k) == 0)
            j //= 2
        k *= 2
    for shift in (4, 6, 7):
        other = [pltpu.roll(x, shift, 0) for x in xs]
        xs = [jnp.maximum(xs[i], other[n - 1 - i]) for i in range(n)]
        _bitonic_merge_desc(xs)
    return jnp.concatenate([x[0:1] for x in xs], axis=0)


def _count_greater(s, v):
    r = [v[a:a + 1] for a in range(PEER_TOPK)]
    c8 = r[7] > s
    c4 = jnp.where(c8, r[11], r[3]) > s
    c2 = jnp.where(c8, jnp.where(c4, r[13], r[9]), jnp.where(c4, r[5], r[1])) > s
    p1 = jnp.where(c8,
                   jnp.where(c4, jnp.where(c2, r[14], r[12]), jnp.where(c2, r[10], r[8])),
                   jnp.where(c4, jnp.where(c2, r[6], r[4]), jnp.where(c2, r[2], r[0])))
    c1 = p1 > s
    g = (jnp.where(c8, 8.0, 0.0) + jnp.where(c4, 4.0, 0.0)) + (jnp.where(c2, 2.0, 0.0) + jnp.where(c1, 1.0, 0.0))
    return jnp.where(r[15] > s, float(PEER_TOPK), g)


def _select_fast(s0, s1, okf):
    w = s0.shape[1]
    s = jnp.concatenate([s0, s1], axis=1)
    v = _sorted_top16(s)
    rank = _count_greater(s, v)
    dup = jnp.max(jnp.where(v[0:15] == v[1:16], 1.0, 0.0), axis=0, keepdims=True)
    members = jnp.sum(jnp.where(s >= v[15:16], 1.0, 0.0), axis=0, keepdims=True)
    tie = dup + jnp.where(members != float(PEER_TOPK), 1.0, 0.0)
    v0, rank0, tie0 = v[:, :w], rank[:, :w], tie[:, :w]
    v1, rank1, tie1 = v[:, w:], rank[:, w:], tie[:, w:]
    cand = _pair_candidates(v0, v1, okf)
    work = cand
    m = cand[0:1]
    for _ in range(PEER_TOPK):
        m = jnp.max(work, axis=0, keepdims=True)
        work = jnp.where(work == m, -jnp.inf, work)
    selm = jnp.where(cand >= m, 1.0, 0.0)
    tie2 = jnp.where(jnp.sum(selm, axis=0, keepdims=True) != float(PEER_TOPK), 1.0, 0.0)
    return _gate_arrays(s0, s1, rank0, rank1, v0, v1, cand, selm), tie0 + tie1 + tie2


def _peer_prep_body(x_ref, wq_ref, keys_ref, flat_ref, okf_ref, rank1_ref, bn_ref, crow_ref, av_ref, q_scr):
    q_scr[...] = jnp.dot(wq_ref[...], x_ref[...], preferred_element_type=F32).astype(BF16)
    nk = N_KEYS
    tm = x_ref.shape[1]
    flat = flat_ref[...]
    okf = okf_ref[...]

    def store(h, cs, outs):
        rank1, bn, crow, av = outs
        rank1_ref[h, :, cs] = rank1.astype(BF16)
        bn_ref[h, :, cs] = bn.astype(BF16)
        crow_ref[h, :, cs] = crow
        av_ref[h, :, cs] = av

    def head(h, carry):
        r0 = pl.multiple_of(h * 2 * nk, 2 * nk)
        for c0 in range(0, tm, PEER_PREP_CW):
            cs = slice(c0, c0 + PEER_PREP_CW)
            s0 = jnp.dot(keys_ref[h, 0], q_scr[pl.ds(r0, nk), cs], preferred_element_type=F32)
            s1 = jnp.dot(keys_ref[h, 1], q_scr[pl.ds(r0 + nk, nk), cs], preferred_element_type=F32)
            outs, tie = _select_fast(s0, s1, okf)
            store(h, cs, outs)

            @pl.when(jnp.max(tie) > 0.0)
            def _():
                store(h, cs, _select_exact(s0, s1, flat, okf))
        return carry

    lax.fori_loop(0, PEER_HEADS, head, 0)


def peer_prep(x_t, wq_t, keys_bf):
    d, t = x_t.shape
    tm = _tile(t, PEER_PREP_TOK)
    rows = _pair_rows()
    flat = jnp.asarray([[a * 16 + b] for a, b in rows], jnp.int32)
    okf = jnp.asarray([[1.0 if (a + 1) * (b + 1) <= PEER_TOPK else 0.0] for a, b in rows], F32)
    out = jax.ShapeDtypeStruct((PEER_HEADS, N_KEYS, t), F32)
    out_bf = jax.ShapeDtypeStruct((PEER_HEADS, N_KEYS, t), BF16)
    ospec = pl.BlockSpec((PEER_HEADS, N_KEYS, tm), lambda i: (0, 0, i))
    return pl.pallas_call(
        _peer_prep_body,
        out_shape=(out_bf, out_bf, out, out),
        grid=(t // tm,),
        in_specs=[pl.BlockSpec((d, tm), lambda i: (0, i)),
                  pl.BlockSpec(wq_t.shape, lambda i: (0, 0)),
                  pl.BlockSpec(keys_bf.shape, lambda i: (0, 0, 0, 0)),
                  pl.BlockSpec(flat.shape, lambda i: (0, 0)),
                  pl.BlockSpec(okf.shape, lambda i: (0, 0))],
        out_specs=(ospec, ospec, ospec, ospec),
        scratch_shapes=[pltpu.VMEM((wq_t.shape[0], tm), BF16)],
        compiler_params=_cparams(("parallel",)),
        name="peer_prep",
    )(x_t, wq_t, keys_bf, flat, okf)


def _row_tile_bf16(row):
    tile = jnp.broadcast_to(row, (BF16_SUBLANES, row.shape[1])).astype(BF16)
    return jnp.concatenate([tile] * (N_KEYS // BF16_SUBLANES), axis=0)


def _peer_main_body(x_ref, u_ref, v_ref, vlast_ref, rank1_ref, bn_ref, crow_ref, av_ref, h_ref, g2_ref, fg_ref,
                    o_ref, acc_scr, pt_scr, *, final_norm):
    j = pl.program_id(1)
    nj = pl.num_programs(1)
    tm = x_ref.shape[1]
    nrow = u_ref.shape[0] // N_KEYS
    sub = PEER_SUB_ROWS * N_KEYS
    nsb = nrow // PEER_SUB_ROWS
    x = x_ref[...]
    rd = (j + 1) % 2
    wr = j % 2
    row0 = j * nrow

    @pl.when(j == 0)
    def _():
        acc_scr[...] = jnp.zeros(acc_scr.shape, F32)
        pt_scr[1] = jnp.zeros(pt_scr.shape[1:], BF16)

    def pre_act(sb):
        return jnp.dot(u_ref[sb * sub:(sb + 1) * sub, :], x, preferred_element_type=F32)

    pt_prev = pt_scr[rd]
    dn = v_ref.shape[1] // nsb
    acts, pvs = [], []
    for sb in range(nsb):
        acts.append(pre_act(sb))
        pvs.append(lax.dot_general(pt_prev, v_ref[:, sb * dn:(sb + 1) * dn], (((0,), (0,)), ((), ())),
                                   preferred_element_type=F32))
    for sb in range(nsb):
        act = acts[sb]
        for rr in range(PEER_SUB_ROWS):
            e1 = row0 + sb * PEER_SUB_ROWS + rr
            for c0 in range(0, tm, PEER_CW):
                cs = slice(c0, c0 + PEER_CW)
                w = None
                for hd in range(PEER_HEADS):
                    cr = _row_tile_bf16(crow_ref[hd, pl.ds(e1, 1), cs])
                    ar = _row_tile_bf16(av_ref[hd, pl.ds(e1, 1), cs])
                    t = jnp.where(rank1_ref[hd, :, cs] < cr, bn_ref[hd, :, cs], jnp.zeros((), BF16)) * ar
                    w = t if w is None else w + t
                a = act[rr * N_KEYS:(rr + 1) * N_KEYS, cs]
                gelu = 0.5 * a * (1.0 + lax.erf(a * (2.0 ** -0.5)))
                r0 = (sb * PEER_SUB_ROWS + rr) * N_KEYS
                pt_scr[wr, r0:r0 + N_KEYS, cs] = gelu.astype(BF16) * w
    for sb in range(nsb):
        acc_scr[:, sb * dn:(sb + 1) * dn] += pvs[sb]

    @pl.when(j == nj - 1)
    def _():
        last = lax.dot_general(pt_scr[wr], vlast_ref[...], (((0,), (0,)), ((), ())), preferred_element_type=F32)
        hn = h_ref[...] + g2_ref[0] * (acc_scr[...] + last)
        if final_norm:
            hn = hn * lax.rsqrt(jnp.mean(hn * hn, axis=-1, keepdims=True) + EPS) * fg_ref[...]
        o_ref[...] = hn


def peer_main(x_t, u_bf, v_bf, layer, prep, h2, g2, tokens_per_batch, final_g, final_norm):
    d, t = x_t.shape
    e = u_bf.shape[1]
    tm = _tile(tokens_per_batch, PEER_TOK)
    te = PEER_ROWS * N_KEYS
    assert e % te == 0 and tm % PEER_CW == 0
    tpb = tokens_per_batch // tm
    nj = e // te
    assert PEER_ROWS % PEER_SUB_ROWS == 0 and PEER_ROWS // PEER_SUB_ROWS >= 2
    pspec = pl.BlockSpec((PEER_HEADS, N_KEYS, tm), lambda i, j: (0, 0, i))
    return pl.pallas_call(
        functools.partial(_peer_main_body, final_norm=final_norm),
        out_shape=jax.ShapeDtypeStruct((t, d), F32),
        grid=(t // tm, nj),
        in_specs=[pl.BlockSpec((d, tm), lambda i, j: (0, i)),
                  pl.BlockSpec((None, te, d), lambda i, j: (layer, j, 0)),
                  pl.BlockSpec((None, te, d), lambda i, j: (layer, jnp.maximum(j - 1, 0), 0)),
                  pl.BlockSpec((None, te, d), lambda i, j: (layer, nj - 1, 0)),
                  pspec, pspec, pspec, pspec,
                  pl.BlockSpec((tm, d), lambda i, j: (i, 0)),
                  pl.BlockSpec((1, 1, d), lambda i, j: (i // tpb, 0, 0)),
                  pl.BlockSpec((1, d), lambda i, j: (0, 0))],
        out_specs=pl.BlockSpec((tm, d), lambda i, j: (i, 0)),
        scratch_shapes=[pltpu.VMEM((tm, d), F32), pltpu.VMEM((2, te, tm), BF16)],
        compiler_params=_cparams(("parallel", "arbitrary")),
        name="peer_dense",
    )(x_t, u_bf, v_bf, v_bf, *prep, h2, g2, final_g)


def peer_residual(h, norm_g, sh, sc, g2, wq_t, keys_bf, u_bf, v_bf, layer, final_g, final_norm):
    bx, lx, d = h.shape
    a_t = norm_mod_t(h, norm_g, sh, sc)
    prep = peer_prep(a_t, wq_t, keys_bf)
    span = lx if g2.shape[0] == bx else bx * lx
    out = peer_main(a_t, u_bf, v_bf, layer, prep, h.reshape(bx * lx, d), g2, span, final_g, final_norm)
    return out.reshape(bx, lx, d)


def kernel(x, c, ctx, c_ctx, ada_w, ada_b, norm1_g, norm2_g, final_g, ab_w_in, ab_w_out, hy_conv_w, hy_conv_b,
           hy_filt_w1, hy_filt_b1, hy_filt_w2, hy_filt_b2, hy_filt_w3, hy_filt_freq, hy_skip, attn_sink,
           hg_w_in, hg_w_out, hg_lb_logits, hg_onorm_g, peer_wq, peer_keys, peer_u, peer_v):
    bsz, seq, d = x.shape
    depth = ada_w.shape[0]
    assert depth == 2
    lb_p = jax.nn.softmax(hg_lb_logits.astype(F32), axis=0)
    lb_all = jnp.cumsum(lb_p, axis=0) - lb_p[0:1]

    c16 = jnp.concatenate([c, c_ctx[None], jnp.zeros((16 - bsz - 1, d), F32)], axis=0)
    final_row = final_g[None]
    u_bf = peer_u.astype(BF16)
    v_bf = peer_v.astype(BF16)
    h_lat, h_ctx = x, ctx

    for l in range(depth):
        need_ctx = l < depth - 1
        mod = ada_mod(c16, ada_w, ada_b[:, None, :], l)
        lat = [mod[:bsz, i * d:(i + 1) * d][:, None, :] for i in range(6)]
        cx = [jnp.broadcast_to(mod[bsz:bsz + 1, i * d:(i + 1) * d][:, None, :], (bsz, 1, d)) for i in range(6)]
        n1 = norm1_g[l][None]
        n2 = norm2_g[l][None]
        wq_t = peer_wq[l].T.astype(BF16)
        keys_bf = peer_keys[l].astype(BF16)
        j = l // 2
        if l % 2 == 0:
            w_in = ab_w_in[j].astype(BF16)
            w_out = ab_w_out[j].astype(BF16)
            p_lat = norm_mod_matmul(h_lat, n1, lat[0], lat[1], w_in)
            p_ctx = norm_mod_matmul(h_ctx, n1, cx[0], cx[1], w_in)
            hy = (hy_conv_w[j], hy_conv_b[j][None], hy_filt_w1[j], hy_filt_b1[j], hy_filt_w2[j], hy_filt_b2[j],
                  hy_filt_w3[j], hy_filt_freq[j], hy_skip[j])
            cos, sin = _rope_tables(seq)
            qr, kr = rope_qk(p_lat, cos, sin, 3, 16)
            sink = attn_sink[j]
            at_lat = window_attention(qr, kr, p_lat, p_ctx, sink, 16, 17)
            hy_lat = hyena_mixer(p_lat, *hy)
            h_lat = out_proj_residual(hy_lat, at_lat, w_out, h_lat, lat[2])
            if need_ctx:
                at_ctx = context_attention(p_ctx, sink, 3, 16, 17)
                hy_ctx = hyena_mixer(p_ctx, *hy)
                h_ctx = out_proj_residual(hy_ctx, at_ctx, w_out, h_ctx, cx[2])
        else:
            w_in = hg_w_in[j].astype(BF16)
            w_out = hg_w_out[j].astype(BF16)
            lb3 = lb_all[l].astype(F32)[:, None, :]
            p_lat = norm_mod_matmul(h_lat, n1, lat[0], lat[1], w_in)
            p_ctx = norm_mod_matmul(h_ctx, n1, cx[0], cx[1], w_in)
            s0 = jnp.zeros((bsz, HG_HEADS, HG_DK, HG_DK), F32)
            o_cf, s_cf = gla_scan(p_ctx, lb3, s0, 0, 0, 1, 3, need_o=need_ctx)
            o_cb, s_cb = gla_scan(p_ctx, lb3, s0, 1, 0, 2, 3, need_o=need_ctx)
            o_lf, _ = gla_scan(p_lat, lb3, s_cf, 0, 0, 1, 3)
            o_lb, _ = gla_scan(p_lat, lb3, s_cb, 1, 0, 2, 3)
            onorm = hg_onorm_g[j].astype(F32)[None]
            h_lat = hgrn_readout_residual(o_lf, o_lb, p_lat, 4, onorm, w_out, h_lat, lat[2])
            if need_ctx:
                h_ctx = hgrn_readout_residual(o_cf, o_cb, p_ctx, 4, onorm, w_out, h_ctx, cx[2])
        last = l == depth - 1
        h_lat = peer_residual(h_lat, n2, lat[3], lat[4], lat[5], wq_t, keys_bf, u_bf, v_bf, l, final_row, last)
        if need_ctx:
            h_ctx = peer_residual(h_ctx, n2, cx[3], cx[4], cx[5][:1], wq_t, keys_bf, u_bf, v_bf, l, final_row, False)
    return h_lat
```

```python
import functools
import math

import jax
import jax.numpy as jnp
import numpy as np
from jax import lax
from jax.experimental import pallas as pl
from jax.experimental.pallas import tpu as pltpu

F32 = jnp.float32
BF16 = jnp.bfloat16
EPS = 1e-6

HY_C = 512
HY_EMB = 33
HY_EMB_PAD = 40
HY_FILT_W = 64
HY_DECAY_SHORT_PCT = 0.3
HY_DECAY_LONG_PCT = 1.5
HY_TARGET = 1e-2
HEAD_DIM = 64
N_Q_HEADS = 8
N_KV_HEADS = 2
GQA_GROUP = 4
WINDOW = 128
ATTN_BLK = 128
GRID_W = 64
ROPE_BASE = 10000.0
HG_HEADS = 8
HG_DK = 128
PEER_HEADS = 8
PEER_TOPK = 16
N_KEYS = 128

LANES = 128
BF16_SUBLANES = 16
VMEM_LIMIT_BYTES = 56 * 1024 * 1024
ROW_TILE = 512
PROJ_ROW_TILE = 1024
GLA_CHUNK = 128
GLA_SEQ_BLOCK = 256
PEER_PREP_TOK = 256
PEER_PREP_CW = 256
PEER_TOK = 512
PEER_ROWS = 8
PEER_SUB_ROWS = 2
PEER_CW = 256
NEG_BIG = -1e30


def _cparams(sem):
    return pltpu.CompilerParams(dimension_semantics=sem, vmem_limit_bytes=VMEM_LIMIT_BYTES)


def _sigmoid(x):
    return 1.0 / (1.0 + jnp.exp(-x))


def _tile(n, t):
    t = min(n, t)
    assert n % t == 0, (n, t)
    return t


def _ada_body(c_ref, w_ref, b_ref, o_ref):
    c = c_ref[...]
    s = c * _sigmoid(c)
    o_ref[...] = jnp.dot(s.astype(BF16), w_ref[...].astype(BF16), preferred_element_type=F32) + b_ref[...]


def ada_mod(c16, w_all, b_all, layer):
    _, d, n = w_all.shape
    tn = _tile(n, 1536)
    return pl.pallas_call(
        _ada_body,
        out_shape=jax.ShapeDtypeStruct((c16.shape[0], n), F32),
        grid=(n // tn,),
        in_specs=[pl.BlockSpec(c16.shape, lambda j: (0, 0)),
                  pl.BlockSpec((None, d, tn), lambda j: (layer, 0, j)),
                  pl.BlockSpec((None, 1, tn), lambda j: (layer, 0, j))],
        out_specs=pl.BlockSpec((c16.shape[0], tn), lambda j: (0, j)),
        compiler_params=_cparams(("arbitrary",)),
        name="ada_mod",
    )(c16, w_all, b_all)


def _norm_mod(x, g, sh, sc):
    y = x * lax.rsqrt(jnp.mean(x * x, axis=-1, keepdims=True) + EPS)
    return (y * g) * (1.0 + sc) + sh


def _nm_matmul_body(h_ref, g_ref, sh_ref, sc_ref, w_ref, p_ref, a_scr):
    @pl.when(pl.program_id(2) == 0)
    def _():
        a_scr[...] = _norm_mod(h_ref[0], g_ref[...], sh_ref[0], sc_ref[0]).astype(BF16)

    p_ref[0] = jnp.dot(a_scr[...], w_ref[...], preferred_element_type=F32)


def norm_mod_matmul(h, g, sh, sc, w_bf):
    bx, lx, d = h.shape
    n = w_bf.shape[1]
    tm = _tile(lx, PROJ_ROW_TILE)
    tn = _tile(n, 1280 if n % 1280 == 0 else (1152 if n % 1152 == 0 else 1024))
    return pl.pallas_call(
        _nm_matmul_body,
        out_shape=jax.ShapeDtypeStruct((bx, lx, n), F32),
        grid=(bx, lx // tm, n // tn),
        in_specs=[pl.BlockSpec((1, tm, d), lambda b, i, j: (b, i, 0)),
                  pl.BlockSpec((1, d), lambda b, i, j: (0, 0)),
                  pl.BlockSpec((1, 1, d), lambda b, i, j: (b, 0, 0)),
                  pl.BlockSpec((1, 1, d), lambda b, i, j: (b, 0, 0)),
                  pl.BlockSpec((d, tn), lambda b, i, j: (0, j))],
        out_specs=pl.BlockSpec((1, tm, tn), lambda b, i, j: (b, i, j)),
        scratch_shapes=[pltpu.VMEM((tm, d), BF16)],
        compiler_params=_cparams(("parallel", "parallel", "arbitrary")),
        name="norm_mod_matmul",
    )(h, g, sh, sc, w_bf)


def _nm_only_body(h_ref, g_ref, sh_ref, sc_ref, a_ref):
    a_ref[...] = _norm_mod(h_ref[0], g_ref[...], sh_ref[0], sc_ref[0]).T.astype(BF16)


def norm_mod_t(h, g, sh, sc):
    bx, lx, d = h.shape
    tm = _tile(lx, ROW_TILE)
    nt = lx // tm
    return pl.pallas_call(
        _nm_only_body,
        out_shape=jax.ShapeDtypeStruct((d, bx * lx), BF16),
        grid=(bx, nt),
        in_specs=[pl.BlockSpec((1, tm, d), lambda b, i: (b, i, 0)),
                  pl.BlockSpec((1, d), lambda b, i: (0, 0)),
                  pl.BlockSpec((1, 1, d), lambda b, i: (b, 0, 0)),
                  pl.BlockSpec((1, 1, d), lambda b, i: (b, 0, 0))],
        out_specs=pl.BlockSpec((d, tm), lambda b, i: (0, b * nt + i)),
        compiler_params=_cparams(("parallel", "parallel")),
        name="norm_mod_t",
    )(h, g, sh, sc)


def _mm_body(a_ref, b_ref, o_ref):
    o_ref[...] = jnp.dot(a_ref[...], b_ref[...], preferred_element_type=F32)


def matmul_bf16(a, b):
    m, k = a.shape
    n = b.shape[1]
    tm = _tile(m, 1024)
    tn = _tile(n, 1024)
    return pl.pallas_call(
        _mm_body,
        out_shape=jax.ShapeDtypeStruct((m, n), F32),
        grid=(m // tm, n // tn),
        in_specs=[pl.BlockSpec((tm, k), lambda i, j: (i, 0)),
                  pl.BlockSpec((k, tn), lambda i, j: (0, j))],
        out_specs=pl.BlockSpec((tm, tn), lambda i, j: (i, j)),
        compiler_params=_cparams(("parallel", "parallel")),
        name="matmul_bf16",
    )(a, b)


def _filter_body(z_ref, w1_ref, b1_ref, w2_ref, b2_ref, w3_ref, fr_ref, dec_ref, o_ref):
    hi = lax.Precision.HIGHEST
    fr = fr_ref[...]
    hdn = jnp.sin(fr * (jnp.dot(z_ref[...], w1_ref[...], precision=hi, preferred_element_type=F32) + b1_ref[...]))
    hdn = jnp.sin(fr * (jnp.dot(hdn, w2_ref[...], precision=hi, preferred_element_type=F32) + b2_ref[...]))
    h = jnp.dot(hdn, w3_ref[...], precision=hi, preferred_element_type=F32)
    dec = dec_ref[...]
    c = dec.shape[1]
    h0 = h[:, :c] * dec
    h1 = h[:, c:] * dec
    nrm = jnp.sum(jnp.abs(h0) + jnp.abs(h1), axis=0, keepdims=True)
    inv = 1.0 / nrm
    ri = lax.broadcasted_iota(jnp.int32, h1.shape, 0)
    o_ref[:, :c] = (h0 * inv).astype(BF16)
    o_ref[:, c:] = jnp.where(ri == 0, 0.0, h1 * inv).astype(BF16)


def hyena_filters(zfeat, w1p, b1, w2, b2, w3, freq, decay):
    l = zfeat.shape[0]
    c = decay.shape[1]
    n_order = w3.shape[1] // (2 * c)
    full = lambda shape: pl.BlockSpec(shape, lambda o: (0,) * len(shape))
    return pl.pallas_call(
        _filter_body,
        out_shape=jax.ShapeDtypeStruct((l, n_order * 2 * c), BF16),
        grid=(n_order,),
        in_specs=[full(zfeat.shape), full(w1p.shape), full(b1.shape), full(w2.shape), full(b2.shape),
                  pl.BlockSpec((w3.shape[0], 2 * c), lambda o: (0, o)),
                  full(freq.shape), full(decay.shape)],
        out_specs=pl.BlockSpec((l, 2 * c), lambda o: (0, o)),
        compiler_params=_cparams(("arbitrary",)),
        name="hyena_filters",
    )(zfeat, w1p, b1, w2, b2, w3, freq, decay)


def _short_conv_body(u_ref, w_ref, b_ref, o_ref):
    u = u_ref[0]
    l = u.shape[0]
    ri = lax.broadcasted_iota(jnp.int32, u.shape, 0)
    prev = jnp.where(ri == 0, 0.0, pltpu.roll(u, 1, 0))
    nxt = jnp.where(ri == l - 1, 0.0, pltpu.roll(u, l - 1, 0))
    w = w_ref[...]
    o_ref[0] = prev * w[0:1] + u * w[1:2] + nxt * w[2:3] + b_ref[...]


def short_conv(p, conv_w, conv_b):
    bx, lx, _ = p.shape
    c3 = conv_w.shape[1]
    tc = _tile(c3, 512)
    return pl.pallas_call(
        _short_conv_body,
        out_shape=jax.ShapeDtypeStruct((bx, lx, c3), F32),
        grid=(bx, c3 // tc),
        in_specs=[pl.BlockSpec((1, lx, tc), lambda b, j: (b, 0, j)),
                  pl.BlockSpec((3, tc), lambda b, j: (0, j)),
                  pl.BlockSpec((1, tc), lambda b, j: (0, j))],
        out_specs=pl.BlockSpec((1, lx, tc), lambda b, j: (b, 0, j)),
        compiler_params=_cparams(("parallel", "parallel")),
        name="hyena_short_conv",
    )(p, conv_w, conv_b)


def _hy_fwd_body(z_ref, f_ref, h0_ref, h1_ref, y_ref):
    z = z_ref[0].astype(BF16)
    kt = f_ref.shape[1]
    zre = jnp.dot(f_ref[0], z, preferred_element_type=F32)
    zim = jnp.dot(f_ref[1], z, preferred_element_type=F32)
    k0 = (pl.program_id(0) * kt + lax.broadcasted_iota(jnp.int32, zre.shape, 0)) == 0
    hre = h0_ref[0] + h1_ref[0]
    him = jnp.where(k0, h0_ref[1] + h1_ref[1], h0_ref[1] - h1_ref[1])
    yre = jnp.where(k0, zre * hre, zre * hre - zim * him)
    yim = jnp.where(k0, zim * him, zre * him + zim * hre)
    y_ref[0, 0] = yre.astype(BF16)
    y_ref[0, 1] = yim.astype(BF16)


def hyena_fwd(z_src, z_col, f3, fh3, order, c):
    bx, lx, _ = z_src.shape
    kt = _tile(lx, 512)
    return pl.pallas_call(
        _hy_fwd_body,
        out_shape=jax.ShapeDtypeStruct((bx, 2, lx, c), BF16),
        grid=(lx // kt, bx),
        in_specs=[pl.BlockSpec((1, lx, c), lambda k, b: (b, 0, z_col)),
                  pl.BlockSpec((2, kt, lx), lambda k, b: (0, k, 0)),
                  pl.BlockSpec((2, kt, c), lambda k, b: (0, k, 2 * order)),
                  pl.BlockSpec((2, kt, c), lambda k, b: (0, k, 2 * order + 1))],
        out_specs=pl.BlockSpec((1, 2, kt, c), lambda k, b: (b, 0, k, 0)),
        compiler_params=_cparams(("parallel", "parallel")),
        name="hyena_dft_fwd",
    )(z_src, f3, fh3, fh3)


def _hy_inv_body(y_ref, fi_ref, gate_ref, z_ref, skip_ref, o_ref):
    conv = jnp.dot(fi_ref[...], y_ref[0], preferred_element_type=F32)
    o_ref[0] = gate_ref[0] * (conv + skip_ref[...] * z_ref[0])


def hyena_inv(yf, finv, gate_src, gate_col, z_src, z_col, skip_row):
    bx, l2, c = yf.shape
    lx = l2 // 2
    tl = _tile(lx, 512)
    return pl.pallas_call(
        _hy_inv_body,
        out_shape=jax.ShapeDtypeStruct((bx, lx, c), F32),
        grid=(lx // tl, bx),
        in_specs=[pl.BlockSpec((1, l2, c), lambda t, b: (b, 0, 0)),
                  pl.BlockSpec((tl, l2), lambda t, b: (t, 0)),
                  pl.BlockSpec((1, tl, c), lambda t, b: (b, t, gate_col)),
                  pl.BlockSpec((1, tl, c), lambda t, b: (b, t, z_col)),
                  pl.BlockSpec((1, c), lambda t, b: (0, 0))],
        out_specs=pl.BlockSpec((1, tl, c), lambda t, b: (b, t, 0)),
        compiler_params=_cparams(("parallel", "parallel")),
        name="hyena_dft_inv",
    )(yf, finv, gate_src, z_src, skip_row)


def _dft_mats(l):
    n = 2 * l
    k = jnp.arange(l, dtype=jnp.int32)[:, None]
    s = jnp.arange(l, dtype=jnp.int32)[None, :]

    def table(rows):
        ang = ((rows[:, None] * s) % n).astype(F32) * (2.0 * math.pi / n)
        return jnp.cos(ang), jnp.sin(ang)

    r = 1
    while r * r < l:
        r *= 2
    ch, sh = table(jnp.arange(l // r, dtype=jnp.int32) * r)
    cl, sl = table(jnp.arange(r, dtype=jnp.int32))
    cosm = (ch[:, None, :] * cl[None, :, :] - sh[:, None, :] * sl[None, :, :]).reshape(l, l)
    sinm = -(sh[:, None, :] * cl[None, :, :] + ch[:, None, :] * sl[None, :, :]).reshape(l, l)
    nyq = jnp.where(s % 2 == 0, 1.0, -1.0).astype(F32)
    imag = jnp.where(k == 0, nyq, sinm)
    fwd = jnp.concatenate([cosm, imag], axis=0)
    scale = jnp.where(jnp.arange(l) == 0, 1.0 / n, 2.0 / n).astype(F32)
    inv = jnp.concatenate([cosm.T * scale[None, :], imag.T * scale[None, :]], axis=1)
    return fwd.astype(BF16), inv.astype(BF16)


def _filter_features(l):
    bands = (HY_EMB - 1) // 2
    t = jnp.linspace(0.0, 1.0, l, dtype=F32)[:, None]
    w = 2.0 * math.pi * jnp.arange(l, dtype=F32)[:, None] / l
    f = jnp.linspace(1e-4, bands - 1, bands, dtype=F32)[None, :]
    z = jnp.concatenate([t, jnp.cos(f * w), -jnp.sin(f * w)], axis=-1)
    z = jnp.pad(z, ((0, 0), (0, HY_EMB_PAD - HY_EMB)))
    max_decay = math.log(HY_TARGET) / HY_DECAY_SHORT_PCT
    min_decay = math.log(HY_TARGET) / HY_DECAY_LONG_PCT
    deltas = jnp.abs(jnp.linspace(min_decay, max_decay, HY_C, dtype=F32))
    return z, jnp.exp(-t * deltas)


def hyena_mixer(p, conv_w, conv_b, fw1, fb1, fw2, fb2, fw3, ffreq, d_skip):
    bx, lx, _ = p.shape
    c = HY_C
    zfeat, decay = _filter_features(lx)
    w1p = jnp.pad(fw1, ((0, HY_EMB_PAD - HY_EMB), (0, 0)))
    filt = hyena_filters(zfeat, w1p, fb1[None], fw2, fb2[None], fw3, ffreq[None], decay)
    fwd, inv = _dft_mats(lx)
    fh3 = matmul_bf16(fwd, filt).reshape(2, lx, filt.shape[1])
    f3 = fwd.reshape(2, lx, lx)
    uc = short_conv(p, conv_w, conv_b)
    y0 = hyena_fwd(uc, 0, f3, fh3, 0, c).reshape(bx, 2 * lx, c)
    z1 = hyena_inv(y0, inv, uc, 1, uc, 0, d_skip[0:1])
    y1 = hyena_fwd(z1, 0, f3, fh3, 1, c).reshape(bx, 2 * lx, c)
    return hyena_inv(y1, inv, uc, 2, z1, 0, d_skip[1:2])


def _rope_tables(l):
    rows = l // GRID_W
    r = jnp.broadcast_to(jnp.arange(rows)[:, None], (rows, GRID_W)).reshape(-1).astype(F32)
    col = jnp.broadcast_to(jnp.arange(GRID_W)[None, :], (rows, GRID_W)).reshape(-1).astype(F32)
    nf = HEAD_DIM // 4
    inv = ROPE_BASE ** (-jnp.arange(nf, dtype=F32) / nf)
    ar = r[:, None] * inv
    ac = col[:, None] * inv
    ang = jnp.concatenate([ar, ar, ac, ac], axis=-1)
    sign = jnp.concatenate([-jnp.ones((nf,)), jnp.ones((nf,))] * 2).astype(F32)
    cos = jnp.tile(jnp.cos(ang), (1, N_Q_HEADS))
    sin = jnp.tile(jnp.sin(ang) * sign[None, :], (1, N_Q_HEADS))
    return cos, sin


def _rope_body(q_ref, k_ref, cos_ref, sin_ref, qo_ref, ko_ref):
    def rot(x, cos, sin):
        n = x.shape[1]
        lane = lax.broadcasted_iota(jnp.int32, x.shape, 1)
        first = (lane % 32) < 16
        partner = jnp.where(first, pltpu.roll(x, n - 16, 1), pltpu.roll(x, 16, 1))
        return x * cos + partner * sin

    nk = k_ref.shape[2]
    qo_ref[0] = (rot(q_ref[0], cos_ref[...], sin_ref[...]) * (HEAD_DIM ** -0.5)).astype(BF16)
    ko_ref[0] = rot(k_ref[0], cos_ref[:, :nk], sin_ref[:, :nk]).astype(BF16)


def rope_qk(p, cos, sin, q_col, k_col):
    bx, lx, _ = p.shape
    nq = N_Q_HEADS * HEAD_DIM
    nk = N_KV_HEADS * HEAD_DIM
    tr = _tile(lx, ROW_TILE)
    return pl.pallas_call(
        _rope_body,
        out_shape=(jax.ShapeDtypeStruct((bx, lx, nq), BF16), jax.ShapeDtypeStruct((bx, lx, nk), BF16)),
        grid=(bx, lx // tr),
        in_specs=[pl.BlockSpec((1, tr, nq), lambda b, i: (b, i, q_col)),
                  pl.BlockSpec((1, tr, nk), lambda b, i: (b, i, k_col)),
                  pl.BlockSpec((tr, nq), lambda b, i: (i, 0)),
                  pl.BlockSpec((tr, nq), lambda b, i: (i, 0))],
        out_specs=(pl.BlockSpec((1, tr, nq), lambda b, i: (b, i, 0)),
                   pl.BlockSpec((1, tr, nk), lambda b, i: (b, i, 0))),
        compiler_params=_cparams(("parallel", "parallel")),
        name="rope_qk",
    )(p, p, cos, sin)


def _attn_heads(q, sink_ref, parts_for_head, o_ref):
    t = q.shape[0]
    nt = (((1,), (1,)), ((), ()))
    heads = range(N_KV_HEADS)
    q4, sink, parts = [], [], []
    for hk in heads:
        q4.append(jnp.concatenate(
            [q[:, (hk * GQA_GROUP + g) * HEAD_DIM:(hk * GQA_GROUP + g + 1) * HEAD_DIM] for g in range(GQA_GROUP)],
            axis=0))
        sink.append(jnp.concatenate(
            [jnp.full((t, 1), sink_ref[hk * GQA_GROUP + g], F32) for g in range(GQA_GROUP)], axis=0))
        parts.append(parts_for_head(hk))
    ss = []
    for hk in heads:
        row = []
        for k, _, mask in parts[hk]:
            sc = lax.dot_general(q4[hk], k, nt, preferred_element_type=F32)
            row.append(sc if mask is None else jnp.where(mask, sc, NEG_BIG))
        ss.append(row)
    m = []
    for hk in heads:
        mh = sink[hk]
        for sc in ss[hk]:
            mh = jnp.maximum(mh, jnp.max(sc, axis=-1, keepdims=True))
        m.append(mh)
    es = [[jnp.exp(sc - m[hk]) for sc in ss[hk]] for hk in heads]
    den = []
    for hk in heads:
        dh = jnp.exp(sink[hk] - m[hk])
        for e in es[hk]:
            dh = dh + jnp.sum(e, axis=-1, keepdims=True)
        den.append(dh)
    outs = []
    for hk in heads:
        acc = None
        for e, (_, v, _) in zip(es[hk], parts[hk]):
            o = jnp.dot(e.astype(BF16), v, preferred_element_type=F32)
            acc = o if acc is None else acc + o
        o4 = acc / den[hk]
        outs.extend([o4[g * t:(g + 1) * t] for g in range(GQA_GROUP)])
    o_ref[0] = jnp.concatenate(outs, axis=1)


def _win_attn_body(sink_ref, q_ref, k0_ref, k1_ref, k2_ref, v0_ref, v1_ref, v2_ref, kc_ref, vc_ref, o_ref, *, seq_len):
    n = pl.program_id(1)
    blk = q_ref.shape[1]
    lc = kc_ref.shape[1]
    ka = jnp.concatenate([k0_ref[0], k1_ref[0], k2_ref[0], kc_ref[0].astype(BF16)], axis=0)
    va = jnp.concatenate([v0_ref[0], v1_ref[0], v2_ref[0], vc_ref[0]], axis=0).astype(BF16)
    qi = lax.broadcasted_iota(jnp.int32, (blk, 3 * blk + lc), 0)
    kj = lax.broadcasted_iota(jnp.int32, (blk, 3 * blk + lc), 1)
    kpos = (n - 1) * blk + kj
    diff = qi + blk - kj
    valid = (kj >= 3 * blk) | ((kpos >= 0) & (kpos < seq_len) & (diff <= WINDOW) & (diff >= -WINDOW))
    valid4 = jnp.concatenate([valid] * GQA_GROUP, axis=0)

    def parts(hk):
        sl = slice(hk * HEAD_DIM, (hk + 1) * HEAD_DIM)
        return [(ka[:, sl], va[:, sl], valid4)]

    _attn_heads(q_ref[0], sink_ref, parts, o_ref)


def window_attention(qr, kr, p, p_ctx, sink, k_col, v_col):
    bx, lx, nq = qr.shape
    nk = kr.shape[2]
    lc = p_ctx.shape[1]
    blk = ATTN_BLK
    nb = lx // blk
    lo = lambda b, n: (b, jnp.maximum(n - 1, 0), 0)
    mid = lambda b, n: (b, n, 0)
    hi = lambda b, n: (b, jnp.minimum(n + 1, nb - 1), 0)
    vlo = lambda b, n: (b, jnp.maximum(n - 1, 0), v_col)
    vmid = lambda b, n: (b, n, v_col)
    vhi = lambda b, n: (b, jnp.minimum(n + 1, nb - 1), v_col)
    return pl.pallas_call(
        functools.partial(_win_attn_body, seq_len=lx),
        out_shape=jax.ShapeDtypeStruct((bx, lx, nq), F32),
        grid=(bx, nb),
        in_specs=[pl.BlockSpec(memory_space=pltpu.SMEM),
                  pl.BlockSpec((1, blk, nq), mid),
                  pl.BlockSpec((1, blk, nk), lo), pl.BlockSpec((1, blk, nk), mid), pl.BlockSpec((1, blk, nk), hi),
                  pl.BlockSpec((1, blk, nk), vlo), pl.BlockSpec((1, blk, nk), vmid), pl.BlockSpec((1, blk, nk), vhi),
                  pl.BlockSpec((1, lc, nk), lambda b, n: (b, 0, k_col)),
                  pl.BlockSpec((1, lc, nk), lambda b, n: (b, 0, v_col))],
        out_specs=pl.BlockSpec((1, blk, nq), mid),
        compiler_params=_cparams(("parallel", "parallel")),
        name="window_attention",
    )(sink, qr, kr, kr, kr, p, p, p, p_ctx, p_ctx)


def _ctx_attn_body(sink_ref, q_ref, kc_ref, vc_ref, o_ref):
    kc = kc_ref[0].astype(BF16)
    vc = vc_ref[0].astype(BF16)
    q = (q_ref[0] * (HEAD_DIM ** -0.5)).astype(BF16)

    def parts(hk):
        sl = slice(hk * HEAD_DIM, (hk + 1) * HEAD_DIM)
        return [(kc[:, sl], vc[:, sl], None)]

    _attn_heads(q, sink_ref, parts, o_ref)


def context_attention(p_ctx, sink, q_col, k_col, v_col):
    bx, lc, _ = p_ctx.shape
    nq = N_Q_HEADS * HEAD_DIM
    nk = N_KV_HEADS * HEAD_DIM
    return pl.pallas_call(
        _ctx_attn_body,
        out_shape=jax.ShapeDtypeStruct((bx, lc, nq), F32),
        grid=(bx,),
        in_specs=[pl.BlockSpec(memory_space=pltpu.SMEM),
                  pl.BlockSpec((1, lc, nq), lambda b: (b, 0, q_col)),
                  pl.BlockSpec((1, lc, nk), lambda b: (b, 0, k_col)),
                  pl.BlockSpec((1, lc, nk), lambda b: (b, 0, v_col))],
        out_specs=pl.BlockSpec((1, lc, nq), lambda b: (b, 0, 0)),
        compiler_params=_cparams(("parallel",)),
        name="context_attention",
    )(sink, p_ctx, p_ctx, p_ctx)


def _out_proj_body(ya_ref, yb_ref, wa_ref, wb_ref, h_ref, g_ref, o_ref):
    y = jnp.dot(ya_ref[0].astype(BF16), wa_ref[...], preferred_element_type=F32)
    y = y + jnp.dot(yb_ref[0].astype(BF16), wb_ref[...], preferred_element_type=F32)
    o_ref[0] = h_ref[0] + g_ref[0] * y


def out_proj_residual(ya, yb, w_bf, h, g1):
    bx, lx, d = h.shape
    ca = ya.shape[2]
    cb = yb.shape[2]
    assert ca == cb and w_bf.shape[0] == ca + cb
    tm = _tile(lx, ROW_TILE)
    return pl.pallas_call(
        _out_proj_body,
        out_shape=jax.ShapeDtypeStruct((bx, lx, d), F32),
        grid=(bx, lx // tm),
        in_specs=[pl.BlockSpec((1, tm, ca), lambda b, i: (b, i, 0)),
                  pl.BlockSpec((1, tm, cb), lambda b, i: (b, i, 0)),
                  pl.BlockSpec((ca, d), lambda b, i: (0, 0)),
                  pl.BlockSpec((cb, d), lambda b, i: (1, 0)),
                  pl.BlockSpec((1, tm, d), lambda b, i: (b, i, 0)),
                  pl.BlockSpec((1, 1, d), lambda b, i: (b, 0, 0))],
        out_specs=pl.BlockSpec((1, tm, d), lambda b, i: (b, i, 0)),
        compiler_params=_cparams(("parallel", "parallel")),
        name="out_proj_residual",
    )(ya, yb, w_bf, w_bf, h, g1)


def _gla_consts(c, reverse):
    t = np.arange(c)
    tau = (c - 1 - t) if reverse else t
    mats = [tau[None, :] <= tau[:, None]]
    ups, masks = [], []
    m = 1
    while m < c:
        blk = tau // (2 * m)
        ref = blk * 2 * m + m - 1
        upper = (tau % (2 * m)) >= m
        mats.append(tau[None, :] <= ref[:, None])
        ups.append(upper[:, None])
        masks.append((blk[:, None] == blk[None, :]) & upper[:, None] & (~upper)[None, :])
        m *= 2
    sgn = np.broadcast_to(2.0 * np.stack(ups).astype(np.float32) - 1.0, (len(ups), c, LANES))
    return jnp.asarray(mats[0], BF16), jnp.asarray(sgn, F32), jnp.asarray(np.stack(masks), F32)


def _ref_rows(b, m, reverse):
    c, w = b.shape
    off = m if reverse else m - 1
    if 2 * m >= 8:
        g = b.reshape(c // (2 * m), 2 * m, w)
        return jnp.broadcast_to(g[:, off:off + 1, :], g.shape).reshape(c, w)
    pos = lax.broadcasted_iota(jnp.int32, b.shape, 0) % (2 * m)
    out = b
    for p in range(2 * m):
        if p != off:
            out = jnp.where(pos == p, pltpu.roll(b, (p - off) % c, 0), out)
    return out


def _gla_body(q_ref, f_ref, v_ref, lb_ref, s0_ref, cum_ref, up_ref, mask_ref, *rest, reverse, need_o):
    if need_o:
        o_ref, sout_ref, st_scr, qs_scr, kk_scr, lf_scr = rest
    else:
        sout_ref, st_scr, qs_scr, kk_scr, lf_scr = rest
    s_idx = pl.program_id(1)
    ts = q_ref.shape[1]
    c = GLA_CHUNK
    nch = ts // c
    nlev = up_ref.shape[0]
    dk = HG_DK
    nt = (((1,), (1,)), ((), ()))
    tn = (((0,), (0,)), ((), ()))

    @pl.when(s_idx == 0)
    def _():
        st_scr[...] = s0_ref[0]

    q = q_ref[0]
    lb = lb_ref[0]
    fg = lb + (1.0 - lb) * _sigmoid(f_ref[0])
    qs_scr[...] = q * _sigmoid(q)
    kk_scr[...] = 1.0 - fg
    lf_scr[...] = jnp.log(fg)

    def chunk(i, carry):
        ci = (nch - 1 - i) if reverse else i
        r0 = pl.multiple_of(ci * c, c)
        heads = range(HG_HEADS)
        cols = [slice(hd * dk, (hd + 1) * dk) for hd in heads]
        qc = [qs_scr[pl.ds(r0, c), cs] for cs in cols]
        kc = [kk_scr[pl.ds(r0, c), cs] for cs in cols]
        vc = [v_ref[0, pl.ds(r0, c), cs] for cs in cols]
        b = []
        for cs in cols:
            lf = lf_scr[pl.ds(r0, c), cs]
            l1 = lf.astype(BF16)
            r1 = lf - l1.astype(F32)
            l2 = r1.astype(BF16)
            l3 = (r1 - l2.astype(F32)).astype(BF16)
            sums = jnp.dot(cum_ref[...], jnp.concatenate([l1, l2, l3], axis=1), preferred_element_type=F32)
            b.append(sums[:, :dk] + sums[:, dk:2 * dk] + sums[:, 2 * dk:])
        st = [st_scr[hd] for hd in heads]
        if need_o:
            o = []
            for hd in heads:
                oi = lax.dot_general((qc[hd] * jnp.exp(b[hd])).astype(BF16), st[hd].astype(BF16), nt,
                                     preferred_element_type=F32)
                o.append(oi + jnp.sum(qc[hd] * kc[hd], axis=-1, keepdims=True) * vc[hd])
            a = [jnp.zeros((c, c), F32) for _ in heads]
            for li in range(nlev):
                for hd in heads:
                    e = jnp.exp((b[hd] - _ref_rows(b[hd], 1 << li, reverse)) * up_ref[li])
                    qd = (qc[hd] * e).astype(BF16)
                    kd = (kc[hd] * e).astype(BF16)
                    a[hd] = a[hd] + mask_ref[li] * lax.dot_general(qd, kd, nt, preferred_element_type=F32)
            for hd in heads:
                o_ref[0, pl.ds(r0, c), cols[hd]] = o[hd] + jnp.dot(a[hd].astype(BF16), vc[hd].astype(BF16),
                                                                    preferred_element_type=F32)
        for hd in heads:
            b_end = b[hd][0:1, :] if reverse else b[hd][c - 1:c, :]
            kdec = kc[hd] * jnp.exp(b_end - b[hd])
            upd = lax.dot_general(vc[hd].astype(BF16), kdec.astype(BF16), tn, preferred_element_type=F32)
            st_scr[hd] = st[hd] * jnp.exp(b_end) + upd
        return carry

    lax.fori_loop(0, nch, chunk, 0)

    @pl.when(s_idx == pl.num_programs(1) - 1)
    def _():
        sout_ref[0] = st_scr[...]


def gla_scan(p, lb3, s0, direction, q_col, f_col, v_col, need_o=True):
    bx, lx, _ = p.shape
    hh = HG_HEADS
    dk = HG_DK
    f = hh * dk
    ts = _tile(lx, GLA_SEQ_BLOCK)
    ns = lx // ts
    rev = direction == 1
    sblk = (lambda s: ns - 1 - s) if rev else (lambda s: s)
    cum, up, mask = _gla_consts(GLA_CHUNK, rev)
    const = lambda a: pl.BlockSpec(a.shape, lambda b, s: (0,) * a.ndim)
    o_shape = (jax.ShapeDtypeStruct((bx, lx, f), F32),) if need_o else ()
    o_spec = (pl.BlockSpec((1, ts, f), lambda b, s: (b, sblk(s), 0)),) if need_o else ()
    outs = pl.pallas_call(
        functools.partial(_gla_body, reverse=rev, need_o=need_o),
        out_shape=o_shape + (jax.ShapeDtypeStruct((bx, hh, dk, dk), F32),),
        grid=(bx, ns),
        in_specs=[pl.BlockSpec((1, ts, f), lambda b, s: (b, sblk(s), q_col)),
                  pl.BlockSpec((1, ts, f), lambda b, s: (b, sblk(s), f_col)),
                  pl.BlockSpec((1, ts, f), lambda b, s: (b, sblk(s), v_col)),
                  pl.BlockSpec((1, 1, f), lambda b, s: (direction, 0, 0)),
                  pl.BlockSpec((1, hh, dk, dk), lambda b, s: (b, 0, 0, 0)),
                  const(cum), const(up), const(mask)],
        out_specs=o_spec + (pl.BlockSpec((1, hh, dk, dk), lambda b, s: (b, 0, 0, 0)),),
        scratch_shapes=[pltpu.VMEM((hh, dk, dk), F32), pltpu.VMEM((ts, f), F32),
                        pltpu.VMEM((ts, f), F32), pltpu.VMEM((ts, f), F32)],
        compiler_params=_cparams(("parallel", "arbitrary")),
        name="gla_scan_bwd" if rev else "gla_scan_fwd",
    )(p, p, p, lb3, s0, cum, up, mask)
    return outs if need_o else (None, outs[0])


def _hg_readout_body(of_ref, ob_ref, g_ref, gn_ref, w_ref, h_ref, g1_ref, o_ref):
    o = of_ref[0] + ob_ref[0]
    gn = gn_ref[...]
    pieces = []
    for hh in range(HG_HEADS):
        oh = o[:, hh * HG_DK:(hh + 1) * HG_DK]
        y = oh * lax.rsqrt(jnp.mean(oh * oh, axis=-1, keepdims=True) + EPS)
        pieces.append(y * gn)
    on = jnp.concatenate(pieces, axis=1)
    g = g_ref[0]
    on = on * (g * _sigmoid(g))
    y = jnp.dot(on.astype(BF16), w_ref[...], preferred_element_type=F32)
    o_ref[0] = h_ref[0] + g1_ref[0] * y


def hgrn_readout_residual(o_f, o_b, p, g_col, onorm_row, w_bf, h, g1):
    bx, lx, d = h.shape
    f = o_f.shape[2]
    tm = _tile(lx, ROW_TILE)
    return pl.pallas_call(
        _hg_readout_body,
        out_shape=jax.ShapeDtypeStruct((bx, lx, d), F32),
        grid=(bx, lx // tm),
        in_specs=[pl.BlockSpec((1, tm, f), lambda b, i: (b, i, 0)),
                  pl.BlockSpec((1, tm, f), lambda b, i: (b, i, 0)),
                  pl.BlockSpec((1, tm, f), lambda b, i: (b, i, g_col)),
                  pl.BlockSpec((1, HG_DK), lambda b, i: (0, 0)),
                  pl.BlockSpec((f, d), lambda b, i: (0, 0)),
                  pl.BlockSpec((1, tm, d), lambda b, i: (b, i, 0)),
                  pl.BlockSpec((1, 1, d), lambda b, i: (b, 0, 0))],
        out_specs=pl.BlockSpec((1, tm, d), lambda b, i: (b, i, 0)),
        compiler_params=_cparams(("parallel", "parallel")),
        name="hgrn_readout_residual",
    )(o_f, o_b, p, onorm_row, w_bf, h, g1)


def _pair_rows():
    rows = [(0, b) for b in range(16)] + [(1, b) for b in range(8)]
    for a in range(2, 8):
        rows += [(a, b) for b in range(8)]
    rows += [(a, 0) for a in range(8, 16)]
    return rows


def _top16_rows(s):
    n = s.shape[0]
    ri = lax.broadcasted_iota(jnp.int32, s.shape, 0)
    rank = jnp.full(s.shape, float(PEER_TOPK), F32)
    vals = []
    for a in range(PEER_TOPK):
        m = jnp.max(s, axis=0, keepdims=True)
        first = jnp.min(jnp.where(s == m, ri, n), axis=0, keepdims=True)
        sel = ri == first
        rank = jnp.where(sel, float(a), rank)
        s = jnp.where(sel, -jnp.inf, s)
        vals.append(m)
    return rank, jnp.concatenate(vals, axis=0)


def _pair_candidates(v0, v1, okf):
    blocks = [v0[0:1] + v1, v0[1:2] + v1[0:8]]
    blocks += [v0[a:a + 1] + v1[0:8] for a in range(2, 8)]
    blocks += [v0[8:16] + v1[0:1]]
    return jnp.where(okf > 0.0, jnp.concatenate(blocks, axis=0), -jnp.inf)


def _gate_arrays(s0, s1, rank0, rank1, v0, v1, cand, selm):
    top = cand[0:1]
    z = jnp.sum(selm * jnp.exp(jnp.where(selm > 0.0, cand - top, 0.0)), axis=0, keepdims=True)
    cnt = [jnp.sum(selm[0:16], axis=0, keepdims=True)]
    cnt += [jnp.sum(selm[16 + 8 * (a - 1):24 + 8 * (a - 1)], axis=0, keepdims=True) for a in range(1, 8)]
    cnt8 = selm[72:80]
    crow = jnp.zeros(rank0.shape, F32)
    for a in range(8):
        crow = jnp.where(rank0 == float(a), cnt[a], crow)
    for a in range(8, 16):
        crow = jnp.where(rank0 == float(a), cnt8[a - 8:a - 7], crow)
    av = jnp.where(rank0 < float(PEER_TOPK), jnp.exp(s0 - v0[0:1]), 0.0)
    bn = jnp.where(rank1 < float(PEER_TOPK), jnp.exp(s1 - v1[0:1]), 0.0) / z
    return rank1, bn, crow, av


def _select_exact(s0, s1, flat, okf):
    rank0, v0 = _top16_rows(s0)
    rank1, v1 = _top16_rows(s1)
    cand = _pair_candidates(v0, v1, okf)
    work = cand
    selm = jnp.zeros(cand.shape, F32)
    for _ in range(PEER_TOPK):
        m = jnp.max(work, axis=0, keepdims=True)
        first = jnp.min(jnp.where(work == m, flat, 1 << 20), axis=0, keepdims=True)
        sel = flat == first
        selm = jnp.where(sel, 1.0, selm)
        work = jnp.where(sel, -jnp.inf, work)
    return _gate_arrays(s0, s1, rank0, rank1, v0, v1, cand, selm)


def _cmp_exchange(xs, i, l, larger_first):
    hi = jnp.maximum(xs[i], xs[l])
    lo = jnp.minimum(xs[i], xs[l])
    xs[i], xs[l] = (hi, lo) if larger_first else (lo, hi)


def _bitonic_merge_desc(xs):
    n = len(xs)
    j = n // 2
    while j >= 1:
        for i in range(n):
            if (i ^ j) > i:
                _cmp_exchange(xs, i, i ^ j, True)
        j //= 2


def _sorted_top16(s):
    n = PEER_TOPK
    xs = [s[g * 8:(g + 1) * 8] for g in range(n)]
    k = 2
    while k <= n:
        j = k // 2
        while j >= 1:
            for i in range(n):
                if (i ^ j) > i:
                    _cmp_exchange(xs, i, i ^ j, (i & k) == 0)
            j //= 2
        k *= 2
    for shift in (4, 6, 7):
        other = [pltpu.roll(x, shift, 0) for x in xs]
        xs = [jnp.maximum(xs[i], other[n - 1 - i]) for i in range(n)]
        _bitonic_merge_desc(xs)
    return jnp.concatenate([x[0:1] for x in xs], axis=0)


def _count_greater(s, v):
    r = [v[a:a + 1] for a in range(PEER_TOPK)]
    c8 = r[7] > s
    c4 = jnp.where(c8, r[11], r[3]) > s
    c2 = jnp.where(c8, jnp.where(c4, r[13], r[9]), jnp.where(c4, r[5], r[1])) > s
    p1 = jnp.where(c8,
                   jnp.where(c4, jnp.where(c2, r[14], r[12]), jnp.where(c2, r[10], r[8])),
                   jnp.where(c4, jnp.where(c2, r[6], r[4]), jnp.where(c2, r[2], r[0])))
    c1 = p1 > s
    g = (jnp.where(c8, 8.0, 0.0) + jnp.where(c4, 4.0, 0.0)) + (jnp.where(c2, 2.0, 0.0) + jnp.where(c1, 1.0, 0.0))
    return jnp.where(r[15] > s, float(PEER_TOPK), g)


def _select_fast(s0, s1, okf):
    w = s0.shape[1]
    s = jnp.concatenate([s0, s1], axis=1)
    v = _sorted_top16(s)
    rank = _count_greater(s, v)
    dup = jnp.max(jnp.where(v[0:15] == v[1:16], 1.0, 0.0), axis=0, keepdims=True)
    members = jnp.sum(jnp.where(s >= v[15:16], 1.0, 0.0), axis=0, keepdims=True)
    tie = dup + jnp.where(members != float(PEER_TOPK), 1.0, 0.0)
    v0, rank0, tie0 = v[:, :w], rank[:, :w], tie[:, :w]
    v1, rank1, tie1 = v[:, w:], rank[:, w:], tie[:, w:]
    cand = _pair_candidates(v0, v1, okf)
    work = cand
    m = cand[0:1]
    for _ in range(PEER_TOPK):
        m = jnp.max(work, axis=0, keepdims=True)
        work = jnp.where(work == m, -jnp.inf, work)
    selm = jnp.where(cand >= m, 1.0, 0.0)
    tie2 = jnp.where(jnp.sum(selm, axis=0, keepdims=True) != float(PEER_TOPK), 1.0, 0.0)
    return _gate_arrays(s0, s1, rank0, rank1, v0, v1, cand, selm), tie0 + tie1 + tie2


def _peer_prep_body(x_ref, wq_ref, keys_ref, flat_ref, okf_ref, rank1_ref, bn_ref, crow_ref, av_ref, q_scr):
    q_scr[...] = jnp.dot(wq_ref[...], x_ref[...], preferred_element_type=F32).astype(BF16)
    nk = N_KEYS
    tm = x_ref.shape[1]
    flat = flat_ref[...]
    okf = okf_ref[...]

    def store(h, cs, outs):
        rank1, bn, crow, av = outs
        rank1_ref[h, :, cs] = rank1.astype(BF16)
        bn_ref[h, :, cs] = bn.astype(BF16)
        crow_ref[h, :, cs] = crow
        av_ref[h, :, cs] = av

    def head(h, carry):
        r0 = pl.multiple_of(h * 2 * nk, 2 * nk)
        for c0 in range(0, tm, PEER_PREP_CW):
            cs = slice(c0, c0 + PEER_PREP_CW)
            s0 = jnp.dot(keys_ref[h, 0], q_scr[pl.ds(r0, nk), cs], preferred_element_type=F32)
            s1 = jnp.dot(keys_ref[h, 1], q_scr[pl.ds(r0 + nk, nk), cs], preferred_element_type=F32)
            outs, tie = _select_fast(s0, s1, okf)
            store(h, cs, outs)

            @pl.when(jnp.max(tie) > 0.0)
            def _():
                store(h, cs, _select_exact(s0, s1, flat, okf))
        return carry

    lax.fori_loop(0, PEER_HEADS, head, 0)


def peer_prep(x_t, wq_t, keys_bf):
    d, t = x_t.shape
    tm = _tile(t, PEER_PREP_TOK)
    rows = _pair_rows()
    flat = jnp.asarray([[a * 16 + b] for a, b in rows], jnp.int32)
    okf = jnp.asarray([[1.0 if (a + 1) * (b + 1) <= PEER_TOPK else 0.0] for a, b in rows], F32)
    out = jax.ShapeDtypeStruct((PEER_HEADS, N_KEYS, t), F32)
    out_bf = jax.ShapeDtypeStruct((PEER_HEADS, N_KEYS, t), BF16)
    ospec = pl.BlockSpec((PEER_HEADS, N_KEYS, tm), lambda i: (0, 0, i))
    return pl.pallas_call(
        _peer_prep_body,
        out_shape=(out_bf, out_bf, out, out),
        grid=(t // tm,),
        in_specs=[pl.BlockSpec((d, tm), lambda i: (0, i)),
                  pl.BlockSpec(wq_t.shape, lambda i: (0, 0)),
                  pl.BlockSpec(keys_bf.shape, lambda i: (0, 0, 0, 0)),
                  pl.BlockSpec(flat.shape, lambda i: (0, 0)),
                  pl.BlockSpec(okf.shape, lambda i: (0, 0))],
        out_specs=(ospec, ospec, ospec, ospec),
        scratch_shapes=[pltpu.VMEM((wq_t.shape[0], tm), BF16)],
        compiler_params=_cparams(("parallel",)),
        name="peer_prep",
    )(x_t, wq_t, keys_bf, flat, okf)


def _row_tile_bf16(row):
    tile = jnp.broadcast_to(row, (BF16_SUBLANES, row.shape[1])).astype(BF16)
    return jnp.concatenate([tile] * (N_KEYS // BF16_SUBLANES), axis=0)


def _peer_main_body(x_ref, u_ref, v_ref, vlast_ref, rank1_ref, bn_ref, crow_ref, av_ref, h_ref, g2_ref, fg_ref,
                    o_ref, acc_scr, pt_scr, *, final_norm):
    j = pl.program_id(1)
    nj = pl.num_programs(1)
    tm = x_ref.shape[1]
    nrow = u_ref.shape[0] // N_KEYS
    sub = PEER_SUB_ROWS * N_KEYS
    nsb = nrow // PEER_SUB_ROWS
    x = x_ref[...]
    rd = (j + 1) % 2
    wr = j % 2
    row0 = j * nrow

    @pl.when(j == 0)
    def _():
        acc_scr[...] = jnp.zeros(acc_scr.shape, F32)
        pt_scr[1] = jnp.zeros(pt_scr.shape[1:], BF16)

    def pre_act(sb):
        return jnp.dot(u_ref[sb * sub:(sb + 1) * sub, :], x, preferred_element_type=F32)

    pt_prev = pt_scr[rd]
    dn = v_ref.shape[1] // nsb
    acts, pvs = [], []
    for sb in range(nsb):
        acts.append(pre_act(sb))
        pvs.append(lax.dot_general(pt_prev, v_ref[:, sb * dn:(sb + 1) * dn], (((0,), (0,)), ((), ())),
                                   preferred_element_type=F32))
    for sb in range(nsb):
        act = acts[sb]
        for rr in range(PEER_SUB_ROWS):
            e1 = row0 + sb * PEER_SUB_ROWS + rr
            for c0 in range(0, tm, PEER_CW):
                cs = slice(c0, c0 + PEER_CW)
                w = None
                for hd in range(PEER_HEADS):
                    cr = _row_tile_bf16(crow_ref[hd, pl.ds(e1, 1), cs])
                    ar = _row_tile_bf16(av_ref[hd, pl.ds(e1, 1), cs])
                    t = jnp.where(rank1_ref[hd, :, cs] < cr, bn_ref[hd, :, cs], jnp.zeros((), BF16)) * ar
                    w = t if w is None else w + t
                a = act[rr * N_KEYS:(rr + 1) * N_KEYS, cs]
                gelu = 0.5 * a * (1.0 + lax.erf(a * (2.0 ** -0.5)))
                r0 = (sb * PEER_SUB_ROWS + rr) * N_KEYS
                pt_scr[wr, r0:r0 + N_KEYS, cs] = gelu.astype(BF16) * w
    for sb in range(nsb):
        acc_scr[:, sb * dn:(sb + 1) * dn] += pvs[sb]

    @pl.when(j == nj - 1)
    def _():
        last = lax.dot_general(pt_scr[wr], vlast_ref[...], (((0,), (0,)), ((), ())), preferred_element_type=F32)
        hn = h_ref[...] + g2_ref[0] * (acc_scr[...] + last)
        if final_norm:
            hn = hn * lax.rsqrt(jnp.mean(hn * hn, axis=-1, keepdims=True) + EPS) * fg_ref[...]
        o_ref[...] = hn


def peer_main(x_t, u_bf, v_bf, layer, prep, h2, g2, tokens_per_batch, final_g, final_norm):
    d, t = x_t.shape
    e = u_bf.shape[1]
    tm = _tile(tokens_per_batch, PEER_TOK)
    te = PEER_ROWS * N_KEYS
    assert e % te == 0 and tm % PEER_CW == 0
    tpb = tokens_per_batch // tm
    nj = e // te
    assert PEER_ROWS % PEER_SUB_ROWS == 0 and PEER_ROWS // PEER_SUB_ROWS >= 2
    pspec = pl.BlockSpec((PEER_HEADS, N_KEYS, tm), lambda i, j: (0, 0, i))
    return pl.pallas_call(
        functools.partial(_peer_main_body, final_norm=final_norm),
        out_shape=jax.ShapeDtypeStruct((t, d), F32),
        grid=(t // tm, nj),
        in_specs=[pl.BlockSpec((d, tm), lambda i, j: (0, i)),
                  pl.BlockSpec((None, te, d), lambda i, j: (layer, j, 0)),
                  pl.BlockSpec((None, te, d), lambda i, j: (layer, jnp.maximum(j - 1, 0), 0)),
                  pl.BlockSpec((None, te, d), lambda i, j: (layer, nj - 1, 0)),
                  pspec, pspec, pspec, pspec,
                  pl.BlockSpec((tm, d), lambda i, j: (i, 0)),
                  pl.BlockSpec((1, 1, d), lambda i, j: (i // tpb, 0, 0)),
                  pl.BlockSpec((1, d), lambda i, j: (0, 0))],
        out_specs=pl.BlockSpec((tm, d), lambda i, j: (i, 0)),
        scratch_shapes=[pltpu.VMEM((tm, d), F32), pltpu.VMEM((2, te, tm), BF16)],
        compiler_params=_cparams(("parallel", "arbitrary")),
        name="peer_dense",
    )(x_t, u_bf, v_bf, v_bf, *prep, h2, g2, final_g)


def peer_residual(h, norm_g, sh, sc, g2, wq_t, keys_bf, u_bf, v_bf, layer, final_g, final_norm):
    bx, lx, d = h.shape
    a_t = norm_mod_t(h, norm_g, sh, sc)
    prep = peer_prep(a_t, wq_t, keys_bf)
    span = lx if g2.shape[0] == bx else bx * lx
    out = peer_main(a_t, u_bf, v_bf, layer, prep, h.reshape(bx * lx, d), g2, span, final_g, final_norm)
    return out.reshape(bx, lx, d)


def kernel(x, c, ctx, c_ctx, ada_w, ada_b, norm1_g, norm2_g, final_g, ab_w_in, ab_w_out, hy_conv_w, hy_conv_b,
           hy_filt_w1, hy_filt_b1, hy_filt_w2, hy_filt_b2, hy_filt_w3, hy_filt_freq, hy_skip, attn_sink,
           hg_w_in, hg_w_out, hg_lb_logits, hg_onorm_g, peer_wq, peer_keys, peer_u, peer_v):
    bsz, seq, d = x.shape
    depth = ada_w.shape[0]
    assert depth == 2
    lb_p = jax.nn.softmax(hg_lb_logits.astype(F32), axis=0)
    lb_all = jnp.cumsum(lb_p, axis=0) - lb_p[0:1]

    c16 = jnp.concatenate([c, c_ctx[None], jnp.zeros((16 - bsz - 1, d), F32)], axis=0)
    final_row = final_g[None]
    u_bf = peer_u.astype(BF16)
    v_bf = peer_v.astype(BF16)
    h_lat, h_ctx = x, ctx

    for l in range(depth):
        need_ctx = l < depth - 1
        mod = ada_mod(c16, ada_w, ada_b[:, None, :], l)
        lat = [mod[:bsz, i * d:(i + 1) * d][:, None, :] for i in range(6)]
        cx = [jnp.broadcast_to(mod[bsz:bsz + 1, i * d:(i + 1) * d][:, None, :], (bsz, 1, d)) for i in range(6)]
        n1 = norm1_g[l][None]
        n2 = norm2_g[l][None]
        wq_t = peer_wq[l].T.astype(BF16)
        keys_bf = peer_keys[l].astype(BF16)
        j = l // 2
        if l % 2 == 0:
            w_in = ab_w_in[j].astype(BF16)
            w_out = ab_w_out[j].astype(BF16)
            p_lat = norm_mod_matmul(h_lat, n1, lat[0], lat[1], w_in)
            p_ctx = norm_mod_matmul(h_ctx, n1, cx[0], cx[1], w_in)
            hy = (hy_conv_w[j], hy_conv_b[j][None], hy_filt_w1[j], hy_filt_b1[j], hy_filt_w2[j], hy_filt_b2[j],
                  hy_filt_w3[j], hy_filt_freq[j], hy_skip[j])
            cos, sin = _rope_tables(seq)
            qr, kr = rope_qk(p_lat, cos, sin, 3, 16)
            sink = attn_sink[j]
            at_lat = window_attention(qr, kr, p_lat, p_ctx, sink, 16, 17)
            hy_lat = hyena_mixer(p_lat, *hy)
            h_lat = out_proj_residual(hy_lat, at_lat, w_out, h_lat, lat[2])
            if need_ctx:
                at_ctx = context_attention(p_ctx, sink, 3, 16, 17)
                hy_ctx = hyena_mixer(p_ctx, *hy)
                h_ctx = out_proj_residual(hy_ctx, at_ctx, w_out, h_ctx, cx[2])
        else:
            w_in = hg_w_in[j].astype(BF16)
            w_out = hg_w_out[j].astype(BF16)
            lb3 = lb_all[l].astype(F32)[:, None, :]
            p_lat = norm_mod_matmul(h_lat, n1, lat[0], lat[1], w_in)
            p_ctx = norm_mod_matmul(h_ctx, n1, cx[0], cx[1], w_in)
            s0 = jnp.zeros((bsz, HG_HEADS, HG_DK, HG_DK), F32)
            o_cf, s_cf = gla_scan(p_ctx, lb3, s0, 0, 0, 1, 3, need_o=need_ctx)
            o_cb, s_cb = gla_scan(p_ctx, lb3, s0, 1, 0, 2, 3, need_o=need_ctx)
            o_lf, _ = gla_scan(p_lat, lb3, s_cf, 0, 0, 1, 3)
            o_lb, _ = gla_scan(p_lat, lb3, s_cb, 1, 0, 2, 3)
            onorm = hg_onorm_g[j].astype(F32)[None]
            h_lat = hgrn_readout_residual(o_lf, o_lb, p_lat, 4, onorm, w_out, h_lat, lat[2])
            if need_ctx:
                h_ctx = hgrn_readout_residual(o_cf, o_cb, p_ctx, 4, onorm, w_out, h_ctx, cx[2])
        last = l == depth - 1
        h_lat = peer_residual(h_lat, n2, lat[3], lat[4], lat[5], wq_t, keys_bf, u_bf, v_bf, l, final_row, last)
        if need_ctx:
            h_ctx = peer_residual(h_ctx, n2, cx[3], cx[4], cx[5][:1], wq_t, keys_bf, u_bf, v_bf, l, final_row, False)
    return h_lat
```

```python
import functools
import math

import jax
import jax.numpy as jnp
import numpy as np
from jax import lax
from jax.experimental import pallas as pl
from jax.experimental.pallas import tpu as pltpu

F32 = jnp.float32
BF16 = jnp.bfloat16
EPS = 1e-6

HY_C = 512
HY_EMB = 33
HY_EMB_PAD = 40
HY_FILT_W = 64
HY_DECAY_SHORT_PCT = 0.3
HY_DECAY_LONG_PCT = 1.5
HY_TARGET = 1e-2
HEAD_DIM = 64
N_Q_HEADS = 8
N_KV_HEADS = 2
GQA_GROUP = 4
WINDOW = 128
ATTN_BLK = 128
GRID_W = 64
ROPE_BASE = 10000.0
HG_HEADS = 8
HG_DK = 128
PEER_HEADS = 8
PEER_TOPK = 16
N_KEYS = 128

LANES = 128
BF16_SUBLANES = 16
VMEM_LIMIT_BYTES = 56 * 1024 * 1024
ROW_TILE = 512
PROJ_ROW_TILE = 1024
GLA_CHUNK = 128
GLA_SEQ_BLOCK = 256
PEER_PREP_TOK = 256
PEER_PREP_CW = 256
PEER_TOK = 512
PEER_ROWS = 8
PEER_SUB_ROWS = 2
PEER_CW = 256
NEG_BIG = -1e30


def _cparams(sem):
    return pltpu.CompilerParams(dimension_semantics=sem, vmem_limit_bytes=VMEM_LIMIT_BYTES)


def _sigmoid(x):
    return 1.0 / (1.0 + jnp.exp(-x))


def _tile(n, t):
    t = min(n, t)
    assert n % t == 0, (n, t)
    return t


def _ada_body(c_ref, w_ref, b_ref, o_ref):
    c = c_ref[...]
    s = c * _sigmoid(c)
    o_ref[...] = jnp.dot(s.astype(BF16), w_ref[...].astype(BF16), preferred_element_type=F32) + b_ref[...]


def ada_mod(c16, w_all, b_all, layer):
    _, d, n = w_all.shape
    tn = _tile(n, 1536)
    return pl.pallas_call(
        _ada_body,
        out_shape=jax.ShapeDtypeStruct((c16.shape[0], n), F32),
        grid=(n // tn,),
        in_specs=[pl.BlockSpec(c16.shape, lambda j: (0, 0)),
                  pl.BlockSpec((None, d, tn), lambda j: (layer, 0, j)),
                  pl.BlockSpec((None, 1, tn), lambda j: (layer, 0, j))],
        out_specs=pl.BlockSpec((c16.shape[0], tn), lambda j: (0, j)),
        compiler_params=_cparams(("arbitrary",)),
        name="ada_mod",
    )(c16, w_all, b_all)


def _norm_mod(x, g, sh, sc):
    y = x * lax.rsqrt(jnp.mean(x * x, axis=-1, keepdims=True) + EPS)
    return (y * g) * (1.0 + sc) + sh


def _nm_matmul_body(h_ref, g_ref, sh_ref, sc_ref, w_ref, p_ref, a_scr):
    @pl.when(pl.program_id(2) == 0)
    def _():
        a_scr[...] = _norm_mod(h_ref[0], g_ref[...], sh_ref[0], sc_ref[0]).astype(BF16)

    p_ref[0] = jnp.dot(a_scr[...], w_ref[...], preferred_element_type=F32)


def norm_mod_matmul(h, g, sh, sc, w_bf):
    bx, lx, d = h.shape
    n = w_bf.shape[1]
    tm = _tile(lx, PROJ_ROW_TILE)
    tn = _tile(n, 1280 if n % 1280 == 0 else (1152 if n % 1152 == 0 else 1024))
    return pl.pallas_call(
        _nm_matmul_body,
        out_shape=jax.ShapeDtypeStruct((bx, lx, n), F32),
        grid=(bx, lx // tm, n // tn),
        in_specs=[pl.BlockSpec((1, tm, d), lambda b, i, j: (b, i, 0)),
                  pl.BlockSpec((1, d), lambda b, i, j: (0, 0)),
                  pl.BlockSpec((1, 1, d), lambda b, i, j: (b, 0, 0)),
                  pl.BlockSpec((1, 1, d), lambda b, i, j: (b, 0, 0)),
                  pl.BlockSpec((d, tn), lambda b, i, j: (0, j))],
        out_specs=pl.BlockSpec((1, tm, tn), lambda b, i, j: (b, i, j)),
        scratch_shapes=[pltpu.VMEM((tm, d), BF16)],
        compiler_params=_cparams(("parallel", "parallel", "arbitrary")),
        name="norm_mod_matmul",
    )(h, g, sh, sc, w_bf)


def _nm_only_body(h_ref, g_ref, sh_ref, sc_ref, a_ref):
    a_ref[...] = _norm_mod(h_ref[0], g_ref[...], sh_ref[0], sc_ref[0]).T.astype(BF16)


def norm_mod_t(h, g, sh, sc):
    bx, lx, d = h.shape
    tm = _tile(lx, ROW_TILE)
    nt = lx // tm
    return pl.pallas_call(
        _nm_only_body,
        out_shape=jax.ShapeDtypeStruct((d, bx * lx), BF16),
        grid=(bx, nt),
        in_specs=[pl.BlockSpec((1, tm, d), lambda b, i: (b, i, 0)),
                  pl.BlockSpec((1, d), lambda b, i: (0, 0)),
                  pl.BlockSpec((1, 1, d), lambda b, i: (b, 0, 0)),
                  pl.BlockSpec((1, 1, d), lambda b, i: (b, 0, 0))],
        out_specs=pl.BlockSpec((d, tm), lambda b, i: (0, b * nt + i)),
        compiler_params=_cparams(("parallel", "parallel")),
        name="norm_mod_t",
    )(h, g, sh, sc)


def _mm_body(a_ref, b_ref, o_ref):
    o_ref[...] = jnp.dot(a_ref[...], b_ref[...], preferred_element_type=F32)


def matmul_bf16(a, b):
    m, k = a.shape
    n = b.shape[1]
    tm = _tile(m, 1024)
    tn = _tile(n, 1024)
    return pl.pallas_call(
        _mm_body,
        out_shape=jax.ShapeDtypeStruct((m, n), F32),
        grid=(m // tm, n // tn),
        in_specs=[pl.BlockSpec((tm, k), lambda i, j: (i, 0)),
                  pl.BlockSpec((k, tn), lambda i, j: (0, j))],
        out_specs=pl.BlockSpec((tm, tn), lambda i, j: (i, j)),
        compiler_params=_cparams(("parallel", "parallel")),
        name="matmul_bf16",
    )(a, b)


def _filter_body(z_ref, w1_ref, b1_ref, w2_ref, b2_ref, w3_ref, fr_ref, dec_ref, o_ref):
    hi = lax.Precision.HIGHEST
    fr = fr_ref[...]
    hdn = jnp.sin(fr * (jnp.dot(z_ref[...], w1_ref[...], precision=hi, preferred_element_type=F32) + b1_ref[...]))
    hdn = jnp.sin(fr * (jnp.dot(hdn, w2_ref[...], precision=hi, preferred_element_type=F32) + b2_ref[...]))
    h = jnp.dot(hdn, w3_ref[...], precision=hi, preferred_element_type=F32)
    dec = dec_ref[...]
    c = dec.shape[1]
    h0 = h[:, :c] * dec
    h1 = h[:, c:] * dec
    nrm = jnp.sum(jnp.abs(h0) + jnp.abs(h1), axis=0, keepdims=True)
    inv = 1.0 / nrm
    ri = lax.broadcasted_iota(jnp.int32, h1.shape, 0)
    o_ref[:, :c] = (h0 * inv).astype(BF16)
    o_ref[:, c:] = jnp.where(ri == 0, 0.0, h1 * inv).astype(BF16)


def hyena_filters(zfeat, w1p, b1, w2, b2, w3, freq, decay):
    l = zfeat.shape[0]
    c = decay.shape[1]
    n_order = w3.shape[1] // (2 * c)
    full = lambda shape: pl.BlockSpec(shape, lambda o: (0,) * len(shape))
    return pl.pallas_call(
        _filter_body,
        out_shape=jax.ShapeDtypeStruct((l, n_order * 2 * c), BF16),
        grid=(n_order,),
        in_specs=[full(zfeat.shape), full(w1p.shape), full(b1.shape), full(w2.shape), full(b2.shape),
                  pl.BlockSpec((w3.shape[0], 2 * c), lambda o: (0, o)),
                  full(freq.shape), full(decay.shape)],
        out_specs=pl.BlockSpec((l, 2 * c), lambda o: (0, o)),
        compiler_params=_cparams(("arbitrary",)),
        name="hyena_filters",
    )(zfeat, w1p, b1, w2, b2, w3, freq, decay)


def _short_conv_body(u_ref, w_ref, b_ref, o_ref):
    u = u_ref[0]
    l = u.shape[0]
    ri = lax.broadcasted_iota(jnp.int32, u.shape, 0)
    prev = jnp.where(ri == 0, 0.0, pltpu.roll(u, 1, 0))
    nxt = jnp.where(ri == l - 1, 0.0, pltpu.roll(u, l - 1, 0))
    w = w_ref[...]
    o_ref[0] = prev * w[0:1] + u * w[1:2] + nxt * w[2:3] + b_ref[...]


def short_conv(p, conv_w, conv_b):
    bx, lx, _ = p.shape
    c3 = conv_w.shape[1]
    tc = _tile(c3, 512)
    return pl.pallas_call(
        _short_conv_body,
        out_shape=jax.ShapeDtypeStruct((bx, lx, c3), F32),
        grid=(bx, c3 // tc),
        in_specs=[pl.BlockSpec((1, lx, tc), lambda b, j: (b, 0, j)),
                  pl.BlockSpec((3, tc), lambda b, j: (0, j)),
                  pl.BlockSpec((1, tc), lambda b, j: (0, j))],
        out_specs=pl.BlockSpec((1, lx, tc), lambda b, j: (b, 0, j)),
        compiler_params=_cparams(("parallel", "parallel")),
        name="hyena_short_conv",
    )(p, conv_w, conv_b)


def _hy_fwd_body(z_ref, f_ref, h0_ref, h1_ref, y_ref):
    z = z_ref[0].astype(BF16)
    kt = f_ref.shape[1]
    zre = jnp.dot(f_ref[0], z, preferred_element_type=F32)
    zim = jnp.dot(f_ref[1], z, preferred_element_type=F32)
    k0 = (pl.program_id(0) * kt + lax.broadcasted_iota(jnp.int32, zre.shape, 0)) == 0
    hre = h0_ref[0] + h1_ref[0]
    him = jnp.where(k0, h0_ref[1] + h1_ref[1], h0_ref[1] - h1_ref[1])
    yre = jnp.where(k0, zre * hre, zre * hre - zim * him)
    yim = jnp.where(k0, zim * him, zre * him + zim * hre)
    y_ref[0, 0] = yre.astype(BF16)
    y_ref[0, 1] = yim.astype(BF16)


def hyena_fwd(z_src, z_col, f3, fh3, order, c):
    bx, lx, _ = z_src.shape
    kt = _tile(lx, 512)
    return pl.pallas_call(
        _hy_fwd_body,
        out_shape=jax.ShapeDtypeStruct((bx, 2, lx, c), BF16),
        grid=(lx // kt, bx),
        in_specs=[pl.BlockSpec((1, lx, c), lambda k, b: (b, 0, z_col)),
                  pl.BlockSpec((2, kt, lx), lambda k, b: (0, k, 0)),
                  pl.BlockSpec((2, kt, c), lambda k, b: (0, k, 2 * order)),
                  pl.BlockSpec((2, kt, c), lambda k, b: (0, k, 2 * order + 1))],
        out_specs=pl.BlockSpec((1, 2, kt, c), lambda k, b: (b, 0, k, 0)),
        compiler_params=_cparams(("parallel", "parallel")),
        name="hyena_dft_fwd",
    )(z_src, f3, fh3, fh3)


def _hy_inv_body(y_ref, fi_ref, gate_ref, z_ref, skip_ref, o_ref):
    conv = jnp.dot(fi_ref[...], y_ref[0], preferred_element_type=F32)
    o_ref[0] = gate_ref[0] * (conv + skip_ref[...] * z_ref[0])


def hyena_inv(yf, finv, gate_src, gate_col, z_src, z_col, skip_row):
    bx, l2, c = yf.shape
    lx = l2 // 2
    tl = _tile(lx, 512)
    return pl.pallas_call(
        _hy_inv_body,
        out_shape=jax.ShapeDtypeStruct((bx, lx, c), F32),
        grid=(lx // tl, bx),
        in_specs=[pl.BlockSpec((1, l2, c), lambda t, b: (b, 0, 0)),
                  pl.BlockSpec((tl, l2), lambda t, b: (t, 0)),
                  pl.BlockSpec((1, tl, c), lambda t, b: (b, t, gate_col)),
                  pl.BlockSpec((1, tl, c), lambda t, b: (b, t, z_col)),
                  pl.BlockSpec((1, c), lambda t, b: (0, 0))],
        out_specs=pl.BlockSpec((1, tl, c), lambda t, b: (b, t, 0)),
        compiler_params=_cparams(("parallel", "parallel")),
        name="hyena_dft_inv",
    )(yf, finv, gate_src, z_src, skip_row)


def _dft_mats(l):
    n = 2 * l
    k = jnp.arange(l, dtype=jnp.int32)[:, None]
    s = jnp.arange(l, dtype=jnp.int32)[None, :]

    def table(rows):
        ang = ((rows[:, None] * s) % n).astype(F32) * (2.0 * math.pi / n)
        return jnp.cos(ang), jnp.sin(ang)

    r = 1
    while r * r < l:
        r *= 2
    ch, sh = table(jnp.arange(l // r, dtype=jnp.int32) * r)
    cl, sl = table(jnp.arange(r, dtype=jnp.int32))
    cosm = (ch[:, None, :] * cl[None, :, :] - sh[:, None, :] * sl[None, :, :]).reshape(l, l)
    sinm = -(sh[:, None, :] * cl[None, :, :] + ch[:, None, :] * sl[None, :, :]).reshape(l, l)
    nyq = jnp.where(s % 2 == 0, 1.0, -1.0).astype(F32)
    imag = jnp.where(k == 0, nyq, sinm)
    fwd = jnp.concatenate([cosm, imag], axis=0)
    scale = jnp.where(jnp.arange(l) == 0, 1.0 / n, 2.0 / n).astype(F32)
    inv = jnp.concatenate([cosm.T * scale[None, :], imag.T * scale[None, :]], axis=1)
    return fwd.astype(BF16), inv.astype(BF16)


def _filter_features(l):
    bands = (HY_EMB - 1) // 2
    t = jnp.linspace(0.0, 1.0, l, dtype=F32)[:, None]
    w = 2.0 * math.pi * jnp.arange(l, dtype=F32)[:, None] / l
    f = jnp.linspace(1e-4, bands - 1, bands, dtype=F32)[None, :]
    z = jnp.concatenate([t, jnp.cos(f * w), -jnp.sin(f * w)], axis=-1)
    z = jnp.pad(z, ((0, 0), (0, HY_EMB_PAD - HY_EMB)))
    max_decay = math.log(HY_TARGET) / HY_DECAY_SHORT_PCT
    min_decay = math.log(HY_TARGET) / HY_DECAY_LONG_PCT
    deltas = jnp.abs(jnp.linspace(min_decay, max_decay, HY_C, dtype=F32))
    return z, jnp.exp(-t * deltas)


def hyena_mixer(p, conv_w, conv_b, fw1, fb1, fw2, fb2, fw3, ffreq, d_skip):
    bx, lx, _ = p.shape
    c = HY_C
    zfeat, decay = _filter_features(lx)
    w1p = jnp.pad(fw1, ((0, HY_EMB_PAD - HY_EMB), (0, 0)))
    filt = hyena_filters(zfeat, w1p, fb1[None], fw2, fb2[None], fw3, ffreq[None], decay)
    fwd, inv = _dft_mats(lx)
    fh3 = matmul_bf16(fwd, filt).reshape(2, lx, filt.shape[1])
    f3 = fwd.reshape(2, lx, lx)
    uc = short_conv(p, conv_w, conv_b)
    y0 = hyena_fwd(uc, 0, f3, fh3, 0, c).reshape(bx, 2 * lx, c)
    z1 = hyena_inv(y0, inv, uc, 1, uc, 0, d_skip[0:1])
    y1 = hyena_fwd(z1, 0, f3, fh3, 1, c).reshape(bx, 2 * lx, c)
    return hyena_inv(y1, inv, uc, 2, z1, 0, d_skip[1:2])


def _rope_tables(l):
    rows = l // GRID_W
    r = jnp.broadcast_to(jnp.arange(rows)[:, None], (rows, GRID_W)).reshape(-1).astype(F32)
    col = jnp.broadcast_to(jnp.arange(GRID_W)[None, :], (rows, GRID_W)).reshape(-1).astype(F32)
    nf = HEAD_DIM // 4
    inv = ROPE_BASE ** (-jnp.arange(nf, dtype=F32) / nf)
    ar = r[:, None] * inv
    ac = col[:, None] * inv
    ang = jnp.concatenate([ar, ar, ac, ac], axis=-1)
    sign = jnp.concatenate([-jnp.ones((nf,)), jnp.ones((nf,))] * 2).astype(F32)
    cos = jnp.tile(jnp.cos(ang), (1, N_Q_HEADS))
    sin = jnp.tile(jnp.sin(ang) * sign[None, :], (1, N_Q_HEADS))
    return cos, sin


def _rope_body(q_ref, k_ref, cos_ref, sin_ref, qo_ref, ko_ref):
    def rot(x, cos, sin):
        n = x.shape[1]
        lane = lax.broadcasted_iota(jnp.int32, x.shape, 1)
        first = (lane % 32) < 16
        partner = jnp.where(first, pltpu.roll(x, n - 16, 1), pltpu.roll(x, 16, 1))
        return x * cos + partner * sin

    nk = k_ref.shape[2]
    qo_ref[0] = (rot(q_ref[0], cos_ref[...], sin_ref[...]) * (HEAD_DIM ** -0.5)).astype(BF16)
    ko_ref[0] = rot(k_ref[0], cos_ref[:, :nk], sin_ref[:, :nk]).astype(BF16)


def rope_qk(p, cos, sin, q_col, k_col):
    bx, lx, _ = p.shape
    nq = N_Q_HEADS * HEAD_DIM
    nk = N_KV_HEADS * HEAD_DIM
    tr = _tile(lx, ROW_TILE)
    return pl.pallas_call(
        _rope_body,
        out_shape=(jax.ShapeDtypeStruct((bx, lx, nq), BF16), jax.ShapeDtypeStruct((bx, lx, nk), BF16)),
        grid=(bx, lx // tr),
        in_specs=[pl.BlockSpec((1, tr, nq), lambda b, i: (b, i, q_col)),
                  pl.BlockSpec((1, tr, nk), lambda b, i: (b, i, k_col)),
                  pl.BlockSpec((tr, nq), lambda b, i: (i, 0)),
                  pl.BlockSpec((tr, nq), lambda b, i: (i, 0))],
        out_specs=(pl.BlockSpec((1, tr, nq), lambda b, i: (b, i, 0)),
                   pl.BlockSpec((1, tr, nk), lambda b, i: (b, i, 0))),
        compiler_params=_cparams(("parallel", "parallel")),
        name="rope_qk",
    )(p, p, cos, sin)


def _attn_heads(q, sink_ref, parts_for_head, o_ref):
    t = q.shape[0]
    nt = (((1,), (1,)), ((), ()))
    heads = range(N_KV_HEADS)
    q4, sink, parts = [], [], []
    for hk in heads:
        q4.append(jnp.concatenate(
            [q[:, (hk * GQA_GROUP + g) * HEAD_DIM:(hk * GQA_GROUP + g + 1) * HEAD_DIM] for g in range(GQA_GROUP)],
            axis=0))
        sink.append(jnp.concatenate(
            [jnp.full((t, 1), sink_ref[hk * GQA_GROUP + g], F32) for g in range(GQA_GROUP)], axis=0))
        parts.append(parts_for_head(hk))
    ss = []
    for hk in heads:
        row = []
        for k, _, mask in parts[hk]:
            sc = lax.dot_general(q4[hk], k, nt, preferred_element_type=F32)
            row.append(sc if mask is None else jnp.where(mask, sc, NEG_BIG))
        ss.append(row)
    m = []
    for hk in heads:
        mh = sink[hk]
        for sc in ss[hk]:
            mh = jnp.maximum(mh, jnp.max(sc, axis=-1, keepdims=True))
        m.append(mh)
    es = [[jnp.exp(sc - m[hk]) for sc in ss[hk]] for hk in heads]
    den = []
    for hk in heads:
        dh = jnp.exp(sink[hk] - m[hk])
        for e in es[hk]:
            dh = dh + jnp.sum(e, axis=-1, keepdims=True)
        den.append(dh)
    outs = []
    for hk in heads:
        acc = None
        for e, (_, v, _) in zip(es[hk], parts[hk]):
            o = jnp.dot(e.astype(BF16), v, preferred_element_type=F32)
            acc = o if acc is None else acc + o
        o4 = acc / den[hk]
        outs.extend([o4[g * t:(g + 1) * t] for g in range(GQA_GROUP)])
    o_ref[0] = jnp.concatenate(outs, axis=1)


def _win_attn_body(sink_ref, q_ref, k0_ref, k1_ref, k2_ref, v0_ref, v1_ref, v2_ref, kc_ref, vc_ref, o_ref, *, seq_len):
    n = pl.program_id(1)
    blk = q_ref.shape[1]
    lc = kc_ref.shape[1]
    ka = jnp.concatenate([k0_ref[0], k1_ref[0], k2_ref[0], kc_ref[0].astype(BF16)], axis=0)
    va = jnp.concatenate([v0_ref[0], v1_ref[0], v2_ref[0], vc_ref[0]], axis=0).astype(BF16)
    qi = lax.broadcasted_iota(jnp.int32, (blk, 3 * blk + lc), 0)
    kj = lax.broadcasted_iota(jnp.int32, (blk, 3 * blk + lc), 1)
    kpos = (n - 1) * blk + kj
    diff = qi + blk - kj
    valid = (kj >= 3 * blk) | ((kpos >= 0) & (kpos < seq_len) & (diff <= WINDOW) & (diff >= -WINDOW))
    valid4 = jnp.concatenate([valid] * GQA_GROUP, axis=0)

    def parts(hk):
        sl = slice(hk * HEAD_DIM, (hk + 1) * HEAD_DIM)
        return [(ka[:, sl], va[:, sl], valid4)]

    _attn_heads(q_ref[0], sink_ref, parts, o_ref)


def window_attention(qr, kr, p, p_ctx, sink, k_col, v_col):
    bx, lx, nq = qr.shape
    nk = kr.shape[2]
    lc = p_ctx.shape[1]
    blk = ATTN_BLK
    nb = lx // blk
    lo = lambda b, n: (b, jnp.maximum(n - 1, 0), 0)
    mid = lambda b, n: (b, n, 0)
    hi = lambda b, n: (b, jnp.minimum(n + 1, nb - 1), 0)
    vlo = lambda b, n: (b, jnp.maximum(n - 1, 0), v_col)
    vmid = lambda b, n: (b, n, v_col)
    vhi = lambda b, n: (b, jnp.minimum(n + 1, nb - 1), v_col)
    return pl.pallas_call(
        functools.partial(_win_attn_body, seq_len=lx),
        out_shape=jax.ShapeDtypeStruct((bx, lx, nq), F32),
        grid=(bx, nb),
        in_specs=[pl.BlockSpec(memory_space=pltpu.SMEM),
                  pl.BlockSpec((1, blk, nq), mid),
                  pl.BlockSpec((1, blk, nk), lo), pl.BlockSpec((1, blk, nk), mid), pl.BlockSpec((1, blk, nk), hi),
                  pl.BlockSpec((1, blk, nk), vlo), pl.BlockSpec((1, blk, nk), vmid), pl.BlockSpec((1, blk, nk), vhi),
                  pl.BlockSpec((1, lc, nk), lambda b, n: (b, 0, k_col)),
                  pl.BlockSpec((1, lc, nk), lambda b, n: (b, 0, v_col))],
        out_specs=pl.BlockSpec((1, blk, nq), mid),
        compiler_params=_cparams(("parallel", "parallel")),
        name="window_attention",
    )(sink, qr, kr, kr, kr, p, p, p, p_ctx, p_ctx)


def _ctx_attn_body(sink_ref, q_ref, kc_ref, vc_ref, o_ref):
    kc = kc_ref[0].astype(BF16)
    vc = vc_ref[0].astype(BF16)
    q = (q_ref[0] * (HEAD_DIM ** -0.5)).astype(BF16)

    def parts(hk):
        sl = slice(hk * HEAD_DIM, (hk + 1) * HEAD_DIM)
        return [(kc[:, sl], vc[:, sl], None)]

    _attn_heads(q, sink_ref, parts, o_ref)


def context_attention(p_ctx, sink, q_col, k_col, v_col):
    bx, lc, _ = p_ctx.shape
    nq = N_Q_HEADS * HEAD_DIM
    nk = N_KV_HEADS * HEAD_DIM
    return pl.pallas_call(
        _ctx_attn_body,
        out_shape=jax.ShapeDtypeStruct((bx, lc, nq), F32),
        grid=(bx,),
        in_specs=[pl.BlockSpec(memory_space=pltpu.SMEM),
                  pl.BlockSpec((1, lc, nq), lambda b: (b, 0, q_col)),
                  pl.BlockSpec((1, lc, nk), lambda b: (b, 0, k_col)),
                  pl.BlockSpec((1, lc, nk), lambda b: (b, 0, v_col))],
        out_specs=pl.BlockSpec((1, lc, nq), lambda b: (b, 0, 0)),
        compiler_params=_cparams(("parallel",)),
        name="context_attention",
    )(sink, p_ctx, p_ctx, p_ctx)


def _next_norm_specs(d, tm, nt, bx, lx):
    ins = [pl.BlockSpec((1, d), lambda b, i: (0, 0)),
           pl.BlockSpec((1, 1, d), lambda b, i: (b, 0, 0)),
           pl.BlockSpec((1, 1, d), lambda b, i: (b, 0, 0))]
    return ins, jax.ShapeDtypeStruct((d, bx * lx), BF16), pl.BlockSpec((d, tm), lambda b, i: (0, b * nt + i))


def _out_proj_body(ya_ref, yb_ref, wa_ref, wb_ref, h_ref, g_ref, n2_ref, sh2_ref, sc2_ref, o_ref, at_ref):
    y = jnp.dot(ya_ref[0].astype(BF16), wa_ref[...], preferred_element_type=F32)
    y = y + jnp.dot(yb_ref[0].astype(BF16), wb_ref[...], preferred_element_type=F32)
    hn = h_ref[0] + g_ref[0] * y
    o_ref[0] = hn
    at_ref[...] = _norm_mod(hn, n2_ref[...], sh2_ref[0], sc2_ref[0]).T.astype(BF16)


def out_proj_residual(ya, yb, w_bf, h, g1, n2, sh2, sc2):
    bx, lx, d = h.shape
    ca = ya.shape[2]
    cb = yb.shape[2]
    assert ca == cb and w_bf.shape[0] == ca + cb
    tm = _tile(lx, ROW_TILE)
    nin, at_shape, at_spec = _next_norm_specs(d, tm, lx // tm, bx, lx)
    return pl.pallas_call(
        _out_proj_body,
        out_shape=(jax.ShapeDtypeStruct((bx, lx, d), F32), at_shape),
        grid=(bx, lx // tm),
        in_specs=[pl.BlockSpec((1, tm, ca), lambda b, i: (b, i, 0)),
                  pl.BlockSpec((1, tm, cb), lambda b, i: (b, i, 0)),
                  pl.BlockSpec((ca, d), lambda b, i: (0, 0)),
                  pl.BlockSpec((cb, d), lambda b, i: (1, 0)),
                  pl.BlockSpec((1, tm, d), lambda b, i: (b, i, 0)),
                  pl.BlockSpec((1, 1, d), lambda b, i: (b, 0, 0))] + nin,
        out_specs=(pl.BlockSpec((1, tm, d), lambda b, i: (b, i, 0)), at_spec),
        compiler_params=_cparams(("parallel", "parallel")),
        name="out_proj_residual",
    )(ya, yb, w_bf, w_bf, h, g1, n2, sh2, sc2)


def _gla_consts(c, reverse):
    t = np.arange(c)
    tau = (c - 1 - t) if reverse else t
    mats = [tau[None, :] <= tau[:, None]]
    ups, masks = [], []
    m = 1
    while m < c:
        blk = tau // (2 * m)
        ref = blk * 2 * m + m - 1
        upper = (tau % (2 * m)) >= m
        mats.append(tau[None, :] <= ref[:, None])
        ups.append(upper[:, None])
        masks.append((blk[:, None] == blk[None, :]) & upper[:, None] & (~upper)[None, :])
        m *= 2
    sgn = np.broadcast_to(2.0 * np.stack(ups).astype(np.float32) - 1.0, (len(ups), c, LANES))
    return jnp.asarray(mats[0], BF16), jnp.asarray(sgn, F32), jnp.asarray(np.stack(masks), F32)


def _ref_rows(b, m, reverse):
    c, w = b.shape
    off = m if reverse else m - 1
    if 2 * m >= 8:
        g = b.reshape(c // (2 * m), 2 * m, w)
        return jnp.broadcast_to(g[:, off:off + 1, :], g.shape).reshape(c, w)
    pos = lax.broadcasted_iota(jnp.int32, b.shape, 0) % (2 * m)
    out = b
    for p in range(2 * m):
        if p != off:
            out = jnp.where(pos == p, pltpu.roll(b, (p - off) % c, 0), out)
    return out


def _gla_body(q_ref, f_ref, v_ref, lb_ref, s0_ref, cum_ref, up_ref, mask_ref, *rest, reverse, need_o):
    if need_o:
        o_ref, sout_ref, st_scr, qs_scr, kk_scr, lf_scr = rest
    else:
        sout_ref, st_scr, qs_scr, kk_scr, lf_scr = rest
    s_idx = pl.program_id(1)
    ts = q_ref.shape[1]
    c = GLA_CHUNK
    nch = ts // c
    nlev = up_ref.shape[0]
    dk = HG_DK
    nt = (((1,), (1,)), ((), ()))
    tn = (((0,), (0,)), ((), ()))

    @pl.when(s_idx == 0)
    def _():
        st_scr[...] = s0_ref[0]

    q = q_ref[0]
    lb = lb_ref[0]
    fg = lb + (1.0 - lb) * _sigmoid(f_ref[0])
    qs_scr[...] = q * _sigmoid(q)
    kk_scr[...] = 1.0 - fg
    lf_scr[...] = jnp.log(fg)

    def chunk(i, carry):
        ci = (nch - 1 - i) if reverse else i
        r0 = pl.multiple_of(ci * c, c)
        heads = range(HG_HEADS)
        cols = [slice(hd * dk, (hd + 1) * dk) for hd in heads]
        qc = [qs_scr[pl.ds(r0, c), cs] for cs in cols]
        kc = [kk_scr[pl.ds(r0, c), cs] for cs in cols]
        vc = [v_ref[0, pl.ds(r0, c), cs] for cs in cols]
        b = []
        for cs in cols:
            lf = lf_scr[pl.ds(r0, c), cs]
            l1 = lf.astype(BF16)
            r1 = lf - l1.astype(F32)
            l2 = r1.astype(BF16)
            l3 = (r1 - l2.astype(F32)).astype(BF16)
            sums = jnp.dot(cum_ref[...], jnp.concatenate([l1, l2, l3], axis=1), preferred_element_type=F32)
            b.append(sums[:, :dk] + sums[:, dk:2 * dk] + sums[:, 2 * dk:])
        st = [st_scr[hd] for hd in heads]
        if need_o:
            o = []
            for hd in heads:
                oi = lax.dot_general((qc[hd] * jnp.exp(b[hd])).astype(BF16), st[hd].astype(BF16), nt,
                                     preferred_element_type=F32)
                o.append(oi + jnp.sum(qc[hd] * kc[hd], axis=-1, keepdims=True) * vc[hd])
            a = [jnp.zeros((c, c), F32) for _ in heads]
            for li in range(nlev):
                for hd in heads:
                    e = jnp.exp((b[hd] - _ref_rows(b[hd], 1 << li, reverse)) * up_ref[li])
                    qd = (qc[hd] * e).astype(BF16)
                    kd = (kc[hd] * e).astype(BF16)
                    a[hd] = a[hd] + mask_ref[li] * lax.dot_general(qd, kd, nt, preferred_element_type=F32)
            for hd in heads:
                o_ref[0, pl.ds(r0, c), cols[hd]] = o[hd] + jnp.dot(a[hd].astype(BF16), vc[hd].astype(BF16),
                                                                    preferred_element_type=F32)
        for hd in heads:
            b_end = b[hd][0:1, :] if reverse else b[hd][c - 1:c, :]
            kdec = kc[hd] * jnp.exp(b_end - b[hd])
            upd = lax.dot_general(vc[hd].astype(BF16), kdec.astype(BF16), tn, preferred_element_type=F32)
            st_scr[hd] = st[hd] * jnp.exp(b_end) + upd
        return carry

    lax.fori_loop(0, nch, chunk, 0)

    @pl.when(s_idx == pl.num_programs(1) - 1)
    def _():
        sout_ref[0] = st_scr[...]


def gla_scan(p, lb3, s0, direction, q_col, f_col, v_col, need_o=True):
    bx, lx, _ = p.shape
    hh = HG_HEADS
    dk = HG_DK
    f = hh * dk
    ts = _tile(lx, GLA_SEQ_BLOCK)
    ns = lx // ts
    rev = direction == 1
    sblk = (lambda s: ns - 1 - s) if rev else (lambda s: s)
    cum, up, mask = _gla_consts(GLA_CHUNK, rev)
    const = lambda a: pl.BlockSpec(a.shape, lambda b, s: (0,) * a.ndim)
    o_shape = (jax.ShapeDtypeStruct((bx, lx, f), F32),) if need_o else ()
    o_spec = (pl.BlockSpec((1, ts, f), lambda b, s: (b, sblk(s), 0)),) if need_o else ()
    outs = pl.pallas_call(
        functools.partial(_gla_body, reverse=rev, need_o=need_o),
        out_shape=o_shape + (jax.ShapeDtypeStruct((bx, hh, dk, dk), F32),),
        grid=(bx, ns),
        in_specs=[pl.BlockSpec((1, ts, f), lambda b, s: (b, sblk(s), q_col)),
                  pl.BlockSpec((1, ts, f), lambda b, s: (b, sblk(s), f_col)),
                  pl.BlockSpec((1, ts, f), lambda b, s: (b, sblk(s), v_col)),
                  pl.BlockSpec((1, 1, f), lambda b, s: (direction, 0, 0)),
                  pl.BlockSpec((1, hh, dk, dk), lambda b, s: (b, 0, 0, 0)),
                  const(cum), const(up), const(mask)],
        out_specs=o_spec + (pl.BlockSpec((1, hh, dk, dk), lambda b, s: (b, 0, 0, 0)),),
        scratch_shapes=[pltpu.VMEM((hh, dk, dk), F32), pltpu.VMEM((ts, f), F32),
                        pltpu.VMEM((ts, f), F32), pltpu.VMEM((ts, f), F32)],
        compiler_params=_cparams(("parallel", "arbitrary")),
        name="gla_scan_bwd" if rev else "gla_scan_fwd",
    )(p, p, p, lb3, s0, cum, up, mask)
    return outs if need_o else (None, outs[0])


def _hg_readout_body(of_ref, ob_ref, g_ref, gn_ref, w_ref, h_ref, g1_ref, n2_ref, sh2_ref, sc2_ref, o_ref, at_ref):
    o = of_ref[0] + ob_ref[0]
    gn = gn_ref[...]
    pieces = []
    for hh in range(HG_HEADS):
        oh = o[:, hh * HG_DK:(hh + 1) * HG_DK]
        y = oh * lax.rsqrt(jnp.mean(oh * oh, axis=-1, keepdims=True) + EPS)
        pieces.append(y * gn)
    on = jnp.concatenate(pieces, axis=1)
    g = g_ref[0]
    on = on * (g * _sigmoid(g))
    y = jnp.dot(on.astype(BF16), w_ref[...], preferred_element_type=F32)
    hn = h_ref[0] + g1_ref[0] * y
    o_ref[0] = hn
    at_ref[...] = _norm_mod(hn, n2_ref[...], sh2_ref[0], sc2_ref[0]).T.astype(BF16)


def hgrn_readout_residual(o_f, o_b, p, g_col, onorm_row, w_bf, h, g1, n2, sh2, sc2):
    bx, lx, d = h.shape
    f = o_f.shape[2]
    tm = _tile(lx, ROW_TILE)
    nin, at_shape, at_spec = _next_norm_specs(d, tm, lx // tm, bx, lx)
    return pl.pallas_call(
        _hg_readout_body,
        out_shape=(jax.ShapeDtypeStruct((bx, lx, d), F32), at_shape),
        grid=(bx, lx // tm),
        in_specs=[pl.BlockSpec((1, tm, f), lambda b, i: (b, i, 0)),
                  pl.BlockSpec((1, tm, f), lambda b, i: (b, i, 0)),
                  pl.BlockSpec((1, tm, f), lambda b, i: (b, i, g_col)),
                  pl.BlockSpec((1, HG_DK), lambda b, i: (0, 0)),
                  pl.BlockSpec((f, d), lambda b, i: (0, 0)),
                  pl.BlockSpec((1, tm, d), lambda b, i: (b, i, 0)),
                  pl.BlockSpec((1, 1, d), lambda b, i: (b, 0, 0))] + nin,
        out_specs=(pl.BlockSpec((1, tm, d), lambda b, i: (b, i, 0)), at_spec),
        compiler_params=_cparams(("parallel", "parallel")),
        name="hgrn_readout_residual",
    )(o_f, o_b, p, onorm_row, w_bf, h, g1, n2, sh2, sc2)


def _pair_rows():
    rows = [(0, b) for b in range(16)] + [(1, b) for b in range(8)]
    for a in range(2, 8):
        rows += [(a, b) for b in range(8)]
    rows += [(a, 0) for a in range(8, 16)]
    return rows


def _top16_rows(s):
    n = s.shape[0]
    ri = lax.broadcasted_iota(jnp.int32, s.shape, 0)
    rank = jnp.full(s.shape, float(PEER_TOPK), F32)
    vals = []
    for a in range(PEER_TOPK):
        m = jnp.max(s, axis=0, keepdims=True)
        first = jnp.min(jnp.where(s == m, ri, n), axis=0, keepdims=True)
        sel = ri == first
        rank = jnp.where(sel, float(a), rank)
        s = jnp.where(sel, -jnp.inf, s)
        vals.append(m)
    return rank, jnp.concatenate(vals, axis=0)


def _pair_candidates(v0, v1, okf):
    blocks = [v0[0:1] + v1, v0[1:2] + v1[0:8]]
    blocks += [v0[a:a + 1] + v1[0:8] for a in range(2, 8)]
    blocks += [v0[8:16] + v1[0:1]]
    return jnp.where(okf > 0.0, jnp.concatenate(blocks, axis=0), -jnp.inf)


def _gate_arrays(s0, s1, rank0, rank1, v0, v1, cand, selm):
    top = cand[0:1]
    z = jnp.sum(selm * jnp.exp(jnp.where(selm > 0.0, cand - top, 0.0)), axis=0, keepdims=True)
    cnt = [jnp.sum(selm[0:16], axis=0, keepdims=True)]
    cnt += [jnp.sum(selm[16 + 8 * (a - 1):24 + 8 * (a - 1)], axis=0, keepdims=True) for a in range(1, 8)]
    cnt8 = selm[72:80]
    crow = jnp.zeros(rank0.shape, F32)
    for a in range(8):
        crow = jnp.where(rank0 == float(a), cnt[a], crow)
    for a in range(8, 16):
        crow = jnp.where(rank0 == float(a), cnt8[a - 8:a - 7], crow)
    av = jnp.where(rank0 < float(PEER_TOPK), jnp.exp(s0 - v0[0:1]), 0.0)
    bn = jnp.where(rank1 < float(PEER_TOPK), jnp.exp(s1 - v1[0:1]), 0.0) / z
    return rank1, bn, crow, av


def _select_exact(s0, s1, flat, okf):
    rank0, v0 = _top16_rows(s0)
    rank1, v1 = _top16_rows(s1)
    cand = _pair_candidates(v0, v1, okf)
    work = cand
    selm = jnp.zeros(cand.shape, F32)
    for _ in range(PEER_TOPK):
        m = jnp.max(work, axis=0, keepdims=True)
        first = jnp.min(jnp.where(work == m, flat, 1 << 20), axis=0, keepdims=True)
        sel = flat == first
        selm = jnp.where(sel, 1.0, selm)
        work = jnp.where(sel, -jnp.inf, work)
    return _gate_arrays(s0, s1, rank0, rank1, v0, v1, cand, selm)


def _cmp_exchange(xs, i, l, larger_first):
    hi = jnp.maximum(xs[i], xs[l])
    lo = jnp.minimum(xs[i], xs[l])
    xs[i], xs[l] = (hi, lo) if larger_first else (lo, hi)


def _bitonic_merge_desc(xs):
    n = len(xs)
    j = n // 2
    while j >= 1:
        for i in range(n):
            if (i ^ j) > i:
                _cmp_exchange(xs, i, i ^ j, True)
        j //= 2


def _sorted_top16(s):
    n = PEER_TOPK
    xs = [s[g * 8:(g + 1) * 8] for g in range(n)]
    k = 2
    while k <= n:
        j = k // 2
        while j >= 1:
            for i in range(n):
                if (i ^ j) > i:
                    _cmp_exchange(xs, i, i ^ j, (i & k) == 0)
            j //= 2
        k *= 2
    for shift in (4, 6, 7):
        other = [pltpu.roll(x, shift, 0) for x in xs]
        xs = [jnp.maximum(xs[i], other[n - 1 - i]) for i in range(n)]
        _bitonic_merge_desc(xs)
    return jnp.concatenate([x[0:1] for x in xs], axis=0)


def _count_greater(s, v):
    r = [v[a:a + 1] for a in range(PEER_TOPK)]
    c8 = r[7] > s
    c4 = jnp.where(c8, r[11], r[3]) > s
    c2 = jnp.where(c8, jnp.where(c4, r[13], r[9]), jnp.where(c4, r[5], r[1])) > s
    p1 = jnp.where(c8,
                   jnp.where(c4, jnp.where(c2, r[14], r[12]), jnp.where(c2, r[10], r[8])),
                   jnp.where(c4, jnp.where(c2, r[6], r[4]), jnp.where(c2, r[2], r[0])))
    c1 = p1 > s
    g = (jnp.where(c8, 8.0, 0.0) + jnp.where(c4, 4.0, 0.0)) + (jnp.where(c2, 2.0, 0.0) + jnp.where(c1, 1.0, 0.0))
    return jnp.where(r[15] > s, float(PEER_TOPK), g)


def _select_fast(s0, s1, okf):
    w = s0.shape[1]
    s = jnp.concatenate([s0, s1], axis=1)
    v = _sorted_top16(s)
    rank = _count_greater(s, v)
    dup = jnp.max(jnp.where(v[0:15] == v[1:16], 1.0, 0.0), axis=0, keepdims=True)
    members = jnp.sum(jnp.where(s >= v[15:16], 1.0, 0.0), axis=0, keepdims=True)
    tie = dup + jnp.where(members != float(PEER_TOPK), 1.0, 0.0)
    v0, rank0, tie0 = v[:, :w], rank[:, :w], tie[:, :w]
    v1, rank1, tie1 = v[:, w:], rank[:, w:], tie[:, w:]
    cand = _pair_candidates(v0, v1, okf)
    work = cand
    m = cand[0:1]
    for _ in range(PEER_TOPK):
        m = jnp.max(work, axis=0, keepdims=True)
        work = jnp.where(work == m, -jnp.inf, work)
    selm = jnp.where(cand >= m, 1.0, 0.0)
    tie2 = jnp.where(jnp.sum(selm, axis=0, keepdims=True) != float(PEER_TOPK), 1.0, 0.0)
    return _gate_arrays(s0, s1, rank0, rank1, v0, v1, cand, selm), tie0 + tie1 + tie2


def _peer_prep_body(x_ref, wq_ref, keys_ref, flat_ref, okf_ref, rank1_ref, bn_ref, crow_ref, av_ref, q_scr):
    q_scr[...] = jnp.dot(wq_ref[...], x_ref[...], preferred_element_type=F32).astype(BF16)
    nk = N_KEYS
    tm = x_ref.shape[1]
    flat = flat_ref[...]
    okf = okf_ref[...]

    def store(h, cs, outs):
        rank1, bn, crow, av = outs
        rank1_ref[h, :, cs] = rank1.astype(BF16)
        bn_ref[h, :, cs] = bn.astype(BF16)
        crow_ref[h, :, cs] = crow
        av_ref[h, :, cs] = av

    def head(h, carry):
        r0 = pl.multiple_of(h * 2 * nk, 2 * nk)
        for c0 in range(0, tm, PEER_PREP_CW):
            cs = slice(c0, c0 + PEER_PREP_CW)
            s0 = jnp.dot(keys_ref[h, 0], q_scr[pl.ds(r0, nk), cs], preferred_element_type=F32)
            s1 = jnp.dot(keys_ref[h, 1], q_scr[pl.ds(r0 + nk, nk), cs], preferred_element_type=F32)
            outs, tie = _select_fast(s0, s1, okf)
            store(h, cs, outs)

            @pl.when(jnp.max(tie) > 0.0)
            def _():
                store(h, cs, _select_exact(s0, s1, flat, okf))
        return carry

    lax.fori_loop(0, PEER_HEADS, head, 0)


def peer_prep(x_t, wq_t, keys_bf):
    d, t = x_t.shape
    tm = _tile(t, PEER_PREP_TOK)
    rows = _pair_rows()
    flat = jnp.asarray([[a * 16 + b] for a, b in rows], jnp.int32)
    okf = jnp.asarray([[1.0 if (a + 1) * (b + 1) <= PEER_TOPK else 0.0] for a, b in rows], F32)
    out = jax.ShapeDtypeStruct((PEER_HEADS, N_KEYS, t), F32)
    out_bf = jax.ShapeDtypeStruct((PEER_HEADS, N_KEYS, t), BF16)
    ospec = pl.BlockSpec((PEER_HEADS, N_KEYS, tm), lambda i: (0, 0, i))
    return pl.pallas_call(
        _peer_prep_body,
        out_shape=(out_bf, out_bf, out, out),
        grid=(t // tm,),
        in_specs=[pl.BlockSpec((d, tm), lambda i: (0, i)),
                  pl.BlockSpec(wq_t.shape, lambda i: (0, 0)),
                  pl.BlockSpec(keys_bf.shape, lambda i: (0, 0, 0, 0)),
                  pl.BlockSpec(flat.shape, lambda i: (0, 0)),
                  pl.BlockSpec(okf.shape, lambda i: (0, 0))],
        out_specs=(ospec, ospec, ospec, ospec),
        scratch_shapes=[pltpu.VMEM((wq_t.shape[0], tm), BF16)],
        compiler_params=_cparams(("parallel",)),
        name="peer_prep",
    )(x_t, wq_t, keys_bf, flat, okf)


def _row_tile_bf16(row):
    tile = jnp.broadcast_to(row, (BF16_SUBLANES, row.shape[1])).astype(BF16)
    return jnp.concatenate([tile] * (N_KEYS // BF16_SUBLANES), axis=0)


def _peer_main_body(x_ref, u_ref, v_ref, vlast_ref, rank1_ref, bn_ref, crow_ref, av_ref, h_ref, g2_ref, fg_ref,
                    o_ref, acc_scr, pt_scr, *, final_norm):
    j = pl.program_id(1)
    nj = pl.num_programs(1)
    tm = x_ref.shape[1]
    nrow = u_ref.shape[0] // N_KEYS
    sub = PEER_SUB_ROWS * N_KEYS
    nsb = nrow // PEER_SUB_ROWS
    x = x_ref[...]
    rd = (j + 1) % 2
    wr = j % 2
    row0 = j * nrow

    @pl.when(j == 0)
    def _():
        acc_scr[...] = jnp.zeros(acc_scr.shape, F32)
        pt_scr[1] = jnp.zeros(pt_scr.shape[1:], BF16)

    def pre_act(sb):
        return jnp.dot(u_ref[sb * sub:(sb + 1) * sub, :], x, preferred_element_type=F32)

    pt_prev = pt_scr[rd]
    dn = v_ref.shape[1] // nsb
    acts, pvs = [], []
    for sb in range(nsb):
        acts.append(pre_act(sb))
        pvs.append(lax.dot_general(pt_prev, v_ref[:, sb * dn:(sb + 1) * dn], (((0,), (0,)), ((), ())),
                                   preferred_element_type=F32))
    for sb in range(nsb):
        act = acts[sb]
        for rr in range(PEER_SUB_ROWS):
            e1 = row0 + sb * PEER_SUB_ROWS + rr
            for c0 in range(0, tm, PEER_CW):
                cs = slice(c0, c0 + PEER_CW)
                w = None
                for hd in range(PEER_HEADS):
                    cr = _row_tile_bf16(crow_ref[hd, pl.ds(e1, 1), cs])
                    ar = _row_tile_bf16(av_ref[hd, pl.ds(e1, 1), cs])
                    t = jnp.where(rank1_ref[hd, :, cs] < cr, bn_ref[hd, :, cs], jnp.zeros((), BF16)) * ar
                    w = t if w is None else w + t
                a = act[rr * N_KEYS:(rr + 1) * N_KEYS, cs]
                gelu = 0.5 * a * (1.0 + lax.erf(a * (2.0 ** -0.5)))
                r0 = (sb * PEER_SUB_ROWS + rr) * N_KEYS
                pt_scr[wr, r0:r0 + N_KEYS, cs] = gelu.astype(BF16) * w
    for sb in range(nsb):
        acc_scr[:, sb * dn:(sb + 1) * dn] += pvs[sb]

    @pl.when(j == nj - 1)
    def _():
        last = lax.dot_general(pt_scr[wr], vlast_ref[...], (((0,), (0,)), ((), ())), preferred_element_type=F32)
        hn = h_ref[...] + g2_ref[0] * (acc_scr[...] + last)
        if final_norm:
            hn = hn * lax.rsqrt(jnp.mean(hn * hn, axis=-1, keepdims=True) + EPS) * fg_ref[...]
        o_ref[...] = hn


def peer_main(x_t, u_bf, v_bf, layer, prep, h2, g2, tokens_per_batch, final_g, final_norm):
    d, t = x_t.shape
    e = u_bf.shape[1]
    tm = _tile(tokens_per_batch, PEER_TOK)
    te = PEER_ROWS * N_KEYS
    assert e % te == 0 and tm % PEER_CW == 0
    tpb = tokens_per_batch // tm
    nj = e // te
    assert PEER_ROWS % PEER_SUB_ROWS == 0 and PEER_ROWS // PEER_SUB_ROWS >= 2
    pspec = pl.BlockSpec((PEER_HEADS, N_KEYS, tm), lambda i, j: (0, 0, i))
    return pl.pallas_call(
        functools.partial(_peer_main_body, final_norm=final_norm),
        out_shape=jax.ShapeDtypeStruct((t, d), F32),
        grid=(t // tm, nj),
        in_specs=[pl.BlockSpec((d, tm), lambda i, j: (0, i)),
                  pl.BlockSpec((None, te, d), lambda i, j: (layer, j, 0)),
                  pl.BlockSpec((None, te, d), lambda i, j: (layer, jnp.maximum(j - 1, 0), 0)),
                  pl.BlockSpec((None, te, d), lambda i, j: (layer, nj - 1, 0)),
                  pspec, pspec, pspec, pspec,
                  pl.BlockSpec((tm, d), lambda i, j: (i, 0)),
                  pl.BlockSpec((1, 1, d), lambda i, j: (i // tpb, 0, 0)),
                  pl.BlockSpec((1, d), lambda i, j: (0, 0))],
        out_specs=pl.BlockSpec((tm, d), lambda i, j: (i, 0)),
        scratch_shapes=[pltpu.VMEM((tm, d), F32), pltpu.VMEM((2, te, tm), BF16)],
        compiler_params=_cparams(("parallel", "arbitrary")),
        name="peer_dense",
    )(x_t, u_bf, v_bf, v_bf, *prep, h2, g2, final_g)


def peer_residual(h, a_t, g2, wq_t, keys_bf, u_bf, v_bf, layer, final_g, final_norm):
    bx, lx, d = h.shape
    prep = peer_prep(a_t, wq_t, keys_bf)
    span = lx if g2.shape[0] == bx else bx * lx
    out = peer_main(a_t, u_bf, v_bf, layer, prep, h.reshape(bx * lx, d), g2, span, final_g, final_norm)
    return out.reshape(bx, lx, d)


def kernel(x, c, ctx, c_ctx, ada_w, ada_b, norm1_g, norm2_g, final_g, ab_w_in, ab_w_out, hy_conv_w, hy_conv_b,
           hy_filt_w1, hy_filt_b1, hy_filt_w2, hy_filt_b2, hy_filt_w3, hy_filt_freq, hy_skip, attn_sink,
           hg_w_in, hg_w_out, hg_lb_logits, hg_onorm_g, peer_wq, peer_keys, peer_u, peer_v):
    bsz, seq, d = x.shape
    depth = ada_w.shape[0]
    assert depth == 2
    lb_p = jax.nn.softmax(hg_lb_logits.astype(F32), axis=0)
    lb_all = jnp.cumsum(lb_p, axis=0) - lb_p[0:1]

    c16 = jnp.concatenate([c, c_ctx[None], jnp.zeros((16 - bsz - 1, d), F32)], axis=0)
    final_row = final_g[None]
    u_bf = peer_u.astype(BF16)
    v_bf = peer_v.astype(BF16)
    h_lat, h_ctx = x, ctx

    for l in range(depth):
        need_ctx = l < depth - 1
        mod = ada_mod(c16, ada_w, ada_b[:, None, :], l)
        lat = [mod[:bsz, i * d:(i + 1) * d][:, None, :] for i in range(6)]
        cx = [jnp.broadcast_to(mod[bsz:bsz + 1, i * d:(i + 1) * d][:, None, :], (bsz, 1, d)) for i in range(6)]
        n1 = norm1_g[l][None]
        n2 = norm2_g[l][None]
        wq_t = peer_wq[l].T.astype(BF16)
        keys_bf = peer_keys[l].astype(BF16)
        j = l // 2
        if l % 2 == 0:
            w_in = ab_w_in[j].astype(BF16)
            w_out = ab_w_out[j].astype(BF16)
            p_lat = norm_mod_matmul(h_lat, n1, lat[0], lat[1], w_in)
            p_ctx = norm_mod_matmul(h_ctx, n1, cx[0], cx[1], w_in)
            hy = (hy_conv_w[j], hy_conv_b[j][None], hy_filt_w1[j], hy_filt_b1[j], hy_filt_w2[j], hy_filt_b2[j],
                  hy_filt_w3[j], hy_filt_freq[j], hy_skip[j])
            cos, sin = _rope_tables(seq)
            qr, kr = rope_qk(p_lat, cos, sin, 3, 16)
            sink = attn_sink[j]
            at_lat = window_attention(qr, kr, p_lat, p_ctx, sink, 16, 17)
            hy_lat = hyena_mixer(p_lat, *hy)
            h_lat, a_lat = out_proj_residual(hy_lat, at_lat, w_out, h_lat, lat[2], n2, lat[3], lat[4])
            if need_ctx:
                at_ctx = context_attention(p_ctx, sink, 3, 16, 17)
                hy_ctx = hyena_mixer(p_ctx, *hy)
                h_ctx, a_ctx = out_proj_residual(hy_ctx, at_ctx, w_out, h_ctx, cx[2], n2, cx[3], cx[4])
        else:
            w_in = hg_w_in[j].astype(BF16)
            w_out = hg_w_out[j].astype(BF16)
            lb3 = lb_all[l].astype(F32)[:, None, :]
            p_lat = norm_mod_matmul(h_lat, n1, lat[0], lat[1], w_in)
            p_ctx = norm_mod_matmul(h_ctx, n1, cx[0], cx[1], w_in)
            s0 = jnp.zeros((bsz, HG_HEADS, HG_DK, HG_DK), F32)
            o_cf, s_cf = gla_scan(p_ctx, lb3, s0, 0, 0, 1, 3, need_o=need_ctx)
            o_cb, s_cb = gla_scan(p_ctx, lb3, s0, 1, 0, 2, 3, need_o=need_ctx)
            o_lf, _ = gla_scan(p_lat, lb3, s_cf, 0, 0, 1, 3)
            o_lb, _ = gla_scan(p_lat, lb3, s_cb, 1, 0, 2, 3)
            onorm = hg_onorm_g[j].astype(F32)[None]
            h_lat, a_lat = hgrn_readout_residual(o_lf, o_lb, p_lat, 4, onorm, w_out, h_lat, lat[2], n2, lat[3], lat[4])
            if need_ctx:
                h_ctx, a_ctx = hgrn_readout_residual(o_cf, o_cb, p_ctx, 4, onorm, w_out, h_ctx, cx[2], n2, cx[3], cx[4])
        last = l == depth - 1
        h_lat = peer_residual(h_lat, a_lat, lat[5], wq_t, keys_bf, u_bf, v_bf, l, final_row, last)
        if need_ctx:
            h_ctx = peer_residual(h_ctx, a_ctx, cx[5][:1], wq_t, keys_bf, u_bf, v_bf, l, final_row, False)
    return h_lat
```
